```python
import jax, jax.numpy as jnp
from jax import lax
import numpy as np

D_MODEL = 1024
BATCH = 4
SEQ = 4096
DEPTH = 2
DEC_BATCH = 2
DEC_SEQ = 16384
PAST_LEN = 128

GRID_W = 64
HEAD_DIM = 64
ROPE_THETA = 10000.0
EPS = 1e-6
NEG_INF = -1e30
BLOCK = 128

A_HEADS = 8
A_KV_HEADS = 2
A_GROUP = A_HEADS // A_KV_HEADS
WINDOW = 128

B_HEADS = 8
NA_MAX_H = 8
NA_W = 16

C_HEADS = 16
C_NOPE = 64
C_ROPE = 32
C_VDIM = 64
C_QK = C_NOPE + C_ROPE
Q_LORA = 384
KV_LORA = 256

D_FF = -(-8 * D_MODEL // (3 * 256)) * 256

N_EVEN = (DEPTH + 1) // 2
N_ODD = DEPTH // 2

A_QW = A_HEADS * HEAD_DIM
A_KVW = A_KV_HEADS * HEAD_DIM
B_W = B_HEADS * HEAD_DIM
AB_IN = A_QW + 2 * A_KVW + 3 * B_W
AB_OUT = A_QW + B_W
AB_SPLITS = [A_QW, A_QW + A_KVW, A_QW + 2 * A_KVW, A_QW + 2 * A_KVW + B_W, A_QW + 2 * A_KVW + 2 * B_W]
C_IN = Q_LORA + KV_LORA + C_ROPE

kernel_name = "hybrid_swa_natten_mla_encoder"


def rmsnorm(x, g):
    xf = x.astype(jnp.float32)
    y = xf * lax.rsqrt(jnp.mean(xf * xf, axis=-1, keepdims=True) + EPS)
    return (y * g.astype(jnp.float32)).astype(x.dtype)


def rope(x, pos):
    half = x.shape[-1] // 2
    inv_freq = ROPE_THETA ** (-jnp.arange(half, dtype=jnp.float32) / half)
    ang = pos.astype(jnp.float32)[:, None] * inv_freq[None, :]
    cos = jnp.cos(ang)[:, None, :]
    sin = jnp.sin(ang)[:, None, :]
    xf = x.astype(jnp.float32)
    x1, x2 = xf[..., :half], xf[..., half:]
    return jnp.concatenate([x1 * cos - x2 * sin, x2 * cos + x1 * sin], axis=-1).astype(x.dtype)


def windowed_gqa_sink(q, k, v, sink):
    bsz, seq, _, d = q.shape
    nb = seq // BLOCK
    qb = q.reshape(bsz, nb, BLOCK, A_KV_HEADS, A_GROUP, d)
    pad = ((0, 0), (BLOCK, BLOCK), (0, 0), (0, 0))
    kp = jnp.pad(k, pad).reshape(bsz, nb + 2, BLOCK, A_KV_HEADS, d)
    vp = jnp.pad(v, pad).reshape(bsz, nb + 2, BLOCK, A_KV_HEADS, d)
    kb = jnp.concatenate([kp[:, :-2], kp[:, 1:-1], kp[:, 2:]], axis=2)
    vb = jnp.concatenate([vp[:, :-2], vp[:, 1:-1], vp[:, 2:]], axis=2)
    s = jnp.einsum('bnqkgd,bnjkd->bnkgqj', qb, kb).astype(jnp.float32) * (d ** -0.5)
    qpos = jnp.arange(nb)[:, None, None] * BLOCK + jnp.arange(BLOCK)[None, :, None]
    kpos = jnp.arange(nb)[:, None, None] * BLOCK - BLOCK + jnp.arange(3 * BLOCK)[None, None, :]
    mask = (jnp.abs(qpos - kpos) <= WINDOW) & (kpos >= 0) & (kpos < seq)
    s = jnp.where(mask[None, :, None, None], s, NEG_INF)
    sk = sink.astype(jnp.float32).reshape(A_KV_HEADS, A_GROUP)[None, None, :, :, None, None]
    m = jnp.maximum(jnp.max(s, axis=-1, keepdims=True), sk)
    p = jnp.exp(s - m)
    denom = jnp.sum(p, axis=-1, keepdims=True) + jnp.exp(sk - m)
    o = jnp.einsum('bnkgqj,bnjkd->bnqkgd', (p / denom).astype(v.dtype), vb)
    return o.reshape(bsz, seq, A_HEADS * d)


def neighbourhood_attention(q, k, v, rpb):
    bsz, seq, nh, d = q.shape
    rows = seq // GRID_W
    wh = min(NA_MAX_H, rows)
    qg = q.reshape(bsz, rows, GRID_W, nh, d).transpose(1, 0, 2, 3, 4)
    kg = k.reshape(bsz, rows, GRID_W, nh, d)
    vg = v.reshape(bsz, rows, GRID_W, nh, d)
    col = jnp.arange(GRID_W)
    cs = jnp.clip(col - NA_W // 2, 0, GRID_W - NA_W)
    col_idx = cs[:, None] + jnp.arange(NA_W)[None, :]
    dc = col_idx - col[:, None] + (NA_W - 1)
    bias_c = rpb.astype(jnp.float32)[:, :, dc]
    scale = d ** -0.5

    def one_row(args):
        r, qr = args
        rs = jnp.clip(r - wh // 2, 0, rows - wh)
        kband = lax.dynamic_slice_in_dim(kg, rs, wh, axis=1)
        vband = lax.dynamic_slice_in_dim(vg, rs, wh, axis=1)
        kwin = kband[:, :, col_idx]
        vwin = vband[:, :, col_idx]
        s = jnp.einsum('bchd,bwckhd->bhcwk', qr, kwin).astype(jnp.float32) * scale
        dr = rs + jnp.arange(wh) - r + (NA_MAX_H - 1)
        bias = bias_c[:, dr].transpose(0, 2, 1, 3)
        s = s + bias[None]
        p = jax.nn.softmax(s.reshape(bsz, nh, GRID_W, wh * NA_W), axis=-1)
        p = p.reshape(bsz, nh, GRID_W, wh, NA_W).astype(v.dtype)
        return jnp.einsum('bhcwk,bwckhd->bchd', p, vwin)

    o = lax.map(one_row, (jnp.arange(rows), qg))
    return o.transpose(1, 0, 2, 3, 4).reshape(bsz, seq, nh * d)


def dense_attention(q, k, v):
    bsz, seq, nh, d = q.shape
    nb = seq // BLOCK
    qb = q.reshape(bsz, nb, BLOCK, nh, d).transpose(1, 0, 2, 3, 4)
    scale = d ** -0.5

    def blk(qi):
        s = jnp.einsum('bqhd,bkhd->bhqk', qi, k).astype(jnp.float32) * scale
        p = jax.nn.softmax(s, axis=-1).astype(v.dtype)
        return jnp.einsum('bhqk,bkhd->bqhd', p, v)

    o = lax.map(blk, qb)
    return o.transpose(1, 0, 2, 3, 4).reshape(bsz, seq, nh * v.shape[-1])


def mixer_ab(h, pos, w_in, w_out, a_q_norm, a_k_norm, a_sink, b_q_norm, b_k_norm, b_rpb):
    bsz, seq, _ = h.shape
    qa, ka, va, qb, kb, vb = jnp.split(h @ w_in, AB_SPLITS, axis=-1)
    qa = rope(rmsnorm(qa.reshape(bsz, seq, A_HEADS, HEAD_DIM), a_q_norm), pos)
    ka = rope(rmsnorm(ka.reshape(bsz, seq, A_KV_HEADS, HEAD_DIM), a_k_norm), pos)
    va = va.reshape(bsz, seq, A_KV_HEADS, HEAD_DIM)
    qb = rmsnorm(qb.reshape(bsz, seq, B_HEADS, HEAD_DIM), b_q_norm)
    kb = rmsnorm(kb.reshape(bsz, seq, B_HEADS, HEAD_DIM), b_k_norm)
    vb = vb.reshape(bsz, seq, B_HEADS, HEAD_DIM)
    oa = windowed_gqa_sink(qa, ka, va, a_sink)
    ob = neighbourhood_attention(qb, kb, vb, b_rpb)
    return jnp.concatenate([oa, ob], axis=-1) @ w_out


def mixer_mla(h, pos, w_in, q_lora_norm, kv_lora_norm, w_q_up, w_kv_up, q_norm, k_norm, w_out):
    bsz, seq, _ = h.shape
    c_q, c_kv, k_pe = jnp.split(h @ w_in, [Q_LORA, Q_LORA + KV_LORA], axis=-1)
    q = (rmsnorm(c_q, q_lora_norm) @ w_q_up).reshape(bsz, seq, C_HEADS, C_QK)
    kv = (rmsnorm(c_kv, kv_lora_norm) @ w_kv_up).reshape(bsz, seq, C_HEADS, C_NOPE + C_VDIM)
    k_nope, v = kv[..., :C_NOPE], kv[..., C_NOPE:]
    k_pe = jnp.broadcast_to(k_pe[:, :, None, :], (bsz, seq, C_HEADS, C_ROPE))
    k = jnp.concatenate([k_nope, k_pe], axis=-1)
    q = rmsnorm(q, q_norm)
    k = rmsnorm(k, k_norm)
    q = jnp.concatenate([q[..., :C_NOPE], rope(q[..., C_NOPE:], pos)], axis=-1)
    k = jnp.concatenate([k[..., :C_NOPE], rope(k[..., C_NOPE:], pos)], axis=-1)
    return dense_attention(q, k, v) @ w_out


def swiglu(h, w_gate, w_up, w_down):
    return (jax.nn.silu(h @ w_gate) * (h @ w_up)) @ w_down


def run_trunk(x, norm_mix, norm_ffn, ab_w_in, ab_w_out, a_q_norm, a_k_norm, a_sink,
              b_q_norm, b_k_norm, b_rpb, c_w_in, c_q_lora_norm, c_kv_lora_norm, c_w_q_up,
              c_w_kv_up, c_q_norm, c_k_norm, c_w_out, ffn_w_gate, ffn_w_up, ffn_w_down):
    pos = jnp.arange(x.shape[1], dtype=jnp.int32)
    for i in range(DEPTH):
        j = i // 2
        h = rmsnorm(x, norm_mix[i])
        if i % 2 == 0:
            x = x + mixer_ab(h, pos, ab_w_in[j], ab_w_out[j], a_q_norm[j], a_k_norm[j], a_sink[j],
                             b_q_norm[j], b_k_norm[j], b_rpb[j])
        else:
            x = x + mixer_mla(h, pos, c_w_in[j], c_q_lora_norm[j], c_kv_lora_norm[j], c_w_q_up[j],
                              c_w_kv_up[j], c_q_norm[j], c_k_norm[j], c_w_out[j])
        h = rmsnorm(x, norm_ffn[i])
        x = x + swiglu(h, ffn_w_gate[i], ffn_w_up[i], ffn_w_down[i])
    return x


def setup_inputs(seed: int = 0) -> dict:
    key = jax.random.key(seed)
    ks = jax.random.split(key, 23)

    def nrm(k, shape, scale):
        return jax.random.normal(k, shape, jnp.float32) * scale

    def gain(k, shape):
        return 1.0 + 0.05 * jax.random.normal(k, shape, jnp.float32)

    return {
        "x_prompt": nrm(ks[0], (BATCH, SEQ, D_MODEL), 1.0),
        "x_sample": nrm(ks[1], (DEC_BATCH, DEC_SEQ, D_MODEL), 1.0),
        "norm_mix": gain(ks[2], (DEPTH, D_MODEL)),
        "norm_ffn": gain(ks[3], (DEPTH, D_MODEL)),
        "ab_w_in": nrm(ks[4], (N_EVEN, D_MODEL, AB_IN), D_MODEL ** -0.5),
        "ab_w_out": nrm(ks[5], (N_EVEN, AB_OUT, D_MODEL), AB_OUT ** -0.5),
        "a_q_norm": gain(ks[6], (N_EVEN, HEAD_DIM)),
        "a_k_norm": gain(ks[7], (N_EVEN, HEAD_DIM)),
        "a_sink": nrm(ks[8], (N_EVEN, A_HEADS), 0.5),
        "b_q_norm": gain(ks[9], (N_EVEN, HEAD_DIM)),
        "b_k_norm": gain(ks[10], (N_EVEN, HEAD_DIM)),
        "b_rpb": nrm(ks[11], (N_EVEN, B_HEADS, 2 * NA_MAX_H - 1, 2 * NA_W - 1), 0.1),
        "c_w_in": nrm(ks[12], (N_ODD, D_MODEL, C_IN), D_MODEL ** -0.5),
        "c_q_lora_norm": gain(ks[13], (N_ODD, Q_LORA)),
        "c_kv_lora_norm": gain(ks[14], (N_ODD, KV_LORA)),
        "c_w_q_up": nrm(ks[15], (N_ODD, Q_LORA, C_HEADS * C_QK), Q_LORA ** -0.5),
        "c_w_kv_up": nrm(ks[16], (N_ODD, KV_LORA, C_HEADS * (C_NOPE + C_VDIM)), KV_LORA ** -0.5),
        "c_q_norm": gain(ks[17], (N_ODD, C_QK)),
        "c_k_norm": gain(ks[18], (N_ODD, C_QK)),
        "c_w_out": nrm(ks[19], (N_ODD, C_HEADS * C_VDIM, D_MODEL), (C_HEADS * C_VDIM) ** -0.5),
        "ffn_w_gate": nrm(ks[20], (DEPTH, D_MODEL, D_FF), D_MODEL ** -0.5),
        "ffn_w_up": nrm(ks[21], (DEPTH, D_MODEL, D_FF), D_MODEL ** -0.5),
        "ffn_w_down": nrm(ks[22], (DEPTH, D_FF, D_MODEL), D_FF ** -0.5),
    }


def reference(x_prompt, x_sample, norm_mix, norm_ffn, ab_w_in, ab_w_out, a_q_norm, a_k_norm, a_sink,
              b_q_norm, b_k_norm, b_rpb, c_w_in, c_q_lora_norm, c_kv_lora_norm, c_w_q_up, c_w_kv_up,
              c_q_norm, c_k_norm, c_w_out, ffn_w_gate, ffn_w_up, ffn_w_down):
    y_prompt = run_trunk(x_prompt, norm_mix, norm_ffn, ab_w_in, ab_w_out, a_q_norm, a_k_norm, a_sink,
                         b_q_norm, b_k_norm, b_rpb, c_w_in, c_q_lora_norm, c_kv_lora_norm, c_w_q_up,
                         c_w_kv_up, c_q_norm, c_k_norm, c_w_out, ffn_w_gate, ffn_w_up, ffn_w_down)
    y_sample = run_trunk(x_sample, norm_mix, norm_ffn, ab_w_in, ab_w_out, a_q_norm, a_k_norm, a_sink,
                         b_q_norm, b_k_norm, b_rpb, c_w_in, c_q_lora_norm, c_kv_lora_norm, c_w_q_up,
                         c_w_kv_up, c_q_norm, c_k_norm, c_w_out, ffn_w_gate, ffn_w_up, ffn_w_down)
    return (y_prompt, y_sample)
```

```python
import functools
import math

import numpy as np
import jax
import jax.numpy as jnp
from jax import lax
from jax.experimental import pallas as pl
from jax.experimental.pallas import tpu as pltpu

F32 = jnp.float32
BF16 = jnp.bfloat16

D_MODEL = 1024
GRID_W = 64
HEAD_DIM = 64
ROPE_THETA = 10000.0
EPS = 1e-6
NEG_INF = -1e30
BLOCK = 128
A_HEADS = 8
A_KV_HEADS = 2
WINDOW = 128
B_HEADS = 8
NA_MAX_H = 8
NA_W = 16
C_HEADS = 16
C_NOPE = 64
C_ROPE = 32
C_VDIM = 64
C_QK = C_NOPE + C_ROPE
Q_LORA = 384
KV_LORA = 256
D_FF = 2816
A_QW = A_HEADS * HEAD_DIM
A_KVW = A_KV_HEADS * HEAD_DIM
B_W = B_HEADS * HEAD_DIM

LANES = 128
V7X_VMEM_BYTES = 64 * 1024 * 1024
VMEM_LIMIT = 48 * 1024 * 1024

TM_PROJ = 512
TM_FFN = 256
TQ_MLA = 512
TK_MLA = TM_PROJ
NA_ROWS = 8

PERM_A = (0, 4, 1, 5, 2, 6, 3, 7)

LOG2E = math.log2(math.e)
NT_DIMS = (((1,), (1,)), ((), ()))


def _cparams(*sem):
    return pltpu.CompilerParams(dimension_semantics=sem, vmem_limit_bytes=VMEM_LIMIT)


def _resident(shape):
    nd = len(shape)
    return pl.BlockSpec(shape, lambda *_: (0,) * nd, pipeline_mode=pl.Buffered(1))


def _rms_rows(x, gain):
    ms = jnp.mean(x * x, axis=-1, keepdims=True)
    return x * lax.rsqrt(ms + EPS) * gain


def _proj_ab_kernel(x_ref, g_ref, w_ref, gsum_ref, gain_ref, cos_ref, sin_ref,
                    qa_ref, ka_ref, va_ref, qb_ref, kb_ref, vb_ref):
    h = _rms_rows(x_ref[0], g_ref[...]).astype(BF16)
    y = jnp.dot(h, w_ref[...], preferred_element_type=F32)
    gsum = gsum_ref[...]
    cos = cos_ref[...]
    sin = sin_ref[...]
    lane = lax.broadcasted_iota(jnp.int32, cos.shape, 1)
    first_half = (lane % HEAD_DIM) < (HEAD_DIM // 2)

    def head_norm(c):
        yc = y[:, 2 * LANES * c:2 * LANES * (c + 1)]
        sq = yc * yc
        hi = sq.astype(BF16)
        lo = (sq - hi.astype(F32)).astype(BF16)
        ss = (jnp.dot(hi, gsum, preferred_element_type=F32)
              + jnp.dot(lo, gsum, preferred_element_type=F32))
        return yc * lax.rsqrt(ss * (1.0 / HEAD_DIM) + EPS) * gain_ref[:, 2 * LANES * c:2 * LANES * (c + 1)]

    def rope(v):
        swapped = jnp.where(first_half, pltpu.roll(v, LANES - HEAD_DIM // 2, 1),
                            pltpu.roll(v, HEAD_DIM // 2, 1))
        return v * cos + swapped * sin

    for c in range(2):
        yn = head_norm(c)
        for b in range(2):
            qa_ref[0, :, LANES * (2 * c + b):LANES * (2 * c + b + 1)] = rope(
                yn[:, LANES * b:LANES * (b + 1)]).astype(BF16)
    for c in range(2):
        qb_ref[0, :, 2 * LANES * c:2 * LANES * (c + 1)] = head_norm(2 + c).astype(BF16)
    for c in range(2):
        kb_ref[0, :, 2 * LANES * c:2 * LANES * (c + 1)] = head_norm(4 + c).astype(BF16)
    ka_ref[0] = rope(head_norm(6)[:, :LANES]).astype(BF16)
    va_ref[0] = y[:, 13 * LANES:14 * LANES].astype(BF16)
    vb_ref[0] = y[:, 14 * LANES:18 * LANES].astype(BF16)


def _proj_ab(x, g, w, gsum, gain, cos, sin):
    bsz, seq, _ = x.shape
    tm = TM_PROJ
    nt = seq // tm
    tok = lambda width: pl.BlockSpec((1, tm, width), lambda b, i: (b, i, 0))
    tab = pl.BlockSpec((tm, LANES), lambda b, i: (i, 0))
    out = lambda width: jax.ShapeDtypeStruct((bsz, seq, width), BF16)
    return pl.pallas_call(
        _proj_ab_kernel,
        grid=(bsz, nt),
        in_specs=[tok(D_MODEL), _resident(g.shape), _resident(w.shape), _resident(gsum.shape),
                  _resident(gain.shape), tab, tab],
        out_specs=[tok(A_QW), tok(A_KVW), tok(A_KVW), tok(B_W), tok(B_W), tok(B_W)],
        out_shape=[out(A_QW), out(A_KVW), out(A_KVW), out(B_W), out(B_W), out(B_W)],
        compiler_params=_cparams("parallel", "parallel"),
        name="proj_ab",
    )(x, g, w, gsum, gain, cos, sin)


def _attn_a_kernel(q_ref, kp_ref, kc_ref, kn_ref, vp_ref, vc_ref, vn_ref, sink_ref, o_ref, *, nb):
    n = pl.program_id(1)
    q = q_ref[0]
    lane = lax.broadcasted_iota(jnp.int32, (BLOCK, LANES), 1)
    lo = lane < HEAD_DIM
    zero = jnp.zeros((BLOCK, LANES), BF16)
    parts = []
    for blk in range(A_QW // LANES):
        qp = q[:, LANES * blk:LANES * (blk + 1)]
        parts.append(jnp.where(lo, qp, zero))
        parts.append(jnp.where(lo, zero, qp))
    qm = jnp.concatenate(parts, axis=0)
    k = jnp.concatenate([kp_ref[0], kc_ref[0], kn_ref[0]], axis=0)
    v = jnp.concatenate([vp_ref[0], vc_ref[0], vn_ref[0]], axis=0)
    s = lax.dot_general(qm, k, NT_DIMS, preferred_element_type=F32)
    i = lax.broadcasted_iota(jnp.int32, s.shape, 0) % BLOCK
    j = lax.broadcasted_iota(jnp.int32, s.shape, 1)
    d = j - i
    j_lo = jnp.where(n > 0, 0, BLOCK)
    j_hi = jnp.where(n < nb - 1, 3 * BLOCK, 2 * BLOCK)
    ok = (d >= BLOCK - WINDOW) & (d <= BLOCK + WINDOW) & (j >= j_lo) & (j < j_hi)
    s = jnp.where(ok, s, NEG_INF)
    sink = sink_ref[...]
    m = jnp.maximum(jnp.max(s, axis=-1, keepdims=True), sink)
    p = jnp.exp(s - m)
    den = jnp.sum(p, axis=-1, keepdims=True) + jnp.exp(sink - m)
    o = jnp.dot(p.astype(BF16), v, preferred_element_type=F32) / den
    for blk in range(A_QW // LANES):
        o_lo = o[2 * BLOCK * blk:2 * BLOCK * blk + BLOCK]
        o_hi = o[2 * BLOCK * blk + BLOCK:2 * BLOCK * (blk + 1)]
        o_ref[0, :, LANES * blk:LANES * (blk + 1)] = jnp.where(lo, o_lo, o_hi).astype(BF16)


def _attn_a(qa, ka, va, sink_col):
    bsz, seq, _ = qa.shape
    nb = seq // BLOCK
    kv = lambda f: pl.BlockSpec((1, BLOCK, A_KVW), f)
    prev = lambda b, n: (b, jnp.maximum(n - 1, 0), 0)
    cur = lambda b, n: (b, n, 0)
    nxt = lambda b, n: (b, jnp.minimum(n + 1, nb - 1), 0)
    return pl.pallas_call(
        functools.partial(_attn_a_kernel, nb=nb),
        grid=(bsz, nb),
        in_specs=[pl.BlockSpec((1, BLOCK, A_QW), cur), kv(prev), kv(cur), kv(nxt),
                  kv(prev), kv(cur), kv(nxt), _resident(sink_col.shape)],
        out_specs=pl.BlockSpec((1, BLOCK, A_QW), cur),
        out_shape=jax.ShapeDtypeStruct((bsz, seq, A_QW), BF16),
        compiler_params=_cparams("parallel", "parallel"),
        name="attn_a",
    )(qa, ka, ka, ka, va, va, va, sink_col)


def _attn_b_kernel(q_ref, kp_ref, kc_ref, kn_ref, vp_ref, vc_ref, vn_ref, bias_ref, o_ref,
                   kbuf, vbuf, *, nrb):
    rb = pl.program_id(1)
    blk_tok = NA_ROWS * GRID_W
    half_tok = blk_tok // 2
    win_tok = NA_MAX_H * GRID_W
    kbuf[0:half_tok] = kp_ref[0, half_tok:blk_tok]
    kbuf[half_tok:half_tok + blk_tok] = kc_ref[0]
    kbuf[half_tok + blk_tok:2 * blk_tok] = kn_ref[0, 0:half_tok]
    vbuf[0:half_tok] = vp_ref[0, half_tok:blk_tok]
    vbuf[half_tok:half_tok + blk_tok] = vc_ref[0]
    vbuf[half_tok + blk_tok:2 * blk_tok] = vn_ref[0, 0:half_tok]
    lane = lax.broadcasted_iota(jnp.int32, (GRID_W, LANES), 1)
    lo = lane < HEAD_DIM
    zero = jnp.zeros((GRID_W, LANES), BF16)
    mid = NA_MAX_H // 2

    def row(t, carry):
        off = jnp.where(rb == 0, jnp.maximum(t, mid),
                        jnp.where(rb == nrb - 1, jnp.minimum(t, mid), t))
        didx = t + mid - off
        kstart = pl.multiple_of(off * GRID_W, GRID_W)
        qstart = pl.multiple_of(t * GRID_W, GRID_W)
        qrow = q_ref[0, pl.ds(qstart, GRID_W), :]
        for blk in range(B_W // LANES):
            qp = qrow[:, LANES * blk:LANES * (blk + 1)]
            qm = jnp.concatenate([jnp.where(lo, qp, zero), jnp.where(lo, zero, qp)], axis=0)
            kw = kbuf[pl.ds(kstart, win_tok), LANES * blk:LANES * (blk + 1)]
            vw = vbuf[pl.ds(kstart, win_tok), LANES * blk:LANES * (blk + 1)]
            s = lax.dot_general(qm, kw, NT_DIMS, preferred_element_type=F32)
            s = s + bias_ref[didx, blk]
            m = jnp.max(s, axis=-1, keepdims=True)
            p = jnp.exp(s - m)
            den = jnp.sum(p, axis=-1, keepdims=True)
            o = jnp.dot(p.astype(BF16), vw, preferred_element_type=F32) / den
            o_ref[0, pl.ds(qstart, GRID_W), LANES * blk:LANES * (blk + 1)] = jnp.where(
                lo, o[:GRID_W], o[GRID_W:]).astype(BF16)
        return carry

    lax.fori_loop(0, NA_ROWS, row, 0)


def _attn_b(qb, kb, vb, bias_tab):
    bsz, seq, _ = qb.shape
    blk_tok = NA_ROWS * GRID_W
    nrb = seq // blk_tok
    assert nrb >= 2, "neighbourhood attention needs at least two row blocks"
    blk = lambda f: pl.BlockSpec((1, blk_tok, B_W), f)
    prev = lambda b, n: (b, jnp.maximum(n - 1, 0), 0)
    cur = lambda b, n: (b, n, 0)
    nxt = lambda b, n: (b, jnp.minimum(n + 1, nrb - 1), 0)
    return pl.pallas_call(
        functools.partial(_attn_b_kernel, nrb=nrb),
        grid=(bsz, nrb),
        in_specs=[blk(cur), blk(prev), blk(cur), blk(nxt), blk(prev), blk(cur), blk(nxt),
                  _resident(bias_tab.shape)],
        out_specs=blk(cur),
        out_shape=jax.ShapeDtypeStruct((bsz, seq, B_W), BF16),
        scratch_shapes=[pltpu.VMEM((2 * blk_tok, B_W), BF16), pltpu.VMEM((2 * blk_tok, B_W), BF16)],
        compiler_params=_cparams("parallel", "parallel"),
        name="attn_b",
    )(qb, kb, kb, kb, vb, vb, vb, bias_tab)


def _out_ffn_kernel(*refs, n_mix):
    x_ref, o_refs, wo_refs = refs[0], refs[1:1 + n_mix], refs[1 + n_mix:1 + 2 * n_mix]
    g_ref, wg_ref, wu_ref, wd_ref, y_ref = refs[1 + 2 * n_mix:]
    x1 = x_ref[0]
    for o_ref, wo_ref in zip(o_refs, wo_refs):
        x1 = x1 + jnp.dot(o_ref[0], wo_ref[...], preferred_element_type=F32)
    h = _rms_rows(x1, g_ref[...]).astype(BF16)
    gate = jnp.dot(h, wg_ref[...], preferred_element_type=F32)
    up = jnp.dot(h, wu_ref[...], preferred_element_type=F32)
    act = (gate / (1.0 + jnp.exp(-gate)) * up).astype(BF16)
    y_ref[0] = x1 + jnp.dot(act, wd_ref[...], preferred_element_type=F32)


def _out_ffn(x, mix_outs, mix_weights, g, wg, wu, wd):
    bsz, seq, _ = x.shape
    tm = TM_FFN
    tok = lambda width: pl.BlockSpec((1, tm, width), lambda b, i: (b, i, 0))
    consts = (*mix_weights, g, wg, wu, wd)
    return pl.pallas_call(
        functools.partial(_out_ffn_kernel, n_mix=len(mix_outs)),
        grid=(bsz, seq // tm),
        in_specs=[tok(D_MODEL), *[tok(o.shape[-1]) for o in mix_outs],
                  *[_resident(c.shape) for c in consts]],
        out_specs=tok(D_MODEL),
        out_shape=jax.ShapeDtypeStruct(x.shape, F32),
        compiler_params=_cparams("parallel", "parallel"),
        name="out_ffn",
    )(x, *mix_outs, *consts)


def _proj_mla_kernel(x_ref, g_ref, win_ref, gq_ref, gkv_ref, wqt_ref, wvt_ref, wk_ref,
                     gqn_ref, gkn_ref, cosk_ref, sink_ref, cosq_ref, sinq_ref,
                     q_ref, k_ref, v_ref):
    tm = x_ref.shape[1]
    h = _rms_rows(x_ref[0], g_ref[...]).astype(BF16)
    y = jnp.dot(h, win_ref[...], preferred_element_type=F32)
    cq = _rms_rows(y[:, :Q_LORA], gq_ref[...]).astype(BF16)
    ckv = _rms_rows(y[:, Q_LORA:Q_LORA + KV_LORA], gkv_ref[...]).astype(BF16)
    kpe = y[:, Q_LORA + KV_LORA:]

    qt = lax.dot_general(wqt_ref[...], cq, NT_DIMS, preferred_element_type=F32)
    qt = qt.reshape(C_HEADS, LANES, tm)
    ssq = jnp.sum(qt * qt, axis=1, keepdims=True)
    qn = qt * lax.rsqrt(ssq * (1.0 / C_QK) + EPS) * gqn_ref[...][None]
    half = C_ROPE // 2
    r1 = qn[:, C_NOPE:C_NOPE + half]
    r2 = qn[:, C_NOPE + half:C_QK]
    cos = cosq_ref[...][None]
    sin = sinq_ref[...][None]
    q_out = jnp.concatenate([qn[:, :C_NOPE], r1 * cos - r2 * sin, r2 * cos + r1 * sin,
                             qn[:, C_QK:]], axis=1)
    q_ref[0] = q_out.astype(BF16)

    vt = lax.dot_general(wvt_ref[...], ckv, NT_DIMS, preferred_element_type=F32)
    v_ref[0, :, 0] = vt.reshape(C_HEADS, C_VDIM, tm).astype(BF16)

    kn = jnp.dot(ckv, wk_ref[...], preferred_element_type=F32)
    gk = gkn_ref[...]
    ss_pe = jnp.sum(kpe * kpe, axis=-1, keepdims=True)
    kg = kpe * gk
    lane = lax.broadcasted_iota(jnp.int32, kg.shape, 1)
    swapped = jnp.where(lane < C_NOPE + half, pltpu.roll(kg, LANES - half, 1),
                        pltpu.roll(kg, half, 1))
    kr = kg * cosk_ref[...] + swapped * sink_ref[...]
    for hd in range(C_HEADS):
        kh = kn[:, LANES * hd:LANES * (hd + 1)]
        ss = jnp.sum(kh * kh, axis=-1, keepdims=True) + ss_pe
        k_ref[0, :, LANES * hd:LANES * (hd + 1)] = (
            (kh * gk + kr) * lax.rsqrt(ss * (1.0 / C_QK) + EPS)).astype(BF16)


def _proj_mla(x, g, win, gq, gkv, wqt, wvt, wk, gqn, gkn, cosk, sink, cosq, sinq):
    bsz, seq, _ = x.shape
    tm = TM_PROJ
    nt = seq // tm
    half = C_ROPE // 2
    return pl.pallas_call(
        _proj_mla_kernel,
        grid=(bsz, nt),
        in_specs=[pl.BlockSpec((1, tm, D_MODEL), lambda b, i: (b, i, 0)),
                  _resident(g.shape), _resident(win.shape), _resident(gq.shape),
                  _resident(gkv.shape), _resident(wqt.shape), _resident(wvt.shape),
                  _resident(wk.shape), _resident(gqn.shape), _resident(gkn.shape),
                  pl.BlockSpec((tm, LANES), lambda b, i: (i, 0)),
                  pl.BlockSpec((tm, LANES), lambda b, i: (i, 0)),
                  pl.BlockSpec((half, tm), lambda b, i: (0, i)),
                  pl.BlockSpec((half, tm), lambda b, i: (0, i))],
        out_specs=[pl.BlockSpec((1, C_HEADS, LANES, tm), lambda b, i: (b, 0, 0, i)),
                   pl.BlockSpec((1, tm, C_HEADS * LANES), lambda b, i: (b, i, 0)),
                   pl.BlockSpec((1, C_HEADS, 1, C_VDIM, tm), lambda b, i: (b, 0, i, 0, 0))],
        out_shape=[jax.ShapeDtypeStruct((bsz, C_HEADS, LANES, seq), BF16),
                   jax.ShapeDtypeStruct((bsz, seq, C_HEADS * LANES), BF16),
                   jax.ShapeDtypeStruct((bsz, C_HEADS, nt, C_VDIM, tm), BF16)],
        compiler_params=_cparams("parallel", "parallel"),
        name="proj_mla",
    )(x, g, win, gq, gkv, wqt, wvt, wk, gqn, gkn, cosk, sink, cosq, sinq)


def _attn_mla_kernel(q_ref, k_ref, v_ref, o_ref, *, nchunks, tk):
    tq = q_ref.shape[-1]
    outs = []
    for hh in range(2):
        qt = q_ref[0, hh]

        def body(c, carry, qt=qt, hh=hh):
            m, l, acc = carry
            start = pl.multiple_of(c * tk, tk)
            kc = k_ref[0, pl.ds(start, tk), LANES * hh:LANES * (hh + 1)]
            s = jnp.dot(kc, qt, preferred_element_type=F32)
            m_new = jnp.maximum(m, jnp.max(s, axis=0, keepdims=True))
            alpha = jnp.exp2(m - m_new)
            p = jnp.exp2(s - m_new)
            l = alpha * l + jnp.sum(p, axis=0, keepdims=True)
            acc = alpha * acc + jnp.dot(v_ref[0, hh, c], p.astype(BF16),
                                        preferred_element_type=F32)
            return m_new, l, acc

        init = (jnp.full((1, tq), NEG_INF, F32), jnp.zeros((1, tq), F32),
                jnp.zeros((C_VDIM, tq), F32))
        _, l, acc = lax.fori_loop(0, nchunks, body, init)
        outs.append(acc / l)
    o_ref[0] = jnp.concatenate(outs, axis=0).T.astype(BF16)


def _attn_mla(qt, k, vt):
    bsz, _, _, seq = qt.shape
    nchunks, tk = vt.shape[2], vt.shape[4]
    tq = TQ_MLA
    return pl.pallas_call(
        functools.partial(_attn_mla_kernel, nchunks=nchunks, tk=tk),
        grid=(bsz, C_HEADS // 2, seq // tq),
        in_specs=[pl.BlockSpec((1, 2, LANES, tq), lambda b, h, i: (b, h, 0, i)),
                  pl.BlockSpec((1, seq, 2 * LANES), lambda b, h, i: (b, 0, h)),
                  pl.BlockSpec((1, 2, nchunks, C_VDIM, tk), lambda b, h, i: (b, h, 0, 0, 0))],
        out_specs=pl.BlockSpec((1, tq, 2 * C_VDIM), lambda b, h, i: (b, i, h)),
        out_shape=jax.ShapeDtypeStruct((bsz, seq, C_HEADS * C_VDIM), BF16),
        compiler_params=_cparams("parallel", "parallel", "arbitrary"),
        name="attn_mla",
    )(qt, k, vt)


def _rope_angles(seq, half):
    inv_freq = ROPE_THETA ** (-jnp.arange(half, dtype=F32) / half)
    return jnp.arange(seq, dtype=F32)[:, None] * inv_freq[None, :]


def _prep_ab(ab_w_in, ab_w_out, a_q_norm, a_k_norm, a_sink, b_q_norm, b_k_norm, b_rpb, seq):
    qa_cols = np.concatenate([np.arange(HEAD_DIM * h, HEAD_DIM * (h + 1)) for h in PERM_A])
    o_qb, o_kb, o_vb = A_QW + 2 * A_KVW, A_QW + 2 * A_KVW + B_W, A_QW + 2 * A_KVW + 2 * B_W
    cols = np.concatenate([qa_cols, o_qb + np.arange(B_W), o_kb + np.arange(B_W),
                           A_QW + np.arange(A_KVW), A_QW + A_KVW + np.arange(A_KVW),
                           o_vb + np.arange(B_W)])
    w_in = ab_w_in[:, cols].astype(BF16)
    woa = ab_w_out[qa_cols].astype(BF16)
    wob = ab_w_out[A_QW:].astype(BF16)
    scale = HEAD_DIM ** -0.5
    gain = jnp.concatenate([jnp.tile(a_q_norm * scale, A_HEADS), jnp.tile(b_q_norm * scale, B_HEADS),
                            jnp.tile(b_k_norm, B_HEADS), jnp.tile(a_k_norm, A_KV_HEADS),
                            jnp.ones((A_KVW,), F32)])[None, :].astype(F32)
    idx = np.arange(2 * LANES) // HEAD_DIM
    gsum = jnp.asarray(idx[:, None] == idx[None, :], BF16)
    ang = _rope_angles(seq, HEAD_DIM // 2)
    cos = jnp.tile(jnp.cos(ang), (1, LANES // (HEAD_DIM // 2)))
    sin = jnp.tile(jnp.concatenate([-jnp.sin(ang), jnp.sin(ang)], axis=1), (1, LANES // HEAD_DIM))
    sink_col = jnp.repeat(a_sink[np.array(PERM_A)], BLOCK)[:, None].astype(F32)

    c = np.arange(GRID_W)
    cs = np.clip(c - NA_W // 2, 0, GRID_W - NA_W)
    inwin = (c[None, :] >= cs[:, None]) & (c[None, :] < cs[:, None] + NA_W)
    dc = np.clip(c[None, :] - c[:, None] + NA_W - 1, 0, 2 * NA_W - 2)
    dr = np.arange(NA_MAX_H)[None, :] + (NA_MAX_H - 1) - np.arange(NA_MAX_H)[:, None]
    tab = b_rpb.astype(F32)[:, dr][:, :, :, dc]
    tab = jnp.where(inwin[None, None, None], tab, NEG_INF)
    tab = tab.transpose(1, 0, 3, 2, 4).reshape(NA_MAX_H, B_HEADS // 2, 2 * GRID_W, NA_MAX_H * GRID_W)
    return w_in, woa, wob, gain, gsum, cos, sin, sink_col, tab


def _prep_mla(c_w_in, c_q_lora_norm, c_kv_lora_norm, c_w_q_up, c_w_kv_up, c_q_norm, c_k_norm,
              c_w_out, seq):
    win = jnp.zeros((D_MODEL, Q_LORA + KV_LORA + LANES), F32)
    win = win.at[:, :Q_LORA + KV_LORA].set(c_w_in[:, :Q_LORA + KV_LORA])
    win = win.at[:, Q_LORA + KV_LORA + C_NOPE:Q_LORA + KV_LORA + C_QK].set(c_w_in[:, Q_LORA + KV_LORA:])
    wq = c_w_q_up.reshape(Q_LORA, C_HEADS, C_QK)
    wq = jnp.pad(wq, ((0, 0), (0, 0), (0, LANES - C_QK)))
    wqt = wq.reshape(Q_LORA, C_HEADS * LANES).T.astype(BF16)
    wkv = c_w_kv_up.reshape(KV_LORA, C_HEADS, C_NOPE + C_VDIM)
    wvt = wkv[:, :, C_NOPE:].reshape(KV_LORA, C_HEADS * C_VDIM).T.astype(BF16)
    wk = jnp.pad(wkv[:, :, :C_NOPE], ((0, 0), (0, 0), (0, LANES - C_NOPE)))
    wk = wk.reshape(KV_LORA, C_HEADS * LANES).astype(BF16)
    qscale = C_QK ** -0.5 * LOG2E
    gqn = jnp.pad(c_q_norm * qscale, (0, LANES - C_QK))[:, None].astype(F32)
    gkn = jnp.pad(c_k_norm, (0, LANES - C_QK))[None, :].astype(F32)
    half = C_ROPE // 2
    ang = _rope_angles(seq, half)
    cos, sin = jnp.cos(ang), jnp.sin(ang)
    zl = jnp.zeros((seq, C_NOPE), F32)
    zr = jnp.zeros((seq, LANES - C_QK), F32)
    cosk = jnp.concatenate([zl, cos, cos, zr], axis=1)
    sink = jnp.concatenate([zl, -sin, sin, zr], axis=1)
    return (win.astype(BF16), c_q_lora_norm[None, :].astype(F32), c_kv_lora_norm[None, :].astype(F32),
            wqt, wvt, wk, gqn, gkn, cosk, sink, cos.T, sin.T, c_w_out.astype(BF16))


def _trunk(x, p):
    seq = x.shape[1]
    w_in, woa, wob, gain, gsum, cos, sin, sink_col, bias_tab = _prep_ab(
        p["ab_w_in"][0], p["ab_w_out"][0], p["a_q_norm"][0], p["a_k_norm"][0], p["a_sink"][0],
        p["b_q_norm"][0], p["b_k_norm"][0], p["b_rpb"][0], seq)
    qa, ka, va, qb, kb, vb = _proj_ab(x, p["norm_mix"][0][None, :], w_in, gsum, gain, cos, sin)
    oa = _attn_a(qa, ka, va, sink_col)
    ob = _attn_b(qb, kb, vb, bias_tab)
    x = _out_ffn(x, (oa, ob), (woa, wob), p["norm_ffn"][0][None, :],
                 p["ffn_w_gate"][0].astype(BF16), p["ffn_w_up"][0].astype(BF16),
                 p["ffn_w_down"][0].astype(BF16))
    (win, gq, gkv, wqt, wvt, wk, gqn, gkn, cosk, sink, cosq, sinq, wo) = _prep_mla(
        p["c_w_in"][0], p["c_q_lora_norm"][0], p["c_kv_lora_norm"][0], p["c_w_q_up"][0],
        p["c_w_kv_up"][0], p["c_q_norm"][0], p["c_k_norm"][0], p["c_w_out"][0], seq)
    qt, k, vt = _proj_mla(x, p["norm_mix"][1][None, :], win, gq, gkv, wqt, wvt, wk, gqn, gkn,
                          cosk, sink, cosq, sinq)
    o = _attn_mla(qt, k, vt)
    x = _out_ffn(x, (o,), (wo,), p["norm_ffn"][1][None, :], p["ffn_w_gate"][1].astype(BF16),
                 p["ffn_w_up"][1].astype(BF16), p["ffn_w_down"][1].astype(BF16))
    return x


def kernel(x_prompt, x_sample, norm_mix, norm_ffn, ab_w_in, ab_w_out, a_q_norm, a_k_norm, a_sink,
           b_q_norm, b_k_norm, b_rpb, c_w_in, c_q_lora_norm, c_kv_lora_norm, c_w_q_up, c_w_kv_up,
           c_q_norm, c_k_norm, c_w_out, ffn_w_gate, ffn_w_up, ffn_w_down):
    p = dict(norm_mix=norm_mix, norm_ffn=norm_ffn, ab_w_in=ab_w_in, ab_w_out=ab_w_out,
             a_q_norm=a_q_norm, a_k_norm=a_k_norm, a_sink=a_sink, b_q_norm=b_q_norm,
             b_k_norm=b_k_norm, b_rpb=b_rpb, c_w_in=c_w_in, c_q_lora_norm=c_q_lora_norm,
             c_kv_lora_norm=c_kv_lora_norm, c_w_q_up=c_w_q_up, c_w_kv_up=c_w_kv_up,
             c_q_norm=c_q_norm, c_k_norm=c_k_norm, c_w_out=c_w_out, ffn_w_gate=ffn_w_gate,
             ffn_w_up=ffn_w_up, ffn_w_down=ffn_w_down)
    return _trunk(x_prompt, p), _trunk(x_sample, p)
```

```python
import functools
import math

import numpy as np
import jax
import jax.numpy as jnp
from jax import lax
from jax.experimental import pallas as pl
from jax.experimental.pallas import tpu as pltpu

F32 = jnp.float32
BF16 = jnp.bfloat16

D_MODEL = 1024
GRID_W = 64
HEAD_DIM = 64
ROPE_THETA = 10000.0
EPS = 1e-6
NEG_INF = -1e30
BLOCK = 128
A_HEADS = 8
A_KV_HEADS = 2
WINDOW = 128
B_HEADS = 8
NA_MAX_H = 8
NA_W = 16
C_HEADS = 16
C_NOPE = 64
C_ROPE = 32
C_VDIM = 64
C_QK = C_NOPE + C_ROPE
Q_LORA = 384
KV_LORA = 256
D_FF = 2816
A_QW = A_HEADS * HEAD_DIM
A_KVW = A_KV_HEADS * HEAD_DIM
B_W = B_HEADS * HEAD_DIM

LANES = 128
V7X_VMEM_BYTES = 64 * 1024 * 1024
VMEM_LIMIT = 48 * 1024 * 1024

TM_PROJ = 512
TM_FFN = 256
TQ_MLA = 512
TK_MLA = TM_PROJ
NA_ROWS = 8
V_ROWS = C_VDIM + 16

PERM_A = (0, 4, 1, 5, 2, 6, 3, 7)

LOG2E = math.log2(math.e)
NT_DIMS = (((1,), (1,)), ((), ()))


def _cparams(*sem):
    return pltpu.CompilerParams(dimension_semantics=sem, vmem_limit_bytes=VMEM_LIMIT)


def _resident(shape):
    nd = len(shape)
    return pl.BlockSpec(shape, lambda *_: (0,) * nd, pipeline_mode=pl.Buffered(1))


def _rms_rows(x, gain):
    ms = jnp.mean(x * x, axis=-1, keepdims=True)
    return x * lax.rsqrt(ms + EPS) * gain


def _proj_ab_kernel(x_ref, g_ref, w_ref, gsum_ref, gain_ref, cos_ref, sin_ref,
                    qa_ref, ka_ref, va_ref, qb_ref, kb_ref, vb_ref):
    h = _rms_rows(x_ref[0], g_ref[...]).astype(BF16)
    y = jnp.dot(h, w_ref[...], preferred_element_type=F32)
    gsum = gsum_ref[...]
    cos = cos_ref[...]
    sin = sin_ref[...]
    lane = lax.broadcasted_iota(jnp.int32, cos.shape, 1)
    first_half = (lane % HEAD_DIM) < (HEAD_DIM // 2)

    def head_norm(c):
        yc = y[:, 2 * LANES * c:2 * LANES * (c + 1)]
        sq = yc * yc
        hi = sq.astype(BF16)
        lo = (sq - hi.astype(F32)).astype(BF16)
        ss = (jnp.dot(hi, gsum, preferred_element_type=F32)
              + jnp.dot(lo, gsum, preferred_element_type=F32))
        return yc * lax.rsqrt(ss * (1.0 / HEAD_DIM) + EPS) * gain_ref[:, 2 * LANES * c:2 * LANES * (c + 1)]

    def rope(v):
        swapped = jnp.where(first_half, pltpu.roll(v, LANES - HEAD_DIM // 2, 1),
                            pltpu.roll(v, HEAD_DIM // 2, 1))
        return v * cos + swapped * sin

    for c in range(2):
        yn = head_norm(c)
        for b in range(2):
            qa_ref[0, :, LANES * (2 * c + b):LANES * (2 * c + b + 1)] = rope(
                yn[:, LANES * b:LANES * (b + 1)]).astype(BF16)
    for c in range(2):
        qb_ref[0, :, 2 * LANES * c:2 * LANES * (c + 1)] = head_norm(2 + c).astype(BF16)
    for c in range(2):
        kb_ref[0, :, 2 * LANES * c:2 * LANES * (c + 1)] = head_norm(4 + c).astype(BF16)
    ka_ref[0] = rope(head_norm(6)[:, :LANES]).astype(BF16)
    va_ref[0] = y[:, 13 * LANES:14 * LANES].astype(BF16)
    vb_ref[0] = y[:, 14 * LANES:18 * LANES].astype(BF16)


def _proj_ab(x, g, w, gsum, gain, cos, sin):
    bsz, seq, _ = x.shape
    tm = TM_PROJ
    nt = seq // tm
    tok = lambda width: pl.BlockSpec((1, tm, width), lambda b, i: (b, i, 0))
    tab = pl.BlockSpec((tm, LANES), lambda b, i: (i, 0))
    out = lambda width: jax.ShapeDtypeStruct((bsz, seq, width), BF16)
    return pl.pallas_call(
        _proj_ab_kernel,
        grid=(bsz, nt),
        in_specs=[tok(D_MODEL), _resident(g.shape), _resident(w.shape), _resident(gsum.shape),
                  _resident(gain.shape), tab, tab],
        out_specs=[tok(A_QW), tok(A_KVW), tok(A_KVW), tok(B_W), tok(B_W), tok(B_W)],
        out_shape=[out(A_QW), out(A_KVW), out(A_KVW), out(B_W), out(B_W), out(B_W)],
        compiler_params=_cparams("parallel", "parallel"),
        name="proj_ab",
    )(x, g, w, gsum, gain, cos, sin)


def _attn_a_kernel(q_ref, kp_ref, kc_ref, kn_ref, vp_ref, vc_ref, vn_ref, sink_ref, o_ref, *, nb):
    n = pl.program_id(1)
    q = q_ref[0]
    lane = lax.broadcasted_iota(jnp.int32, (BLOCK, LANES), 1)
    lo = lane < HEAD_DIM
    zero = jnp.zeros((BLOCK, LANES), BF16)
    parts = []
    for blk in range(A_QW // LANES):
        qp = q[:, LANES * blk:LANES * (blk + 1)]
        parts.append(jnp.where(lo, qp, zero))
        parts.append(jnp.where(lo, zero, qp))
    qm = jnp.concatenate(parts, axis=0)
    k = jnp.concatenate([kp_ref[0], kc_ref[0], kn_ref[0]], axis=0)
    v = jnp.concatenate([vp_ref[0], vc_ref[0], vn_ref[0]], axis=0)
    s = lax.dot_general(qm, k, NT_DIMS, preferred_element_type=F32)
    i = lax.broadcasted_iota(jnp.int32, s.shape, 0) % BLOCK
    j = lax.broadcasted_iota(jnp.int32, s.shape, 1)
    d = j - i
    j_lo = jnp.where(n > 0, 0, BLOCK)
    j_hi = jnp.where(n < nb - 1, 3 * BLOCK, 2 * BLOCK)
    ok = (d >= BLOCK - WINDOW) & (d <= BLOCK + WINDOW) & (j >= j_lo) & (j < j_hi)
    s = jnp.where(ok, s, NEG_INF)
    sink = sink_ref[...]
    m = jnp.maximum(jnp.max(s, axis=-1, keepdims=True), sink)
    p = jnp.exp(s - m)
    den = jnp.sum(p, axis=-1, keepdims=True) + jnp.exp(sink - m)
    o = jnp.dot(p.astype(BF16), v, preferred_element_type=F32) / den
    for blk in range(A_QW // LANES):
        o_lo = o[2 * BLOCK * blk:2 * BLOCK * blk + BLOCK]
        o_hi = o[2 * BLOCK * blk + BLOCK:2 * BLOCK * (blk + 1)]
        o_ref[0, :, LANES * blk:LANES * (blk + 1)] = jnp.where(lo, o_lo, o_hi).astype(BF16)


def _attn_a(qa, ka, va, sink_col):
    bsz, seq, _ = qa.shape
    nb = seq // BLOCK
    kv = lambda f: pl.BlockSpec((1, BLOCK, A_KVW), f)
    prev = lambda b, n: (b, jnp.maximum(n - 1, 0), 0)
    cur = lambda b, n: (b, n, 0)
    nxt = lambda b, n: (b, jnp.minimum(n + 1, nb - 1), 0)
    return pl.pallas_call(
        functools.partial(_attn_a_kernel, nb=nb),
        grid=(bsz, nb),
        in_specs=[pl.BlockSpec((1, BLOCK, A_QW), cur), kv(prev), kv(cur), kv(nxt),
                  kv(prev), kv(cur), kv(nxt), _resident(sink_col.shape)],
        out_specs=pl.BlockSpec((1, BLOCK, A_QW), cur),
        out_shape=jax.ShapeDtypeStruct((bsz, seq, A_QW), BF16),
        compiler_params=_cparams("parallel", "parallel"),
        name="attn_a",
    )(qa, ka, ka, ka, va, va, va, sink_col)


def _attn_b_kernel(q_ref, kp_ref, kc_ref, kn_ref, vp_ref, vc_ref, vn_ref, bias_ref, o_ref,
                   kbuf, vbuf, *, nrb):
    rb = pl.program_id(1)
    blk_tok = NA_ROWS * GRID_W
    half_tok = blk_tok // 2
    win_tok = NA_MAX_H * GRID_W
    kbuf[0:half_tok] = kp_ref[0, half_tok:blk_tok]
    kbuf[half_tok:half_tok + blk_tok] = kc_ref[0]
    kbuf[half_tok + blk_tok:2 * blk_tok] = kn_ref[0, 0:half_tok]
    vbuf[0:half_tok] = vp_ref[0, half_tok:blk_tok]
    vbuf[half_tok:half_tok + blk_tok] = vc_ref[0]
    vbuf[half_tok + blk_tok:2 * blk_tok] = vn_ref[0, 0:half_tok]
    lane = lax.broadcasted_iota(jnp.int32, (GRID_W, LANES), 1)
    lo = lane < HEAD_DIM
    zero = jnp.zeros((GRID_W, LANES), BF16)
    mid = NA_MAX_H // 2

    def row(t, carry):
        off = jnp.where(rb == 0, jnp.maximum(t, mid),
                        jnp.where(rb == nrb - 1, jnp.minimum(t, mid), t))
        didx = t + mid - off
        kstart = pl.multiple_of(off * GRID_W, GRID_W)
        qstart = pl.multiple_of(t * GRID_W, GRID_W)
        qrow = q_ref[0, pl.ds(qstart, GRID_W), :]
        for blk in range(B_W // LANES):
            qp = qrow[:, LANES * blk:LANES * (blk + 1)]
            qm = jnp.concatenate([jnp.where(lo, qp, zero), jnp.where(lo, zero, qp)], axis=0)
            kw = kbuf[pl.ds(kstart, win_tok), LANES * blk:LANES * (blk + 1)]
            vw = vbuf[pl.ds(kstart, win_tok), LANES * blk:LANES * (blk + 1)]
            s = lax.dot_general(qm, kw, NT_DIMS, preferred_element_type=F32)
            s = s + bias_ref[didx, blk]
            m = jnp.max(s, axis=-1, keepdims=True)
            p = jnp.exp(s - m)
            den = jnp.sum(p, axis=-1, keepdims=True)
            o = jnp.dot(p.astype(BF16), vw, preferred_element_type=F32) / den
            o_ref[0, pl.ds(qstart, GRID_W), LANES * blk:LANES * (blk + 1)] = jnp.where(
                lo, o[:GRID_W], o[GRID_W:]).astype(BF16)
        return carry

    lax.fori_loop(0, NA_ROWS, row, 0)


def _attn_b(qb, kb, vb, bias_tab):
    bsz, seq, _ = qb.shape
    blk_tok = NA_ROWS * GRID_W
    nrb = seq // blk_tok
    assert nrb >= 2, "neighbourhood attention needs at least two row blocks"
    blk = lambda f: pl.BlockSpec((1, blk_tok, B_W), f)
    prev = lambda b, n: (b, jnp.maximum(n - 1, 0), 0)
    cur = lambda b, n: (b, n, 0)
    nxt = lambda b, n: (b, jnp.minimum(n + 1, nrb - 1), 0)
    return pl.pallas_call(
        functools.partial(_attn_b_kernel, nrb=nrb),
        grid=(bsz, nrb),
        in_specs=[blk(cur), blk(prev), blk(cur), blk(nxt), blk(prev), blk(cur), blk(nxt),
                  _resident(bias_tab.shape)],
        out_specs=blk(cur),
        out_shape=jax.ShapeDtypeStruct((bsz, seq, B_W), BF16),
        scratch_shapes=[pltpu.VMEM((2 * blk_tok, B_W), BF16), pltpu.VMEM((2 * blk_tok, B_W), BF16)],
        compiler_params=_cparams("parallel", "parallel"),
        name="attn_b",
    )(qb, kb, kb, kb, vb, vb, vb, bias_tab)


def _out_ffn_kernel(*refs, n_mix):
    x_ref, o_refs, wo_refs = refs[0], refs[1:1 + n_mix], refs[1 + n_mix:1 + 2 * n_mix]
    g_ref, wg_ref, wu_ref, wd_ref, y_ref = refs[1 + 2 * n_mix:]
    x1 = x_ref[0]
    for o_ref, wo_ref in zip(o_refs, wo_refs):
        x1 = x1 + jnp.dot(o_ref[0], wo_ref[...], preferred_element_type=F32)
    h = _rms_rows(x1, g_ref[...]).astype(BF16)
    gate = jnp.dot(h, wg_ref[...], preferred_element_type=F32)
    up = jnp.dot(h, wu_ref[...], preferred_element_type=F32)
    act = (gate / (1.0 + jnp.exp(-gate)) * up).astype(BF16)
    y_ref[0] = x1 + jnp.dot(act, wd_ref[...], preferred_element_type=F32)


def _out_ffn(x, mix_outs, mix_weights, g, wg, wu, wd):
    bsz, seq, _ = x.shape
    tm = TM_FFN
    tok = lambda width: pl.BlockSpec((1, tm, width), lambda b, i: (b, i, 0))
    consts = (*mix_weights, g, wg, wu, wd)
    return pl.pallas_call(
        functools.partial(_out_ffn_kernel, n_mix=len(mix_outs)),
        grid=(bsz, seq // tm),
        in_specs=[tok(D_MODEL), *[tok(o.shape[-1]) for o in mix_outs],
                  *[_resident(c.shape) for c in consts]],
        out_specs=tok(D_MODEL),
        out_shape=jax.ShapeDtypeStruct(x.shape, F32),
        compiler_params=_cparams("parallel", "parallel"),
        name="out_ffn",
    )(x, *mix_outs, *consts)


def _proj_mla_kernel(x_ref, g_ref, win_ref, gq_ref, gkv_ref, wqt_ref, wvt_ref, wk_ref,
                     gqn_ref, gkn_ref, cosk_ref, sink_ref, cosq_ref, sinq_ref,
                     q_ref, k_ref, v_ref):
    tm = x_ref.shape[1]
    h = _rms_rows(x_ref[0], g_ref[...]).astype(BF16)
    y = jnp.dot(h, win_ref[...], preferred_element_type=F32)
    cq = _rms_rows(y[:, :Q_LORA], gq_ref[...]).astype(BF16)
    ckv = _rms_rows(y[:, Q_LORA:Q_LORA + KV_LORA], gkv_ref[...]).astype(BF16)
    kpe = y[:, Q_LORA + KV_LORA:]

    qt = lax.dot_general(wqt_ref[...], cq, NT_DIMS, preferred_element_type=F32)
    qt = qt.reshape(C_HEADS, LANES, tm)
    ssq = jnp.sum(qt * qt, axis=1, keepdims=True)
    qn = qt * lax.rsqrt(ssq * (1.0 / C_QK) + EPS) * gqn_ref[...][None]
    half = C_ROPE // 2
    r1 = qn[:, C_NOPE:C_NOPE + half]
    r2 = qn[:, C_NOPE + half:C_QK]
    cos = cosq_ref[...][None]
    sin = sinq_ref[...][None]
    q_out = jnp.concatenate([qn[:, :C_NOPE], r1 * cos - r2 * sin, r2 * cos + r1 * sin,
                             qn[:, C_QK:]], axis=1)
    q_ref[0] = q_out.astype(BF16)

    vt = lax.dot_general(wvt_ref[...], ckv, NT_DIMS, preferred_element_type=F32)
    vt = vt.reshape(C_HEADS, C_VDIM, tm).astype(BF16)
    row = lax.broadcasted_iota(jnp.int32, (C_HEADS, V_ROWS - C_VDIM, tm), 1)
    v_ref[0, :, 0] = jnp.concatenate([vt, jnp.where(row == 0, 1.0, 0.0).astype(BF16)], axis=1)

    kn = jnp.dot(ckv, wk_ref[...], preferred_element_type=F32)
    gk = gkn_ref[...]
    ss_pe = jnp.sum(kpe * kpe, axis=-1, keepdims=True)
    kg = kpe * gk
    lane = lax.broadcasted_iota(jnp.int32, kg.shape, 1)
    swapped = jnp.where(lane < C_NOPE + half, pltpu.roll(kg, LANES - half, 1),
                        pltpu.roll(kg, half, 1))
    kr = kg * cosk_ref[...] + swapped * sink_ref[...]
    for hd in range(C_HEADS):
        kh = kn[:, LANES * hd:LANES * (hd + 1)]
        ss = jnp.sum(kh * kh, axis=-1, keepdims=True) + ss_pe
        k_ref[0, :, LANES * hd:LANES * (hd + 1)] = (
            (kh * gk + kr) * lax.rsqrt(ss * (1.0 / C_QK) + EPS)).astype(BF16)


def _proj_mla(x, g, win, gq, gkv, wqt, wvt, wk, gqn, gkn, cosk, sink, cosq, sinq):
    bsz, seq, _ = x.shape
    tm = TM_PROJ
    nt = seq // tm
    half = C_ROPE // 2
    return pl.pallas_call(
        _proj_mla_kernel,
        grid=(bsz, nt),
        in_specs=[pl.BlockSpec((1, tm, D_MODEL), lambda b, i: (b, i, 0)),
                  _resident(g.shape), _resident(win.shape), _resident(gq.shape),
                  _resident(gkv.shape), _resident(wqt.shape), _resident(wvt.shape),
                  _resident(wk.shape), _resident(gqn.shape), _resident(gkn.shape),
                  pl.BlockSpec((tm, LANES), lambda b, i: (i, 0)),
                  pl.BlockSpec((tm, LANES), lambda b, i: (i, 0)),
                  pl.BlockSpec((half, tm), lambda b, i: (0, i)),
                  pl.BlockSpec((half, tm), lambda b, i: (0, i))],
        out_specs=[pl.BlockSpec((1, C_HEADS, LANES, tm), lambda b, i: (b, 0, 0, i)),
                   pl.BlockSpec((1, tm, C_HEADS * LANES), lambda b, i: (b, i, 0)),
                   pl.BlockSpec((1, C_HEADS, 1, V_ROWS, tm), lambda b, i: (b, 0, i, 0, 0))],
        out_shape=[jax.ShapeDtypeStruct((bsz, C_HEADS, LANES, seq), BF16),
                   jax.ShapeDtypeStruct((bsz, seq, C_HEADS * LANES), BF16),
                   jax.ShapeDtypeStruct((bsz, C_HEADS, nt, V_ROWS, tm), BF16)],
        compiler_params=_cparams("parallel", "parallel"),
        name="proj_mla",
    )(x, g, win, gq, gkv, wqt, wvt, wk, gqn, gkn, cosk, sink, cosq, sinq)


def _attn_mla_kernel(q_ref, k_ref, v_ref, o_ref, s_buf, cmax_buf, p_buf, alpha_buf, m_buf, acc_buf,
                     *, nchunks, tk):
    heads = range(2)

    def stage_a(c, slot):
        start = pl.multiple_of(c * tk, tk)
        for hh in heads:
            kc = k_ref[0, pl.ds(start, tk), LANES * hh:LANES * (hh + 1)]
            s = jnp.dot(kc, q_ref[0, hh], preferred_element_type=F32)
            s_buf[hh, slot] = s
            cmax_buf[hh, slot] = jnp.max(s, axis=0, keepdims=True)

    def stage_b(slot):
        for hh in heads:
            m_old = m_buf[hh]
            m_new = jnp.maximum(m_old, cmax_buf[hh, slot])
            alpha_buf[hh, slot] = jnp.exp2(m_old - m_new)
            m_buf[hh] = m_new
            p_buf[hh, slot] = jnp.exp2(s_buf[hh, slot] - m_new).astype(BF16)

    def stage_c(c, slot):
        for hh in heads:
            pv = jnp.dot(v_ref[0, hh, c], p_buf[hh, slot], preferred_element_type=F32)
            acc_buf[hh] = alpha_buf[hh, slot] * acc_buf[hh] + pv

    m_buf[...] = jnp.full(m_buf.shape, NEG_INF, F32)
    acc_buf[...] = jnp.zeros(acc_buf.shape, F32)
    p_buf[:, 1] = jnp.zeros(p_buf.shape[:1] + p_buf.shape[2:], BF16)
    alpha_buf[:, 1] = jnp.ones(alpha_buf.shape[:1] + alpha_buf.shape[2:], F32)
    stage_a(0, 0)

    def body(i, carry):
        c0 = 2 * i
        stage_b(0)
        stage_a(c0 + 1, 1)
        stage_c(jnp.maximum(c0 - 1, 0), 1)
        stage_b(1)
        stage_a(jnp.minimum(c0 + 2, nchunks - 1), 0)
        stage_c(c0, 0)
        return carry

    lax.fori_loop(0, nchunks // 2, body, 0)
    stage_c(nchunks - 1, 1)
    outs = [acc_buf[hh, :C_VDIM] / acc_buf[hh, C_VDIM:C_VDIM + 1] for hh in heads]
    o_ref[0] = jnp.concatenate(outs, axis=0).T.astype(BF16)


def _attn_mla(qt, k, vt):
    bsz, _, _, seq = qt.shape
    nchunks, tk = vt.shape[2], vt.shape[4]
    assert nchunks % 2 == 0, "the pipelined loop handles two key chunks per trip"
    tq = TQ_MLA
    return pl.pallas_call(
        functools.partial(_attn_mla_kernel, nchunks=nchunks, tk=tk),
        grid=(bsz, C_HEADS // 2, seq // tq),
        in_specs=[pl.BlockSpec((1, 2, LANES, tq), lambda b, h, i: (b, h, 0, i)),
                  pl.BlockSpec((1, seq, 2 * LANES), lambda b, h, i: (b, 0, h)),
                  pl.BlockSpec((1, 2, nchunks, V_ROWS, tk), lambda b, h, i: (b, h, 0, 0, 0))],
        out_specs=pl.BlockSpec((1, tq, 2 * C_VDIM), lambda b, h, i: (b, i, h)),
        out_shape=jax.ShapeDtypeStruct((bsz, seq, C_HEADS * C_VDIM), BF16),
        scratch_shapes=[pltpu.VMEM((2, 2, tk, tq), F32),
                        pltpu.VMEM((2, 2, 1, tq), F32),
                        pltpu.VMEM((2, 2, tk, tq), BF16),
                        pltpu.VMEM((2, 2, 1, tq), F32),
                        pltpu.VMEM((2, 1, tq), F32),
                        pltpu.VMEM((2, V_ROWS, tq), F32)],
        compiler_params=_cparams("parallel", "parallel", "arbitrary"),
        name="attn_mla",
    )(qt, k, vt)


def _rope_angles(seq, half):
    inv_freq = ROPE_THETA ** (-jnp.arange(half, dtype=F32) / half)
    return jnp.arange(seq, dtype=F32)[:, None] * inv_freq[None, :]


def _prep_ab(ab_w_in, ab_w_out, a_q_norm, a_k_norm, a_sink, b_q_norm, b_k_norm, b_rpb, seq):
    qa_cols = np.concatenate([np.arange(HEAD_DIM * h, HEAD_DIM * (h + 1)) for h in PERM_A])
    o_qb, o_kb, o_vb = A_QW + 2 * A_KVW, A_QW + 2 * A_KVW + B_W, A_QW + 2 * A_KVW + 2 * B_W
    cols = np.concatenate([qa_cols, o_qb + np.arange(B_W), o_kb + np.arange(B_W),
                           A_QW + np.arange(A_KVW), A_QW + A_KVW + np.arange(A_KVW),
                           o_vb + np.arange(B_W)])
    w_in = ab_w_in[:, cols].astype(BF16)
    woa = ab_w_out[qa_cols].astype(BF16)
    wob = ab_w_out[A_QW:].astype(BF16)
    scale = HEAD_DIM ** -0.5
    gain = jnp.concatenate([jnp.tile(a_q_norm * scale, A_HEADS), jnp.tile(b_q_norm * scale, B_HEADS),
                            jnp.tile(b_k_norm, B_HEADS), jnp.tile(a_k_norm, A_KV_HEADS),
                            jnp.ones((A_KVW,), F32)])[None, :].astype(F32)
    idx = np.arange(2 * LANES) // HEAD_DIM
    gsum = jnp.asarray(idx[:, None] == idx[None, :], BF16)
    ang = _rope_angles(seq, HEAD_DIM // 2)
    cos = jnp.tile(jnp.cos(ang), (1, LANES // (HEAD_DIM // 2)))
    sin = jnp.tile(jnp.concatenate([-jnp.sin(ang), jnp.sin(ang)], axis=1), (1, LANES // HEAD_DIM))
    sink_col = jnp.repeat(a_sink[np.array(PERM_A)], BLOCK)[:, None].astype(F32)

    c = np.arange(GRID_W)
    cs = np.clip(c - NA_W // 2, 0, GRID_W - NA_W)
    inwin = (c[None, :] >= cs[:, None]) & (c[None, :] < cs[:, None] + NA_W)
    dc = np.clip(c[None, :] - c[:, None] + NA_W - 1, 0, 2 * NA_W - 2)
    dr = np.arange(NA_MAX_H)[None, :] + (NA_MAX_H - 1) - np.arange(NA_MAX_H)[:, None]
    tab = b_rpb.astype(F32)[:, dr][:, :, :, dc]
    tab = jnp.where(inwin[None, None, None], tab, NEG_INF)
    tab = tab.transpose(1, 0, 3, 2, 4).reshape(NA_MAX_H, B_HEADS // 2, 2 * GRID_W, NA_MAX_H * GRID_W)
    return w_in, woa, wob, gain, gsum, cos, sin, sink_col, tab


def _prep_mla(c_w_in, c_q_lora_norm, c_kv_lora_norm, c_w_q_up, c_w_kv_up, c_q_norm, c_k_norm,
              c_w_out, seq):
    win = jnp.zeros((D_MODEL, Q_LORA + KV_LORA + LANES), F32)
    win = win.at[:, :Q_LORA + KV_LORA].set(c_w_in[:, :Q_LORA + KV_LORA])
    win = win.at[:, Q_LORA + KV_LORA + C_NOPE:Q_LORA + KV_LORA + C_QK].set(c_w_in[:, Q_LORA + KV_LORA:])
    wq = c_w_q_up.reshape(Q_LORA, C_HEADS, C_QK)
    wq = jnp.pad(wq, ((0, 0), (0, 0), (0, LANES - C_QK)))
    wqt = wq.reshape(Q_LORA, C_HEADS * LANES).T.astype(BF16)
    wkv = c_w_kv_up.reshape(KV_LORA, C_HEADS, C_NOPE + C_VDIM)
    wvt = wkv[:, :, C_NOPE:].reshape(KV_LORA, C_HEADS * C_VDIM).T.astype(BF16)
    wk = jnp.pad(wkv[:, :, :C_NOPE], ((0, 0), (0, 0), (0, LANES - C_NOPE)))
    wk = wk.reshape(KV_LORA, C_HEADS * LANES).astype(BF16)
    qscale = C_QK ** -0.5 * LOG2E
    gqn = jnp.pad(c_q_norm * qscale, (0, LANES - C_QK))[:, None].astype(F32)
    gkn = jnp.pad(c_k_norm, (0, LANES - C_QK))[None, :].astype(F32)
    half = C_ROPE // 2
    ang = _rope_angles(seq, half)
    cos, sin = jnp.cos(ang), jnp.sin(ang)
    zl = jnp.zeros((seq, C_NOPE), F32)
    zr = jnp.zeros((seq, LANES - C_QK), F32)
    cosk = jnp.concatenate([zl, cos, cos, zr], axis=1)
    sink = jnp.concatenate([zl, -sin, sin, zr], axis=1)
    return (win.astype(BF16), c_q_lora_norm[None, :].astype(F32), c_kv_lora_norm[None, :].astype(F32),
            wqt, wvt, wk, gqn, gkn, cosk, sink, cos.T, sin.T, c_w_out.astype(BF16))


def _trunk(x, p):
    seq = x.shape[1]
    w_in, woa, wob, gain, gsum, cos, sin, sink_col, bias_tab = _prep_ab(
        p["ab_w_in"][0], p["ab_w_out"][0], p["a_q_norm"][0], p["a_k_norm"][0], p["a_sink"][0],
        p["b_q_norm"][0], p["b_k_norm"][0], p["b_rpb"][0], seq)
    qa, ka, va, qb, kb, vb = _proj_ab(x, p["norm_mix"][0][None, :], w_in, gsum, gain, cos, sin)
    oa = _attn_a(qa, ka, va, sink_col)
    ob = _attn_b(qb, kb, vb, bias_tab)
    x = _out_ffn(x, (oa, ob), (woa, wob), p["norm_ffn"][0][None, :],
                 p["ffn_w_gate"][0].astype(BF16), p["ffn_w_up"][0].astype(BF16),
                 p["ffn_w_down"][0].astype(BF16))
    (win, gq, gkv, wqt, wvt, wk, gqn, gkn, cosk, sink, cosq, sinq, wo) = _prep_mla(
        p["c_w_in"][0], p["c_q_lora_norm"][0], p["c_kv_lora_norm"][0], p["c_w_q_up"][0],
        p["c_w_kv_up"][0], p["c_q_norm"][0], p["c_k_norm"][0], p["c_w_out"][0], seq)
    qt, k, vt = _proj_mla(x, p["norm_mix"][1][None, :], win, gq, gkv, wqt, wvt, wk, gqn, gkn,
                          cosk, sink, cosq, sinq)
    o = _attn_mla(qt, k, vt)
    x = _out_ffn(x, (o,), (wo,), p["norm_ffn"][1][None, :], p["ffn_w_gate"][1].astype(BF16),
                 p["ffn_w_up"][1].astype(BF16), p["ffn_w_down"][1].astype(BF16))
    return x


def kernel(x_prompt, x_sample, norm_mix, norm_ffn, ab_w_in, ab_w_out, a_q_norm, a_k_norm, a_sink,
           b_q_norm, b_k_norm, b_rpb, c_w_in, c_q_lora_norm, c_kv_lora_norm, c_w_q_up, c_w_kv_up,
           c_q_norm, c_k_norm, c_w_out, ffn_w_gate, ffn_w_up, ffn_w_down):
    p = dict(norm_mix=norm_mix, norm_ffn=norm_ffn, ab_w_in=ab_w_in, ab_w_out=ab_w_out,
             a_q_norm=a_q_norm, a_k_norm=a_k_norm, a_sink=a_sink, b_q_norm=b_q_norm,
             b_k_norm=b_k_norm, b_rpb=b_rpb, c_w_in=c_w_in, c_q_lora_norm=c_q_lora_norm,
             c_kv_lora_norm=c_kv_lora_norm, c_w_q_up=c_w_q_up, c_w_kv_up=c_w_kv_up,
             c_q_norm=c_q_norm, c_k_norm=c_k_norm, c_w_out=c_w_out, ffn_w_gate=ffn_w_gate,
             ffn_w_up=ffn_w_up, ffn_w_down=ffn_w_down)
    return _trunk(x_prompt, p), _trunk(x_sample, p)
```

```python
import functools
import math

import numpy as np
import jax
import jax.numpy as jnp
from jax import lax
from jax.experimental import pallas as pl
from jax.experimental.pallas import tpu as pltpu

F32 = jnp.float32
BF16 = jnp.bfloat16

D_MODEL = 1024
GRID_W = 64
HEAD_DIM = 64
ROPE_THETA = 10000.0
EPS = 1e-6
NEG_INF = -1e30
BLOCK = 128
A_HEADS = 8
A_KV_HEADS = 2
WINDOW = 128
B_HEADS = 8
NA_MAX_H = 8
NA_W = 16
C_HEADS = 16
C_NOPE = 64
C_ROPE = 32
C_VDIM = 64
C_QK = C_NOPE + C_ROPE
Q_LORA = 384
KV_LORA = 256
D_FF = 2816
A_QW = A_HEADS * HEAD_DIM
A_KVW = A_KV_HEADS * HEAD_DIM
B_W = B_HEADS * HEAD_DIM

LANES = 128
V7X_VMEM_BYTES = 64 * 1024 * 1024
VMEM_LIMIT = 48 * 1024 * 1024

TM_PROJ = 512
TM_FFN = 256
TQ_MLA = 1024
TK_MLA = TM_PROJ
NA_ROWS = 8
A_QBLOCKS = 4
V_ROWS = C_VDIM + 16

PERM_A = (0, 4, 1, 5, 2, 6, 3, 7)

LOG2E = math.log2(math.e)
NT_DIMS = (((1,), (1,)), ((), ()))


def _cparams(*sem):
    return pltpu.CompilerParams(dimension_semantics=sem, vmem_limit_bytes=VMEM_LIMIT)


def _resident(shape):
    nd = len(shape)
    return pl.BlockSpec(shape, lambda *_: (0,) * nd, pipeline_mode=pl.Buffered(1))


def _rms_rows(x, gain):
    ms = jnp.mean(x * x, axis=-1, keepdims=True)
    return x * lax.rsqrt(ms + EPS) * gain


def _proj_ab_kernel(x_ref, g_ref, w_ref, gsum_ref, gain_ref, cos_ref, sin_ref,
                    qa_ref, ka_ref, va_ref, qb_ref, kb_ref, vb_ref):
    h = _rms_rows(x_ref[0], g_ref[...]).astype(BF16)
    y = jnp.dot(h, w_ref[...], preferred_element_type=F32)
    gsum = gsum_ref[...]
    cos = cos_ref[...]
    sin = sin_ref[...]
    lane = lax.broadcasted_iota(jnp.int32, cos.shape, 1)
    first_half = (lane % HEAD_DIM) < (HEAD_DIM // 2)

    def head_norm(c):
        yc = y[:, 2 * LANES * c:2 * LANES * (c + 1)]
        sq = yc * yc
        hi = sq.astype(BF16)
        lo = (sq - hi.astype(F32)).astype(BF16)
        ss = (jnp.dot(hi, gsum, preferred_element_type=F32)
              + jnp.dot(lo, gsum, preferred_element_type=F32))
        return yc * lax.rsqrt(ss * (1.0 / HEAD_DIM) + EPS) * gain_ref[:, 2 * LANES * c:2 * LANES * (c + 1)]

    def rope(v):
        swapped = jnp.where(first_half, pltpu.roll(v, LANES - HEAD_DIM // 2, 1),
                            pltpu.roll(v, HEAD_DIM // 2, 1))
        return v * cos + swapped * sin

    for c in range(2):
        yn = head_norm(c)
        for b in range(2):
            qa_ref[0, :, LANES * (2 * c + b):LANES * (2 * c + b + 1)] = rope(
                yn[:, LANES * b:LANES * (b + 1)]).astype(BF16)
    for c in range(2):
        qb_ref[0, :, 2 * LANES * c:2 * LANES * (c + 1)] = head_norm(2 + c).astype(BF16)
    for c in range(2):
        kb_ref[0, :, 2 * LANES * c:2 * LANES * (c + 1)] = head_norm(4 + c).astype(BF16)
    ka_ref[0] = rope(head_norm(6)[:, :LANES]).astype(BF16)
    va_ref[0] = y[:, 13 * LANES:14 * LANES].astype(BF16)
    vb_ref[0] = y[:, 14 * LANES:18 * LANES].astype(BF16)


def _proj_ab(x, g, w, gsum, gain, cos, sin):
    bsz, seq, _ = x.shape
    tm = TM_PROJ
    nt = seq // tm
    tok = lambda width: pl.BlockSpec((1, tm, width), lambda b, i: (b, i, 0))
    tab = pl.BlockSpec((tm, LANES), lambda b, i: (i, 0))
    out = lambda width: jax.ShapeDtypeStruct((bsz, seq, width), BF16)
    return pl.pallas_call(
        _proj_ab_kernel,
        grid=(bsz, nt),
        in_specs=[tok(D_MODEL), _resident(g.shape), _resident(w.shape), _resident(gsum.shape),
                  _resident(gain.shape), tab, tab],
        out_specs=[tok(A_QW), tok(A_KVW), tok(A_KVW), tok(B_W), tok(B_W), tok(B_W)],
        out_shape=[out(A_QW), out(A_KVW), out(A_KVW), out(B_W), out(B_W), out(B_W)],
        compiler_params=_cparams("parallel", "parallel"),
        name="proj_ab",
    )(x, g, w, gsum, gain, cos, sin)


def _attn_a_kernel(q_ref, kp_ref, kc_ref, kn_ref, vp_ref, vc_ref, vn_ref, bias_ref, sink_ref, o_ref,
                   *, nsteps):
    n = pl.program_id(1)
    k = jnp.concatenate([kp_ref[0], kc_ref[0], kn_ref[0]], axis=0)
    v = jnp.concatenate([vp_ref[0], vc_ref[0], vn_ref[0]], axis=0)
    vext = jnp.concatenate([v, jnp.ones(v.shape, BF16)], axis=1)
    lane = lax.broadcasted_iota(jnp.int32, (BLOCK, LANES), 1)
    lo = lane < HEAD_DIM
    zero = jnp.zeros((BLOCK, LANES), BF16)
    for t in range(A_QBLOCKS):
        if t == 0:
            bias = bias_ref[jnp.where(n == 0, 0, 1)]
        elif t == A_QBLOCKS - 1:
            bias = bias_ref[jnp.where(n == nsteps - 1, 2, 1)]
        else:
            bias = bias_ref[1]
        kw = k[BLOCK * t:BLOCK * (t + 3)]
        vw = vext[BLOCK * t:BLOCK * (t + 3)]
        for blk in range(A_QW // LANES):
            qp = q_ref[0, BLOCK * t:BLOCK * (t + 1), LANES * blk:LANES * (blk + 1)]
            qm = jnp.concatenate([jnp.where(lo, qp, zero), jnp.where(lo, zero, qp)], axis=0)
            s = lax.dot_general(qm, kw, NT_DIMS, preferred_element_type=F32) + bias
            sink = sink_ref[blk]
            m = jnp.maximum(jnp.max(s, axis=-1, keepdims=True), sink)
            p = jnp.exp2(s - m)
            oe = jnp.dot(p.astype(BF16), vw, preferred_element_type=F32)
            o = oe[:, :LANES] / (oe[:, LANES:] + jnp.exp2(sink - m))
            o_ref[0, BLOCK * t:BLOCK * (t + 1), LANES * blk:LANES * (blk + 1)] = jnp.where(
                lo, o[:BLOCK], o[BLOCK:]).astype(BF16)


def _attn_a(qa, ka, va, bias_tab, sink_col):
    bsz, seq, _ = qa.shape
    nb = seq // BLOCK
    qtok = A_QBLOCKS * BLOCK
    nsteps = seq // qtok
    edge = lambda f: pl.BlockSpec((1, BLOCK, A_KVW), f)
    prev = lambda b, n: (b, jnp.maximum(A_QBLOCKS * n - 1, 0), 0)
    cur = lambda b, n: (b, n, 0)
    nxt = lambda b, n: (b, jnp.minimum(A_QBLOCKS * (n + 1), nb - 1), 0)
    mid = pl.BlockSpec((1, qtok, A_KVW), cur)
    return pl.pallas_call(
        functools.partial(_attn_a_kernel, nsteps=nsteps),
        grid=(bsz, nsteps),
        in_specs=[pl.BlockSpec((1, qtok, A_QW), cur), edge(prev), mid, edge(nxt),
                  edge(prev), mid, edge(nxt), _resident(bias_tab.shape), _resident(sink_col.shape)],
        out_specs=pl.BlockSpec((1, qtok, A_QW), cur),
        out_shape=jax.ShapeDtypeStruct((bsz, seq, A_QW), BF16),
        compiler_params=_cparams("parallel", "parallel"),
        name="attn_a",
    )(qa, ka, ka, ka, va, va, va, bias_tab, sink_col)


def _attn_b_kernel(q_ref, kp_ref, kc_ref, kn_ref, vp_ref, vc_ref, vn_ref, bias_ref, o_ref,
                   kbuf, vbuf, *, nrb):
    rb = pl.program_id(1)
    blk_tok = NA_ROWS * GRID_W
    half_tok = blk_tok // 2
    win_tok = NA_MAX_H * GRID_W
    kbuf[0:half_tok] = kp_ref[0, half_tok:blk_tok]
    kbuf[half_tok:half_tok + blk_tok] = kc_ref[0]
    kbuf[half_tok + blk_tok:2 * blk_tok] = kn_ref[0, 0:half_tok]
    vbuf[0:half_tok] = vp_ref[0, half_tok:blk_tok]
    vbuf[half_tok:half_tok + blk_tok] = vc_ref[0]
    vbuf[half_tok + blk_tok:2 * blk_tok] = vn_ref[0, 0:half_tok]
    lane = lax.broadcasted_iota(jnp.int32, (GRID_W, LANES), 1)
    lo = lane < HEAD_DIM
    zero = jnp.zeros((GRID_W, LANES), BF16)
    ones = jnp.ones((win_tok, LANES), BF16)
    mid = NA_MAX_H // 2
    for t in range(NA_ROWS):
        off = jnp.where(rb == 0, max(t, mid), jnp.where(rb == nrb - 1, min(t, mid), t))
        didx = t + mid - off
        kstart = pl.multiple_of(off * GRID_W, GRID_W)
        for blk in range(B_W // LANES):
            qp = q_ref[0, GRID_W * t:GRID_W * (t + 1), LANES * blk:LANES * (blk + 1)]
            qm = jnp.concatenate([jnp.where(lo, qp, zero), jnp.where(lo, zero, qp)], axis=0)
            kw = kbuf[pl.ds(kstart, win_tok), LANES * blk:LANES * (blk + 1)]
            vw = vbuf[pl.ds(kstart, win_tok), LANES * blk:LANES * (blk + 1)]
            s = lax.dot_general(qm, kw, NT_DIMS, preferred_element_type=F32)
            s = s + bias_ref[didx, blk]
            p = jnp.exp2(s - jnp.max(s, axis=-1, keepdims=True))
            oe = jnp.dot(p.astype(BF16), jnp.concatenate([vw, ones], axis=1),
                         preferred_element_type=F32)
            o = oe[:, :LANES] / oe[:, LANES:]
            o_ref[0, GRID_W * t:GRID_W * (t + 1), LANES * blk:LANES * (blk + 1)] = jnp.where(
                lo, o[:GRID_W], o[GRID_W:]).astype(BF16)


def _attn_b(qb, kb, vb, bias_tab):
    bsz, seq, _ = qb.shape
    blk_tok = NA_ROWS * GRID_W
    nrb = seq // blk_tok
    assert nrb >= 2, "neighbourhood attention needs at least two row blocks"
    blk = lambda f: pl.BlockSpec((1, blk_tok, B_W), f)
    prev = lambda b, n: (b, jnp.maximum(n - 1, 0), 0)
    cur = lambda b, n: (b, n, 0)
    nxt = lambda b, n: (b, jnp.minimum(n + 1, nrb - 1), 0)
    return pl.pallas_call(
        functools.partial(_attn_b_kernel, nrb=nrb),
        grid=(bsz, nrb),
        in_specs=[blk(cur), blk(prev), blk(cur), blk(nxt), blk(prev), blk(cur), blk(nxt),
                  _resident(bias_tab.shape)],
        out_specs=blk(cur),
        out_shape=jax.ShapeDtypeStruct((bsz, seq, B_W), BF16),
        scratch_shapes=[pltpu.VMEM((2 * blk_tok, B_W), BF16), pltpu.VMEM((2 * blk_tok, B_W), BF16)],
        compiler_params=_cparams("parallel", "parallel"),
        name="attn_b",
    )(qb, kb, kb, kb, vb, vb, vb, bias_tab)


def _out_ffn_kernel(*refs, n_mix):
    x_ref, o_refs, wo_refs = refs[0], refs[1:1 + n_mix], refs[1 + n_mix:1 + 2 * n_mix]
    g_ref, wg_ref, wu_ref, wd_ref, y_ref = refs[1 + 2 * n_mix:]
    x1 = x_ref[0]
    for o_ref, wo_ref in zip(o_refs, wo_refs):
        x1 = x1 + jnp.dot(o_ref[0], wo_ref[...], preferred_element_type=F32)
    h = _rms_rows(x1, g_ref[...]).astype(BF16)
    gate = jnp.dot(h, wg_ref[...], preferred_element_type=F32)
    up = jnp.dot(h, wu_ref[...], preferred_element_type=F32)
    act = (gate / (1.0 + jnp.exp(-gate)) * up).astype(BF16)
    y_ref[0] = x1 + jnp.dot(act, wd_ref[...], preferred_element_type=F32)


def _out_ffn(x, mix_outs, mix_weights, g, wg, wu, wd):
    bsz, seq, _ = x.shape
    tm = TM_FFN
    tok = lambda width: pl.BlockSpec((1, tm, width), lambda b, i: (b, i, 0))
    consts = (*mix_weights, g, wg, wu, wd)
    return pl.pallas_call(
        functools.partial(_out_ffn_kernel, n_mix=len(mix_outs)),
        grid=(bsz, seq // tm),
        in_specs=[tok(D_MODEL), *[tok(o.shape[-1]) for o in mix_outs],
                  *[_resident(c.shape) for c in consts]],
        out_specs=tok(D_MODEL),
        out_shape=jax.ShapeDtypeStruct(x.shape, F32),
        compiler_params=_cparams("parallel", "parallel"),
        name="out_ffn",
    )(x, *mix_outs, *consts)


def _proj_mla_kernel(x_ref, g_ref, win_ref, gq_ref, gkv_ref, wqt_ref, wvt_ref, wk_ref,
                     gqn_ref, gkn_ref, cosk_ref, sink_ref, cosq_ref, sinq_ref,
                     q_ref, k_ref, v_ref):
    tm = x_ref.shape[1]
    h = _rms_rows(x_ref[0], g_ref[...]).astype(BF16)
    y = jnp.dot(h, win_ref[...], preferred_element_type=F32)
    cq = _rms_rows(y[:, :Q_LORA], gq_ref[...]).astype(BF16)
    ckv = _rms_rows(y[:, Q_LORA:Q_LORA + KV_LORA], gkv_ref[...]).astype(BF16)
    kpe = y[:, Q_LORA + KV_LORA:]

    qt = lax.dot_general(wqt_ref[...], cq, NT_DIMS, preferred_element_type=F32)
    qt = qt.reshape(C_HEADS, LANES, tm)
    ssq = jnp.sum(qt * qt, axis=1, keepdims=True)
    qn = qt * lax.rsqrt(ssq * (1.0 / C_QK) + EPS) * gqn_ref[...][None]
    half = C_ROPE // 2
    r1 = qn[:, C_NOPE:C_NOPE + half]
    r2 = qn[:, C_NOPE + half:C_QK]
    cos = cosq_ref[...][None]
    sin = sinq_ref[...][None]
    q_out = jnp.concatenate([qn[:, :C_NOPE], r1 * cos - r2 * sin, r2 * cos + r1 * sin,
                             qn[:, C_QK:]], axis=1)
    q_ref[0] = q_out.astype(BF16)

    vt = lax.dot_general(wvt_ref[...], ckv, NT_DIMS, preferred_element_type=F32)
    vt = vt.reshape(C_HEADS, C_VDIM, tm).astype(BF16)
    row = lax.broadcasted_iota(jnp.int32, (C_HEADS, V_ROWS - C_VDIM, tm), 1)
    v_ref[0, :, 0] = jnp.concatenate([vt, jnp.where(row == 0, 1.0, 0.0).astype(BF16)], axis=1)

    kn = jnp.dot(ckv, wk_ref[...], preferred_element_type=F32)
    gk = gkn_ref[...]
    ss_pe = jnp.sum(kpe * kpe, axis=-1, keepdims=True)
    kg = kpe * gk
    lane = lax.broadcasted_iota(jnp.int32, kg.shape, 1)
    swapped = jnp.where(lane < C_NOPE + half, pltpu.roll(kg, LANES - half, 1),
                        pltpu.roll(kg, half, 1))
    kr = kg * cosk_ref[...] + swapped * sink_ref[...]
    for hd in range(C_HEADS):
        kh = kn[:, LANES * hd:LANES * (hd + 1)]
        ss = jnp.sum(kh * kh, axis=-1, keepdims=True) + ss_pe
        k_ref[0, :, LANES * hd:LANES * (hd + 1)] = (
            (kh * gk + kr) * lax.rsqrt(ss * (1.0 / C_QK) + EPS)).astype(BF16)


def _proj_mla(x, g, win, gq, gkv, wqt, wvt, wk, gqn, gkn, cosk, sink, cosq, sinq):
    bsz, seq, _ = x.shape
    tm = TM_PROJ
    nt = seq // tm
    half = C_ROPE // 2
    return pl.pallas_call(
        _proj_mla_kernel,
        grid=(bsz, nt),
        in_specs=[pl.BlockSpec((1, tm, D_MODEL), lambda b, i: (b, i, 0)),
                  _resident(g.shape), _resident(win.shape), _resident(gq.shape),
                  _resident(gkv.shape), _resident(wqt.shape), _resident(wvt.shape),
                  _resident(wk.shape), _resident(gqn.shape), _resident(gkn.shape),
                  pl.BlockSpec((tm, LANES), lambda b, i: (i, 0)),
                  pl.BlockSpec((tm, LANES), lambda b, i: (i, 0)),
                  pl.BlockSpec((half, tm), lambda b, i: (0, i)),
                  pl.BlockSpec((half, tm), lambda b, i: (0, i))],
        out_specs=[pl.BlockSpec((1, C_HEADS, LANES, tm), lambda b, i: (b, 0, 0, i)),
                   pl.BlockSpec((1, tm, C_HEADS * LANES), lambda b, i: (b, i, 0)),
                   pl.BlockSpec((1, C_HEADS, 1, V_ROWS, tm), lambda b, i: (b, 0, i, 0, 0))],
        out_shape=[jax.ShapeDtypeStruct((bsz, C_HEADS, LANES, seq), BF16),
                   jax.ShapeDtypeStruct((bsz, seq, C_HEADS * LANES), BF16),
                   jax.ShapeDtypeStruct((bsz, C_HEADS, nt, V_ROWS, tm), BF16)],
        compiler_params=_cparams("parallel", "parallel"),
        name="proj_mla",
    )(x, g, win, gq, gkv, wqt, wvt, wk, gqn, gkn, cosk, sink, cosq, sinq)


def _attn_mla_kernel(q_ref, k_ref, v_ref, o_ref, s_buf, cmax_buf, p_buf, alpha_buf, m_buf, acc_buf,
                     *, nchunks, tk):
    heads = range(2)

    def stage_a(c, slot):
        start = pl.multiple_of(c * tk, tk)
        for hh in heads:
            kc = k_ref[0, pl.ds(start, tk), LANES * hh:LANES * (hh + 1)]
            s = jnp.dot(kc, q_ref[0, hh], preferred_element_type=F32)
            s_buf[hh, slot] = s
            cmax_buf[hh, slot] = jnp.max(s, axis=0, keepdims=True)

    def stage_b(slot):
        for hh in heads:
            m_old = m_buf[hh]
            m_new = jnp.maximum(m_old, cmax_buf[hh, slot])
            alpha_buf[hh, slot] = jnp.exp2(m_old - m_new)
            m_buf[hh] = m_new
            p_buf[hh, slot] = jnp.exp2(s_buf[hh, slot] - m_new).astype(BF16)

    def stage_c(c, slot):
        for hh in heads:
            pv = jnp.dot(v_ref[0, hh, c], p_buf[hh, slot], preferred_element_type=F32)
            acc_buf[hh] = alpha_buf[hh, slot] * acc_buf[hh] + pv

    m_buf[...] = jnp.full(m_buf.shape, NEG_INF, F32)
    acc_buf[...] = jnp.zeros(acc_buf.shape, F32)
    p_buf[:, 1] = jnp.zeros(p_buf.shape[:1] + p_buf.shape[2:], BF16)
    alpha_buf[:, 1] = jnp.ones(alpha_buf.shape[:1] + alpha_buf.shape[2:], F32)
    stage_a(0, 0)

    def body(i, carry):
        c0 = 2 * i
        stage_b(0)
        stage_a(c0 + 1, 1)
        stage_c(jnp.maximum(c0 - 1, 0), 1)
        stage_b(1)
        stage_a(jnp.minimum(c0 + 2, nchunks - 1), 0)
        stage_c(c0, 0)
        return carry

    lax.fori_loop(0, nchunks // 2, body, 0)
    stage_c(nchunks - 1, 1)
    outs = [acc_buf[hh, :C_VDIM] / acc_buf[hh, C_VDIM:C_VDIM + 1] for hh in heads]
    o_ref[0] = jnp.concatenate(outs, axis=0).T.astype(BF16)


def _attn_mla(qt, k, vt):
    bsz, _, _, seq = qt.shape
    nchunks, tk = vt.shape[2], vt.shape[4]
    assert nchunks % 2 == 0, "the pipelined loop handles two key chunks per trip"
    tq = TQ_MLA
    return pl.pallas_call(
        functools.partial(_attn_mla_kernel, nchunks=nchunks, tk=tk),
        grid=(bsz, C_HEADS // 2, seq // tq),
        in_specs=[pl.BlockSpec((1, 2, LANES, tq), lambda b, h, i: (b, h, 0, i)),
                  pl.BlockSpec((1, seq, 2 * LANES), lambda b, h, i: (b, 0, h)),
                  pl.BlockSpec((1, 2, nchunks, V_ROWS, tk), lambda b, h, i: (b, h, 0, 0, 0))],
        out_specs=pl.BlockSpec((1, tq, 2 * C_VDIM), lambda b, h, i: (b, i, h)),
        out_shape=jax.ShapeDtypeStruct((bsz, seq, C_HEADS * C_VDIM), BF16),
        scratch_shapes=[pltpu.VMEM((2, 2, tk, tq), F32),
                        pltpu.VMEM((2, 2, 1, tq), F32),
                        pltpu.VMEM((2, 2, tk, tq), BF16),
                        pltpu.VMEM((2, 2, 1, tq), F32),
                        pltpu.VMEM((2, 1, tq), F32),
                        pltpu.VMEM((2, V_ROWS, tq), F32)],
        compiler_params=_cparams("parallel", "parallel", "arbitrary"),
        name="attn_mla",
    )(qt, k, vt)


def _rope_angles(seq, half):
    inv_freq = ROPE_THETA ** (-jnp.arange(half, dtype=F32) / half)
    return jnp.arange(seq, dtype=F32)[:, None] * inv_freq[None, :]


def _prep_ab(ab_w_in, ab_w_out, a_q_norm, a_k_norm, a_sink, b_q_norm, b_k_norm, b_rpb, seq):
    qa_cols = np.concatenate([np.arange(HEAD_DIM * h, HEAD_DIM * (h + 1)) for h in PERM_A])
    o_qb, o_kb, o_vb = A_QW + 2 * A_KVW, A_QW + 2 * A_KVW + B_W, A_QW + 2 * A_KVW + 2 * B_W
    cols = np.concatenate([qa_cols, o_qb + np.arange(B_W), o_kb + np.arange(B_W),
                           A_QW + np.arange(A_KVW), A_QW + A_KVW + np.arange(A_KVW),
                           o_vb + np.arange(B_W)])
    w_in = ab_w_in[:, cols].astype(BF16)
    woa = ab_w_out[qa_cols].astype(BF16)
    wob = ab_w_out[A_QW:].astype(BF16)
    scale = HEAD_DIM ** -0.5 * LOG2E
    gain = jnp.concatenate([jnp.tile(a_q_norm * scale, A_HEADS), jnp.tile(b_q_norm * scale, B_HEADS),
                            jnp.tile(b_k_norm, B_HEADS), jnp.tile(a_k_norm, A_KV_HEADS),
                            jnp.ones((A_KVW,), F32)])[None, :].astype(F32)
    idx = np.arange(2 * LANES) // HEAD_DIM
    gsum = jnp.asarray(idx[:, None] == idx[None, :], BF16)
    ang = _rope_angles(seq, HEAD_DIM // 2)
    cos = jnp.tile(jnp.cos(ang), (1, LANES // (HEAD_DIM // 2)))
    sin = jnp.tile(jnp.concatenate([-jnp.sin(ang), jnp.sin(ang)], axis=1), (1, LANES // HEAD_DIM))
    sink_col = jnp.repeat(a_sink[np.array(PERM_A)] * LOG2E, BLOCK).reshape(
        A_QW // LANES, 2 * BLOCK, 1).astype(F32)
    i = np.arange(2 * BLOCK)[:, None] % BLOCK
    j = np.arange(3 * BLOCK)[None, :]
    band = (j - i >= BLOCK - WINDOW) & (j - i <= BLOCK + WINDOW)
    band = np.stack([band & (j >= BLOCK), band, band & (j < 2 * BLOCK)])
    band_tab = jnp.asarray(np.where(band, 0.0, NEG_INF), F32)

    c = np.arange(GRID_W)
    cs = np.clip(c - NA_W // 2, 0, GRID_W - NA_W)
    inwin = (c[None, :] >= cs[:, None]) & (c[None, :] < cs[:, None] + NA_W)
    dc = np.clip(c[None, :] - c[:, None] + NA_W - 1, 0, 2 * NA_W - 2)
    dr = np.arange(NA_MAX_H)[None, :] + (NA_MAX_H - 1) - np.arange(NA_MAX_H)[:, None]
    tab = b_rpb.astype(F32)[:, dr][:, :, :, dc]
    tab = jnp.where(inwin[None, None, None], tab * LOG2E, NEG_INF)
    tab = tab.transpose(1, 0, 3, 2, 4).reshape(NA_MAX_H, B_HEADS // 2, 2 * GRID_W, NA_MAX_H * GRID_W)
    return w_in, woa, wob, gain, gsum, cos, sin, band_tab, sink_col, tab


def _prep_mla(c_w_in, c_q_lora_norm, c_kv_lora_norm, c_w_q_up, c_w_kv_up, c_q_norm, c_k_norm,
              c_w_out, seq):
    win = jnp.zeros((D_MODEL, Q_LORA + KV_LORA + LANES), F32)
    win = win.at[:, :Q_LORA + KV_LORA].set(c_w_in[:, :Q_LORA + KV_LORA])
    win = win.at[:, Q_LORA + KV_LORA + C_NOPE:Q_LORA + KV_LORA + C_QK].set(c_w_in[:, Q_LORA + KV_LORA:])
    wq = c_w_q_up.reshape(Q_LORA, C_HEADS, C_QK)
    wq = jnp.pad(wq, ((0, 0), (0, 0), (0, LANES - C_QK)))
    wqt = wq.reshape(Q_LORA, C_HEADS * LANES).T.astype(BF16)
    wkv = c_w_kv_up.reshape(KV_LORA, C_HEADS, C_NOPE + C_VDIM)
    wvt = wkv[:, :, C_NOPE:].reshape(KV_LORA, C_HEADS * C_VDIM).T.astype(BF16)
    wk = jnp.pad(wkv[:, :, :C_NOPE], ((0, 0), (0, 0), (0, LANES - C_NOPE)))
    wk = wk.reshape(KV_LORA, C_HEADS * LANES).astype(BF16)
    qscale = C_QK ** -0.5 * LOG2E
    gqn = jnp.pad(c_q_norm * qscale, (0, LANES - C_QK))[:, None].astype(F32)
    gkn = jnp.pad(c_k_norm, (0, LANES - C_QK))[None, :].astype(F32)
    half = C_ROPE // 2
    ang = _rope_angles(seq, half)
    cos, sin = jnp.cos(ang), jnp.sin(ang)
    zl = jnp.zeros((seq, C_NOPE), F32)
    zr = jnp.zeros((seq, LANES - C_QK), F32)
    cosk = jnp.concatenate([zl, cos, cos, zr], axis=1)
    sink = jnp.concatenate([zl, -sin, sin, zr], axis=1)
    return (win.astype(BF16), c_q_lora_norm[None, :].astype(F32), c_kv_lora_norm[None, :].astype(F32),
            wqt, wvt, wk, gqn, gkn, cosk, sink, cos.T, sin.T, c_w_out.astype(BF16))


def _trunk(x, p):
    seq = x.shape[1]
    w_in, woa, wob, gain, gsum, cos, sin, band_tab, sink_col, bias_tab = _prep_ab(
        p["ab_w_in"][0], p["ab_w_out"][0], p["a_q_norm"][0], p["a_k_norm"][0], p["a_sink"][0],
        p["b_q_norm"][0], p["b_k_norm"][0], p["b_rpb"][0], seq)
    qa, ka, va, qb, kb, vb = _proj_ab(x, p["norm_mix"][0][None, :], w_in, gsum, gain, cos, sin)
    oa = _attn_a(qa, ka, va, band_tab, sink_col)
    ob = _attn_b(qb, kb, vb, bias_tab)
    x = _out_ffn(x, (oa, ob), (woa, wob), p["norm_ffn"][0][None, :],
                 p["ffn_w_gate"][0].astype(BF16), p["ffn_w_up"][0].astype(BF16),
                 p["ffn_w_down"][0].astype(BF16))
    (win, gq, gkv, wqt, wvt, wk, gqn, gkn, cosk, sink, cosq, sinq, wo) = _prep_mla(
        p["c_w_in"][0], p["c_q_lora_norm"][0], p["c_kv_lora_norm"][0], p["c_w_q_up"][0],
        p["c_w_kv_up"][0], p["c_q_norm"][0], p["c_k_norm"][0], p["c_w_out"][0], seq)
    qt, k, vt = _proj_mla(x, p["norm_mix"][1][None, :], win, gq, gkv, wqt, wvt, wk, gqn, gkn,
                          cosk, sink, cosq, sinq)
    o = _attn_mla(qt, k, vt)
    x = _out_ffn(x, (o,), (wo,), p["norm_ffn"][1][None, :], p["ffn_w_gate"][1].astype(BF16),
                 p["ffn_w_up"][1].astype(BF16), p["ffn_w_down"][1].astype(BF16))
    return x


def kernel(x_prompt, x_sample, norm_mix, norm_ffn, ab_w_in, ab_w_out, a_q_norm, a_k_norm, a_sink,
           b_q_norm, b_k_norm, b_rpb, c_w_in, c_q_lora_norm, c_kv_lora_norm, c_w_q_up, c_w_kv_up,
           c_q_norm, c_k_norm, c_w_out, ffn_w_gate, ffn_w_up, ffn_w_down):
    p = dict(norm_mix=norm_mix, norm_ffn=norm_ffn, ab_w_in=ab_w_in, ab_w_out=ab_w_out,
             a_q_norm=a_q_norm, a_k_norm=a_k_norm, a_sink=a_sink, b_q_norm=b_q_norm,
             b_k_norm=b_k_norm, b_rpb=b_rpb, c_w_in=c_w_in, c_q_lora_norm=c_q_lora_norm,
             c_kv_lora_norm=c_kv_lora_norm, c_w_q_up=c_w_q_up, c_w_kv_up=c_w_kv_up,
             c_q_norm=c_q_norm, c_k_norm=c_k_norm, c_w_out=c_w_out, ffn_w_gate=ffn_w_gate,
             ffn_w_up=ffn_w_up, ffn_w_down=ffn_w_down)
    return _trunk(x_prompt, p), _trunk(x_sample, p)
```

```python
import functools
import math

import numpy as np
import jax
import jax.numpy as jnp
from jax import lax
from jax.experimental import pallas as pl
from jax.experimental.pallas import tpu as pltpu

F32 = jnp.float32
BF16 = jnp.bfloat16

D_MODEL = 1024
GRID_W = 64
HEAD_DIM = 64
ROPE_THETA = 10000.0
EPS = 1e-6
NEG_INF = -1e30
BLOCK = 128
A_HEADS = 8
A_KV_HEADS = 2
WINDOW = 128
B_HEADS = 8
NA_MAX_H = 8
NA_W = 16
C_HEADS = 16
C_NOPE = 64
C_ROPE = 32
C_VDIM = 64
C_QK = C_NOPE + C_ROPE
Q_LORA = 384
KV_LORA = 256
D_FF = 2816
A_QW = A_HEADS * HEAD_DIM
A_KVW = A_KV_HEADS * HEAD_DIM
B_W = B_HEADS * HEAD_DIM

LANES = 128
V7X_VMEM_BYTES = 64 * 1024 * 1024
VMEM_LIMIT = 48 * 1024 * 1024

TM_PROJ = 512
TM_FFN = 256
TQ_MLA = 1024
TK_MLA = TM_PROJ
NA_ROWS = 8
A_QBLOCKS = 4
V_ROWS = C_VDIM + 16

PERM_A = (0, 4, 1, 5, 2, 6, 3, 7)

LOG2E = math.log2(math.e)
SHIFT_MAX = 60.0
BF16_NORM_MARGIN = 1.0 + 2.0 ** -6
NT_DIMS = (((1,), (1,)), ((), ()))


def _cparams(*sem):
    return pltpu.CompilerParams(dimension_semantics=sem, vmem_limit_bytes=VMEM_LIMIT)


def _resident(shape):
    nd = len(shape)
    return pl.BlockSpec(shape, lambda *_: (0,) * nd, pipeline_mode=pl.Buffered(1))


def _rms_rows(x, gain):
    ms = jnp.mean(x * x, axis=-1, keepdims=True)
    return x * lax.rsqrt(ms + EPS) * gain


def _proj_ab_kernel(x_ref, g_ref, w_ref, gsum_ref, gain_ref, cos_ref, sin_ref,
                    qa_ref, ka_ref, va_ref, qb_ref, kb_ref, vb_ref):
    h = _rms_rows(x_ref[0], g_ref[...]).astype(BF16)
    y = jnp.dot(h, w_ref[...], preferred_element_type=F32)
    gsum = gsum_ref[...]
    cos = cos_ref[...]
    sin = sin_ref[...]
    lane = lax.broadcasted_iota(jnp.int32, cos.shape, 1)
    first_half = (lane % HEAD_DIM) < (HEAD_DIM // 2)

    def head_norm(c):
        yc = y[:, 2 * LANES * c:2 * LANES * (c + 1)]
        sq = yc * yc
        hi = sq.astype(BF16)
        lo = (sq - hi.astype(F32)).astype(BF16)
        ss = (jnp.dot(hi, gsum, preferred_element_type=F32)
              + jnp.dot(lo, gsum, preferred_element_type=F32))
        return yc * lax.rsqrt(ss * (1.0 / HEAD_DIM) + EPS) * gain_ref[:, 2 * LANES * c:2 * LANES * (c + 1)]

    def rope(v):
        swapped = jnp.where(first_half, pltpu.roll(v, LANES - HEAD_DIM // 2, 1),
                            pltpu.roll(v, HEAD_DIM // 2, 1))
        return v * cos + swapped * sin

    for c in range(2):
        yn = head_norm(c)
        for b in range(2):
            qa_ref[0, :, LANES * (2 * c + b):LANES * (2 * c + b + 1)] = rope(
                yn[:, LANES * b:LANES * (b + 1)]).astype(BF16)
    for c in range(2):
        qb_ref[0, :, 2 * LANES * c:2 * LANES * (c + 1)] = head_norm(2 + c).astype(BF16)
    for c in range(2):
        kb_ref[0, :, 2 * LANES * c:2 * LANES * (c + 1)] = head_norm(4 + c).astype(BF16)
    ka_ref[0] = rope(head_norm(6)[:, :LANES]).astype(BF16)
    va_ref[0] = y[:, 13 * LANES:14 * LANES].astype(BF16)
    vb_ref[0] = y[:, 14 * LANES:18 * LANES].astype(BF16)


def _proj_ab(x, g, w, gsum, gain, cos, sin):
    bsz, seq, _ = x.shape
    tm = TM_PROJ
    nt = seq // tm
    tok = lambda width: pl.BlockSpec((1, tm, width), lambda b, i: (b, i, 0))
    tab = pl.BlockSpec((tm, LANES), lambda b, i: (i, 0))
    out = lambda width: jax.ShapeDtypeStruct((bsz, seq, width), BF16)
    return pl.pallas_call(
        _proj_ab_kernel,
        grid=(bsz, nt),
        in_specs=[tok(D_MODEL), _resident(g.shape), _resident(w.shape), _resident(gsum.shape),
                  _resident(gain.shape), tab, tab],
        out_specs=[tok(A_QW), tok(A_KVW), tok(A_KVW), tok(B_W), tok(B_W), tok(B_W)],
        out_shape=[out(A_QW), out(A_KVW), out(A_KVW), out(B_W), out(B_W), out(B_W)],
        compiler_params=_cparams("parallel", "parallel"),
        name="proj_ab",
    )(x, g, w, gsum, gain, cos, sin)


def _attn_a_kernel(q_ref, kp_ref, kc_ref, kn_ref, vp_ref, vc_ref, vn_ref, bias_ref, sink_ref, o_ref,
                   *, nsteps):
    n = pl.program_id(1)
    k = jnp.concatenate([kp_ref[0], kc_ref[0], kn_ref[0]], axis=0)
    v = jnp.concatenate([vp_ref[0], vc_ref[0], vn_ref[0]], axis=0)
    vext = jnp.concatenate([v, jnp.ones(v.shape, BF16)], axis=1)
    lane = lax.broadcasted_iota(jnp.int32, (BLOCK, LANES), 1)
    lo = lane < HEAD_DIM
    zero = jnp.zeros((BLOCK, LANES), BF16)
    for t in range(A_QBLOCKS):
        if t == 0:
            bias = bias_ref[jnp.where(n == 0, 0, 1)]
        elif t == A_QBLOCKS - 1:
            bias = bias_ref[jnp.where(n == nsteps - 1, 2, 1)]
        else:
            bias = bias_ref[1]
        kw = k[BLOCK * t:BLOCK * (t + 3)]
        vw = vext[BLOCK * t:BLOCK * (t + 3)]
        for blk in range(A_QW // LANES):
            qp = q_ref[0, BLOCK * t:BLOCK * (t + 1), LANES * blk:LANES * (blk + 1)]
            qm = jnp.concatenate([jnp.where(lo, qp, zero), jnp.where(lo, zero, qp)], axis=0)
            s = lax.dot_general(qm, kw, NT_DIMS, preferred_element_type=F32) + bias
            sink = sink_ref[blk]
            m = jnp.maximum(jnp.max(s, axis=-1, keepdims=True), sink)
            p = jnp.exp2(s - m)
            oe = jnp.dot(p.astype(BF16), vw, preferred_element_type=F32)
            o = oe[:, :LANES] / (oe[:, LANES:] + jnp.exp2(sink - m))
            o_ref[0, BLOCK * t:BLOCK * (t + 1), LANES * blk:LANES * (blk + 1)] = jnp.where(
                lo, o[:BLOCK], o[BLOCK:]).astype(BF16)


def _attn_a(qa, ka, va, bias_tab, sink_col):
    bsz, seq, _ = qa.shape
    nb = seq // BLOCK
    qtok = A_QBLOCKS * BLOCK
    nsteps = seq // qtok
    edge = lambda f: pl.BlockSpec((1, BLOCK, A_KVW), f)
    prev = lambda b, n: (b, jnp.maximum(A_QBLOCKS * n - 1, 0), 0)
    cur = lambda b, n: (b, n, 0)
    nxt = lambda b, n: (b, jnp.minimum(A_QBLOCKS * (n + 1), nb - 1), 0)
    mid = pl.BlockSpec((1, qtok, A_KVW), cur)
    return pl.pallas_call(
        functools.partial(_attn_a_kernel, nsteps=nsteps),
        grid=(bsz, nsteps),
        in_specs=[pl.BlockSpec((1, qtok, A_QW), cur), edge(prev), mid, edge(nxt),
                  edge(prev), mid, edge(nxt), _resident(bias_tab.shape), _resident(sink_col.shape)],
        out_specs=pl.BlockSpec((1, qtok, A_QW), cur),
        out_shape=jax.ShapeDtypeStruct((bsz, seq, A_QW), BF16),
        compiler_params=_cparams("parallel", "parallel"),
        name="attn_a",
    )(qa, ka, ka, ka, va, va, va, bias_tab, sink_col)


def _attn_b_kernel(q_ref, kp_ref, kc_ref, kn_ref, vp_ref, vc_ref, vn_ref, bias_ref, o_ref,
                   kbuf, vbuf, *, nrb):
    rb = pl.program_id(1)
    blk_tok = NA_ROWS * GRID_W
    half_tok = blk_tok // 2
    win_tok = NA_MAX_H * GRID_W
    kbuf[0:half_tok] = kp_ref[0, half_tok:blk_tok]
    kbuf[half_tok:half_tok + blk_tok] = kc_ref[0]
    kbuf[half_tok + blk_tok:2 * blk_tok] = kn_ref[0, 0:half_tok]
    vbuf[0:half_tok] = vp_ref[0, half_tok:blk_tok]
    vbuf[half_tok:half_tok + blk_tok] = vc_ref[0]
    vbuf[half_tok + blk_tok:2 * blk_tok] = vn_ref[0, 0:half_tok]
    lane = lax.broadcasted_iota(jnp.int32, (GRID_W, LANES), 1)
    lo = lane < HEAD_DIM
    zero = jnp.zeros((GRID_W, LANES), BF16)
    ones = jnp.ones((win_tok, LANES), BF16)
    mid = NA_MAX_H // 2
    for t in range(NA_ROWS):
        off = jnp.where(rb == 0, max(t, mid), jnp.where(rb == nrb - 1, min(t, mid), t))
        didx = t + mid - off
        kstart = pl.multiple_of(off * GRID_W, GRID_W)
        for blk in range(B_W // LANES):
            qp = q_ref[0, GRID_W * t:GRID_W * (t + 1), LANES * blk:LANES * (blk + 1)]
            qm = jnp.concatenate([jnp.where(lo, qp, zero), jnp.where(lo, zero, qp)], axis=0)
            kw = kbuf[pl.ds(kstart, win_tok), LANES * blk:LANES * (blk + 1)]
            vw = vbuf[pl.ds(kstart, win_tok), LANES * blk:LANES * (blk + 1)]
            s = lax.dot_general(qm, kw, NT_DIMS, preferred_element_type=F32)
            s = s + bias_ref[didx, blk]
            p = jnp.exp2(s - jnp.max(s, axis=-1, keepdims=True))
            oe = jnp.dot(p.astype(BF16), jnp.concatenate([vw, ones], axis=1),
                         preferred_element_type=F32)
            o = oe[:, :LANES] / oe[:, LANES:]
            o_ref[0, GRID_W * t:GRID_W * (t + 1), LANES * blk:LANES * (blk + 1)] = jnp.where(
                lo, o[:GRID_W], o[GRID_W:]).astype(BF16)


def _attn_b(qb, kb, vb, bias_tab):
    bsz, seq, _ = qb.shape
    blk_tok = NA_ROWS * GRID_W
    nrb = seq // blk_tok
    assert nrb >= 2, "neighbourhood attention needs at least two row blocks"
    blk = lambda f: pl.BlockSpec((1, blk_tok, B_W), f)
    prev = lambda b, n: (b, jnp.maximum(n - 1, 0), 0)
    cur = lambda b, n: (b, n, 0)
    nxt = lambda b, n: (b, jnp.minimum(n + 1, nrb - 1), 0)
    return pl.pallas_call(
        functools.partial(_attn_b_kernel, nrb=nrb),
        grid=(bsz, nrb),
        in_specs=[blk(cur), blk(prev), blk(cur), blk(nxt), blk(prev), blk(cur), blk(nxt),
                  _resident(bias_tab.shape)],
        out_specs=blk(cur),
        out_shape=jax.ShapeDtypeStruct((bsz, seq, B_W), BF16),
        scratch_shapes=[pltpu.VMEM((2 * blk_tok, B_W), BF16), pltpu.VMEM((2 * blk_tok, B_W), BF16)],
        compiler_params=_cparams("parallel", "parallel"),
        name="attn_b",
    )(qb, kb, kb, kb, vb, vb, vb, bias_tab)


def _out_ffn_kernel(*refs, n_mix):
    x_ref, o_refs, wo_refs = refs[0], refs[1:1 + n_mix], refs[1 + n_mix:1 + 2 * n_mix]
    g_ref, wg_ref, wu_ref, wd_ref, y_ref = refs[1 + 2 * n_mix:]
    x1 = x_ref[0]
    for o_ref, wo_ref in zip(o_refs, wo_refs):
        x1 = x1 + jnp.dot(o_ref[0], wo_ref[...], preferred_element_type=F32)
    h = _rms_rows(x1, g_ref[...]).astype(BF16)
    gate = jnp.dot(h, wg_ref[...], preferred_element_type=F32)
    up = jnp.dot(h, wu_ref[...], preferred_element_type=F32)
    act = (gate / (1.0 + jnp.exp(-gate)) * up).astype(BF16)
    y_ref[0] = x1 + jnp.dot(act, wd_ref[...], preferred_element_type=F32)


def _out_ffn(x, mix_outs, mix_weights, g, wg, wu, wd):
    bsz, seq, _ = x.shape
    tm = TM_FFN
    tok = lambda width: pl.BlockSpec((1, tm, width), lambda b, i: (b, i, 0))
    consts = (*mix_weights, g, wg, wu, wd)
    return pl.pallas_call(
        functools.partial(_out_ffn_kernel, n_mix=len(mix_outs)),
        grid=(bsz, seq // tm),
        in_specs=[tok(D_MODEL), *[tok(o.shape[-1]) for o in mix_outs],
                  *[_resident(c.shape) for c in consts]],
        out_specs=tok(D_MODEL),
        out_shape=jax.ShapeDtypeStruct(x.shape, F32),
        compiler_params=_cparams("parallel", "parallel"),
        name="out_ffn",
    )(x, *mix_outs, *consts)


def _proj_mla_kernel(shift_ref, x_ref, g_ref, win_ref, gq_ref, gkv_ref, wqt_ref, wvt_ref, wk_ref,
                     gqn_ref, gkn_ref, cosk_ref, sink_ref, cosq_ref, sinq_ref,
                     q_ref, k_ref, v_ref):
    tm = x_ref.shape[1]
    shift = shift_ref[0]
    h = _rms_rows(x_ref[0], g_ref[...]).astype(BF16)
    y = jnp.dot(h, win_ref[...], preferred_element_type=F32)
    cq = _rms_rows(y[:, :Q_LORA], gq_ref[...]).astype(BF16)
    ckv = _rms_rows(y[:, Q_LORA:Q_LORA + KV_LORA], gkv_ref[...]).astype(BF16)
    kpe = y[:, Q_LORA + KV_LORA:]

    qt = lax.dot_general(wqt_ref[...], cq, NT_DIMS, preferred_element_type=F32)
    qt = qt.reshape(C_HEADS, LANES, tm)
    ssq = jnp.sum(qt * qt, axis=1, keepdims=True)
    qn = qt * lax.rsqrt(ssq * (1.0 / C_QK) + EPS) * gqn_ref[...][None]
    half = C_ROPE // 2
    r1 = qn[:, C_NOPE:C_NOPE + half]
    r2 = qn[:, C_NOPE + half:C_QK]
    cos = cosq_ref[...][None]
    sin = sinq_ref[...][None]
    pad_row = lax.broadcasted_iota(jnp.int32, (C_HEADS, LANES - C_QK, tm), 1)
    q_out = jnp.concatenate([qn[:, :C_NOPE], r1 * cos - r2 * sin, r2 * cos + r1 * sin,
                             jnp.where(pad_row == 0, -shift, 0.0)], axis=1)
    q_ref[0] = q_out.astype(BF16)

    vt = lax.dot_general(wvt_ref[...], ckv, NT_DIMS, preferred_element_type=F32)
    vt = vt.reshape(C_HEADS, C_VDIM, tm).astype(BF16)
    row = lax.broadcasted_iota(jnp.int32, (C_HEADS, V_ROWS - C_VDIM, tm), 1)
    v_ref[0, :, 0] = jnp.concatenate([vt, jnp.where(row == 0, 1.0, 0.0).astype(BF16)], axis=1)

    kn = jnp.dot(ckv, wk_ref[...], preferred_element_type=F32)
    gk = gkn_ref[...]
    ss_pe = jnp.sum(kpe * kpe, axis=-1, keepdims=True)
    kg = kpe * gk
    lane = lax.broadcasted_iota(jnp.int32, kg.shape, 1)
    swapped = jnp.where(lane < C_NOPE + half, pltpu.roll(kg, LANES - half, 1),
                        pltpu.roll(kg, half, 1))
    kr = kg * cosk_ref[...] + swapped * sink_ref[...]
    for hd in range(C_HEADS):
        kh = kn[:, LANES * hd:LANES * (hd + 1)]
        ss = jnp.sum(kh * kh, axis=-1, keepdims=True) + ss_pe
        kval = (kh * gk + kr) * lax.rsqrt(ss * (1.0 / C_QK) + EPS)
        k_ref[0, :, LANES * hd:LANES * (hd + 1)] = jnp.where(lane == C_QK, 1.0, kval).astype(BF16)


def _proj_mla(shift, x, g, win, gq, gkv, wqt, wvt, wk, gqn, gkn, cosk, sink, cosq, sinq):
    bsz, seq, _ = x.shape
    tm = TM_PROJ
    nt = seq // tm
    half = C_ROPE // 2
    return pl.pallas_call(
        _proj_mla_kernel,
        grid=(bsz, nt),
        in_specs=[pl.BlockSpec(memory_space=pltpu.SMEM),
                  pl.BlockSpec((1, tm, D_MODEL), lambda b, i: (b, i, 0)),
                  _resident(g.shape), _resident(win.shape), _resident(gq.shape),
                  _resident(gkv.shape), _resident(wqt.shape), _resident(wvt.shape),
                  _resident(wk.shape), _resident(gqn.shape), _resident(gkn.shape),
                  pl.BlockSpec((tm, LANES), lambda b, i: (i, 0)),
                  pl.BlockSpec((tm, LANES), lambda b, i: (i, 0)),
                  pl.BlockSpec((half, tm), lambda b, i: (0, i)),
                  pl.BlockSpec((half, tm), lambda b, i: (0, i))],
        out_specs=[pl.BlockSpec((1, C_HEADS, LANES, tm), lambda b, i: (b, 0, 0, i)),
                   pl.BlockSpec((1, tm, C_HEADS * LANES), lambda b, i: (b, i, 0)),
                   pl.BlockSpec((1, C_HEADS, 1, V_ROWS, tm), lambda b, i: (b, 0, i, 0, 0))],
        out_shape=[jax.ShapeDtypeStruct((bsz, C_HEADS, LANES, seq), BF16),
                   jax.ShapeDtypeStruct((bsz, seq, C_HEADS * LANES), BF16),
                   jax.ShapeDtypeStruct((bsz, C_HEADS, nt, V_ROWS, tm), BF16)],
        compiler_params=_cparams("parallel", "parallel"),
        name="proj_mla",
    )(shift, x, g, win, gq, gkv, wqt, wvt, wk, gqn, gkn, cosk, sink, cosq, sinq)


def _attn_mla_kernel(q_ref, k_ref, v_ref, o_ref, s_buf, cmax_buf, p_buf, alpha_buf, m_buf, acc_buf,
                     *, nchunks, tk):
    heads = range(2)

    def stage_a(c, slot):
        start = pl.multiple_of(c * tk, tk)
        for hh in heads:
            kc = k_ref[0, pl.ds(start, tk), LANES * hh:LANES * (hh + 1)]
            s = jnp.dot(kc, q_ref[0, hh], preferred_element_type=F32)
            s_buf[hh, slot] = s
            cmax_buf[hh, slot] = jnp.max(s, axis=0, keepdims=True)

    def stage_b(slot):
        for hh in heads:
            m_old = m_buf[hh]
            m_new = jnp.maximum(m_old, cmax_buf[hh, slot])
            alpha_buf[hh, slot] = jnp.exp2(m_old - m_new)
            m_buf[hh] = m_new
            p_buf[hh, slot] = jnp.exp2(s_buf[hh, slot] - m_new).astype(BF16)

    def stage_c(c, slot):
        for hh in heads:
            pv = jnp.dot(v_ref[0, hh, c], p_buf[hh, slot], preferred_element_type=F32)
            acc_buf[hh] = alpha_buf[hh, slot] * acc_buf[hh] + pv

    m_buf[...] = jnp.full(m_buf.shape, NEG_INF, F32)
    acc_buf[...] = jnp.zeros(acc_buf.shape, F32)
    p_buf[:, 1] = jnp.zeros(p_buf.shape[:1] + p_buf.shape[2:], BF16)
    alpha_buf[:, 1] = jnp.ones(alpha_buf.shape[:1] + alpha_buf.shape[2:], F32)
    stage_a(0, 0)

    def body(i, carry):
        c0 = 2 * i
        stage_b(0)
        stage_a(c0 + 1, 1)
        stage_c(jnp.maximum(c0 - 1, 0), 1)
        stage_b(1)
        stage_a(jnp.minimum(c0 + 2, nchunks - 1), 0)
        stage_c(c0, 0)
        return carry

    lax.fori_loop(0, nchunks // 2, body, 0)
    stage_c(nchunks - 1, 1)
    outs = [acc_buf[hh, :C_VDIM] / acc_buf[hh, C_VDIM:C_VDIM + 1] for hh in heads]
    o_ref[0] = jnp.concatenate(outs, axis=0).T.astype(BF16)


def _attn_mla_bounded_kernel(q_ref, k_ref, v_ref, o_ref, p_buf, acc_buf, *, nchunks, tk):
    heads = range(2)

    def stage_p(c, slot):
        start = pl.multiple_of(c * tk, tk)
        for hh in heads:
            kc = k_ref[0, pl.ds(start, tk), LANES * hh:LANES * (hh + 1)]
            s = jnp.dot(kc, q_ref[0, hh], preferred_element_type=F32)
            p_buf[hh, slot] = jnp.exp2(s).astype(BF16)

    def stage_c(c, slot):
        for hh in heads:
            acc_buf[hh] += jnp.dot(v_ref[0, hh, c], p_buf[hh, slot], preferred_element_type=F32)

    acc_buf[...] = jnp.zeros(acc_buf.shape, F32)
    stage_p(0, 0)

    def body(i, carry):
        c0 = 2 * i
        stage_p(c0 + 1, 1)
        stage_c(c0, 0)
        stage_p(jnp.minimum(c0 + 2, nchunks - 1), 0)
        stage_c(c0 + 1, 1)
        return carry

    lax.fori_loop(0, nchunks // 2, body, 0)
    outs = [acc_buf[hh, :C_VDIM] / acc_buf[hh, C_VDIM:C_VDIM + 1] for hh in heads]
    o_ref[0] = jnp.concatenate(outs, axis=0).T.astype(BF16)


def _attn_mla(qt, k, vt, bounded):
    bsz, _, _, seq = qt.shape
    nchunks, tk = vt.shape[2], vt.shape[4]
    assert nchunks % 2 == 0, "the pipelined loops handle two key chunks per trip"
    tq = TQ_MLA
    if bounded:
        body = _attn_mla_bounded_kernel
        scratch = [pltpu.VMEM((2, 2, tk, tq), BF16),
                   pltpu.VMEM((2, V_ROWS, tq), F32)]
    else:
        body = _attn_mla_kernel
        scratch = [pltpu.VMEM((2, 2, tk, tq), F32),
                   pltpu.VMEM((2, 2, 1, tq), F32),
                   pltpu.VMEM((2, 2, tk, tq), BF16),
                   pltpu.VMEM((2, 2, 1, tq), F32),
                   pltpu.VMEM((2, 1, tq), F32),
                   pltpu.VMEM((2, V_ROWS, tq), F32)]
    return pl.pallas_call(
        functools.partial(body, nchunks=nchunks, tk=tk),
        grid=(bsz, C_HEADS // 2, seq // tq),
        in_specs=[pl.BlockSpec((1, 2, LANES, tq), lambda b, h, i: (b, h, 0, i)),
                  pl.BlockSpec((1, seq, 2 * LANES), lambda b, h, i: (b, 0, h)),
                  pl.BlockSpec((1, 2, nchunks, V_ROWS, tk), lambda b, h, i: (b, h, 0, 0, 0))],
        out_specs=pl.BlockSpec((1, tq, 2 * C_VDIM), lambda b, h, i: (b, i, h)),
        out_shape=jax.ShapeDtypeStruct((bsz, seq, C_HEADS * C_VDIM), BF16),
        scratch_shapes=scratch,
        compiler_params=_cparams("parallel", "parallel", "arbitrary"),
        name="attn_mla_bounded" if bounded else "attn_mla",
    )(qt, k, vt)


def _rope_angles(seq, half):
    inv_freq = ROPE_THETA ** (-jnp.arange(half, dtype=F32) / half)
    return jnp.arange(seq, dtype=F32)[:, None] * inv_freq[None, :]


def _prep_ab(ab_w_in, ab_w_out, a_q_norm, a_k_norm, a_sink, b_q_norm, b_k_norm, b_rpb, seq):
    qa_cols = np.concatenate([np.arange(HEAD_DIM * h, HEAD_DIM * (h + 1)) for h in PERM_A])
    o_qb, o_kb, o_vb = A_QW + 2 * A_KVW, A_QW + 2 * A_KVW + B_W, A_QW + 2 * A_KVW + 2 * B_W
    cols = np.concatenate([qa_cols, o_qb + np.arange(B_W), o_kb + np.arange(B_W),
                           A_QW + np.arange(A_KVW), A_QW + A_KVW + np.arange(A_KVW),
                           o_vb + np.arange(B_W)])
    w_in = ab_w_in[:, cols].astype(BF16)
    woa = ab_w_out[qa_cols].astype(BF16)
    wob = ab_w_out[A_QW:].astype(BF16)
    scale = HEAD_DIM ** -0.5 * LOG2E
    gain = jnp.concatenate([jnp.tile(a_q_norm * scale, A_HEADS), jnp.tile(b_q_norm * scale, B_HEADS),
                            jnp.tile(b_k_norm, B_HEADS), jnp.tile(a_k_norm, A_KV_HEADS),
                            jnp.ones((A_KVW,), F32)])[None, :].astype(F32)
    idx = np.arange(2 * LANES) // HEAD_DIM
    gsum = jnp.asarray(idx[:, None] == idx[None, :], BF16)
    ang = _rope_angles(seq, HEAD_DIM // 2)
    cos = jnp.tile(jnp.cos(ang), (1, LANES // (HEAD_DIM // 2)))
    sin = jnp.tile(jnp.concatenate([-jnp.sin(ang), jnp.sin(ang)], axis=1), (1, LANES // HEAD_DIM))
    sink_col = jnp.repeat(a_sink[np.array(PERM_A)] * LOG2E, BLOCK).reshape(
        A_QW // LANES, 2 * BLOCK, 1).astype(F32)
    i = np.arange(2 * BLOCK)[:, None] % BLOCK
    j = np.arange(3 * BLOCK)[None, :]
    band = (j - i >= BLOCK - WINDOW) & (j - i <= BLOCK + WINDOW)
    band = np.stack([band & (j >= BLOCK), band, band & (j < 2 * BLOCK)])
    band_tab = jnp.asarray(np.where(band, 0.0, NEG_INF), F32)

    c = np.arange(GRID_W)
    cs = np.clip(c - NA_W // 2, 0, GRID_W - NA_W)
    inwin = (c[None, :] >= cs[:, None]) & (c[None, :] < cs[:, None] + NA_W)
    dc = np.clip(c[None, :] - c[:, None] + NA_W - 1, 0, 2 * NA_W - 2)
    dr = np.arange(NA_MAX_H)[None, :] + (NA_MAX_H - 1) - np.arange(NA_MAX_H)[:, None]
    tab = b_rpb.astype(F32)[:, dr][:, :, :, dc]
    tab = jnp.where(inwin[None, None, None], tab * LOG2E, NEG_INF)
    tab = tab.transpose(1, 0, 3, 2, 4).reshape(NA_MAX_H, B_HEADS // 2, 2 * GRID_W, NA_MAX_H * GRID_W)
    return w_in, woa, wob, gain, gsum, cos, sin, band_tab, sink_col, tab


def _prep_mla(c_w_in, c_q_lora_norm, c_kv_lora_norm, c_w_q_up, c_w_kv_up, c_q_norm, c_k_norm,
              c_w_out, seq):
    win = jnp.zeros((D_MODEL, Q_LORA + KV_LORA + LANES), F32)
    win = win.at[:, :Q_LORA + KV_LORA].set(c_w_in[:, :Q_LORA + KV_LORA])
    win = win.at[:, Q_LORA + KV_LORA + C_NOPE:Q_LORA + KV_LORA + C_QK].set(c_w_in[:, Q_LORA + KV_LORA:])
    wq = c_w_q_up.reshape(Q_LORA, C_HEADS, C_QK)
    wq = jnp.pad(wq, ((0, 0), (0, 0), (0, LANES - C_QK)))
    wqt = wq.reshape(Q_LORA, C_HEADS * LANES).T.astype(BF16)
    wkv = c_w_kv_up.reshape(KV_LORA, C_HEADS, C_NOPE + C_VDIM)
    wvt = wkv[:, :, C_NOPE:].reshape(KV_LORA, C_HEADS * C_VDIM).T.astype(BF16)
    wk = jnp.pad(wkv[:, :, :C_NOPE], ((0, 0), (0, 0), (0, LANES - C_NOPE)))
    wk = wk.reshape(KV_LORA, C_HEADS * LANES).astype(BF16)
    qscale = C_QK ** -0.5 * LOG2E
    shift = (BF16_NORM_MARGIN * C_QK * jnp.max(jnp.abs(c_q_norm * qscale)) * jnp.max(jnp.abs(c_k_norm)))
    shift = shift.reshape(1).astype(F32)
    gqn = jnp.pad(c_q_norm * qscale, (0, LANES - C_QK))[:, None].astype(F32)
    gkn = jnp.pad(c_k_norm, (0, LANES - C_QK))[None, :].astype(F32)
    half = C_ROPE // 2
    ang = _rope_angles(seq, half)
    cos, sin = jnp.cos(ang), jnp.sin(ang)
    zl = jnp.zeros((seq, C_NOPE), F32)
    zr = jnp.zeros((seq, LANES - C_QK), F32)
    cosk = jnp.concatenate([zl, cos, cos, zr], axis=1)
    sink = jnp.concatenate([zl, -sin, sin, zr], axis=1)
    return (shift, win.astype(BF16), c_q_lora_norm[None, :].astype(F32),
            c_kv_lora_norm[None, :].astype(F32), wqt, wvt, wk, gqn, gkn, cosk, sink, cos.T, sin.T,
            c_w_out.astype(BF16))


def _trunk(x, p):
    seq = x.shape[1]
    w_in, woa, wob, gain, gsum, cos, sin, band_tab, sink_col, bias_tab = _prep_ab(
        p["ab_w_in"][0], p["ab_w_out"][0], p["a_q_norm"][0], p["a_k_norm"][0], p["a_sink"][0],
        p["b_q_norm"][0], p["b_k_norm"][0], p["b_rpb"][0], seq)
    qa, ka, va, qb, kb, vb = _proj_ab(x, p["norm_mix"][0][None, :], w_in, gsum, gain, cos, sin)
    oa = _attn_a(qa, ka, va, band_tab, sink_col)
    ob = _attn_b(qb, kb, vb, bias_tab)
    x = _out_ffn(x, (oa, ob), (woa, wob), p["norm_ffn"][0][None, :],
                 p["ffn_w_gate"][0].astype(BF16), p["ffn_w_up"][0].astype(BF16),
                 p["ffn_w_down"][0].astype(BF16))
    (shift, win, gq, gkv, wqt, wvt, wk, gqn, gkn, cosk, sink, cosq, sinq, wo) = _prep_mla(
        p["c_w_in"][0], p["c_q_lora_norm"][0], p["c_kv_lora_norm"][0], p["c_w_q_up"][0],
        p["c_w_kv_up"][0], p["c_q_norm"][0], p["c_k_norm"][0], p["c_w_out"][0], seq)
    qt, k, vt = _proj_mla(shift, x, p["norm_mix"][1][None, :], win, gq, gkv, wqt, wvt, wk, gqn, gkn,
                          cosk, sink, cosq, sinq)
    o = lax.cond(shift[0] <= SHIFT_MAX,
                 functools.partial(_attn_mla, bounded=True),
                 functools.partial(_attn_mla, bounded=False), qt, k, vt)
    x = _out_ffn(x, (o,), (wo,), p["norm_ffn"][1][None, :], p["ffn_w_gate"][1].astype(BF16),
                 p["ffn_w_up"][1].astype(BF16), p["ffn_w_down"][1].astype(BF16))
    return x


def kernel(x_prompt, x_sample, norm_mix, norm_ffn, ab_w_in, ab_w_out, a_q_norm, a_k_norm, a_sink,
           b_q_norm, b_k_norm, b_rpb, c_w_in, c_q_lora_norm, c_kv_lora_norm, c_w_q_up, c_w_kv_up,
           c_q_norm, c_k_norm, c_w_out, ffn_w_gate, ffn_w_up, ffn_w_down):
    p = dict(norm_mix=norm_mix, norm_ffn=norm_ffn, ab_w_in=ab_w_in, ab_w_out=ab_w_out,
             a_q_norm=a_q_norm, a_k_norm=a_k_norm, a_sink=a_sink, b_q_norm=b_q_norm,
             b_k_norm=b_k_norm, b_rpb=b_rpb, c_w_in=c_w_in, c_q_lora_norm=c_q_lora_norm,
             c_kv_lora_norm=c_kv_lora_norm, c_w_q_up=c_w_q_up, c_w_kv_up=c_w_kv_up,
             c_q_norm=c_q_norm, c_k_norm=c_k_norm, c_w_out=c_w_out, ffn_w_gate=ffn_w_gate,
             ffn_w_up=ffn_w_up, ffn_w_down=ffn_w_down)
    return _trunk(x_prompt, p), _trunk(x_sample, p)
```

```python
import functools
import math

import numpy as np
import jax
import jax.numpy as jnp
from jax import lax
from jax.experimental import pallas as pl
from jax.experimental.pallas import tpu as pltpu

F32 = jnp.float32
BF16 = jnp.bfloat16

D_MODEL = 1024
GRID_W = 64
HEAD_DIM = 64
ROPE_THETA = 10000.0
EPS = 1e-6
NEG_INF = -1e30
BLOCK = 128
A_HEADS = 8
A_KV_HEADS = 2
WINDOW = 128
B_HEADS = 8
NA_MAX_H = 8
NA_W = 16
C_HEADS = 16
C_NOPE = 64
C_ROPE = 32
C_VDIM = 64
C_QK = C_NOPE + C_ROPE
Q_LORA = 384
KV_LORA = 256
D_FF = 2816
A_QW = A_HEADS * HEAD_DIM
A_KVW = A_KV_HEADS * HEAD_DIM
B_W = B_HEADS * HEAD_DIM

LANES = 128
V7X_VMEM_BYTES = 64 * 1024 * 1024
VMEM_LIMIT = 48 * 1024 * 1024

TM_PROJ = 512
TM_FFN = 512
TQ_MLA = 1024
TQ_MLA_BOUNDED = 2048
TK_MLA = TM_PROJ
NA_ROWS = 8
A_QBLOCKS = 4
V_ROWS = C_VDIM + 16

PERM_A = (0, 4, 1, 5, 2, 6, 3, 7)

LOG2E = math.log2(math.e)
SHIFT_MAX = 60.0
BF16_NORM_MARGIN = 1.0 + 2.0 ** -6
NT_DIMS = (((1,), (1,)), ((), ()))


def _cparams(*sem):
    return pltpu.CompilerParams(dimension_semantics=sem, vmem_limit_bytes=VMEM_LIMIT)


def _resident(shape):
    nd = len(shape)
    return pl.BlockSpec(shape, lambda *_: (0,) * nd, pipeline_mode=pl.Buffered(1))


def _rms_rows(x, gain):
    ms = jnp.mean(x * x, axis=-1, keepdims=True)
    return x * lax.rsqrt(ms + EPS) * gain


def _proj_ab_kernel(x_ref, g_ref, w_ref, gsum_ref, gain_ref, cos_ref, sin_ref,
                    qa_ref, ka_ref, va_ref, qb_ref, kb_ref, vb_ref):
    h = _rms_rows(x_ref[0], g_ref[...]).astype(BF16)
    y = jnp.dot(h, w_ref[...], preferred_element_type=F32)
    gsum = gsum_ref[...]
    cos = cos_ref[...]
    sin = sin_ref[...]
    lane = lax.broadcasted_iota(jnp.int32, cos.shape, 1)
    first_half = (lane % HEAD_DIM) < (HEAD_DIM // 2)

    def head_norm(c):
        yc = y[:, 2 * LANES * c:2 * LANES * (c + 1)]
        sq = yc * yc
        hi = sq.astype(BF16)
        lo = (sq - hi.astype(F32)).astype(BF16)
        ss = (jnp.dot(hi, gsum, preferred_element_type=F32)
              + jnp.dot(lo, gsum, preferred_element_type=F32))
        return yc * lax.rsqrt(ss * (1.0 / HEAD_DIM) + EPS) * gain_ref[:, 2 * LANES * c:2 * LANES * (c + 1)]

    def rope(v):
        swapped = jnp.where(first_half, pltpu.roll(v, LANES - HEAD_DIM // 2, 1),
                            pltpu.roll(v, HEAD_DIM // 2, 1))
        return v * cos + swapped * sin

    for c in range(2):
        yn = head_norm(c)
        for b in range(2):
            qa_ref[0, :, LANES * (2 * c + b):LANES * (2 * c + b + 1)] = rope(
                yn[:, LANES * b:LANES * (b + 1)]).astype(BF16)
    for c in range(2):
        qb_ref[0, :, 2 * LANES * c:2 * LANES * (c + 1)] = head_norm(2 + c).astype(BF16)
    for c in range(2):
        kb_ref[0, :, 2 * LANES * c:2 * LANES * (c + 1)] = head_norm(4 + c).astype(BF16)
    ka_ref[0] = rope(head_norm(6)[:, :LANES]).astype(BF16)
    va_ref[0] = y[:, 13 * LANES:14 * LANES].astype(BF16)
    vb_ref[0] = y[:, 14 * LANES:18 * LANES].astype(BF16)


def _proj_ab(x, g, w, gsum, gain, cos, sin):
    bsz, seq, _ = x.shape
    tm = TM_PROJ
    nt = seq // tm
    tok = lambda width: pl.BlockSpec((1, tm, width), lambda b, i: (b, i, 0))
    tab = pl.BlockSpec((tm, LANES), lambda b, i: (i, 0))
    out = lambda width: jax.ShapeDtypeStruct((bsz, seq, width), BF16)
    return pl.pallas_call(
        _proj_ab_kernel,
        grid=(bsz, nt),
        in_specs=[tok(D_MODEL), _resident(g.shape), _resident(w.shape), _resident(gsum.shape),
                  _resident(gain.shape), tab, tab],
        out_specs=[tok(A_QW), tok(A_KVW), tok(A_KVW), tok(B_W), tok(B_W), tok(B_W)],
        out_shape=[out(A_QW), out(A_KVW), out(A_KVW), out(B_W), out(B_W), out(B_W)],
        compiler_params=_cparams("parallel", "parallel"),
        name="proj_ab",
    )(x, g, w, gsum, gain, cos, sin)


def _attn_a_kernel(q_ref, kp_ref, kc_ref, kn_ref, vp_ref, vc_ref, vn_ref, bias_ref, sink_ref, o_ref,
                   *, nsteps):
    n = pl.program_id(1)
    k = jnp.concatenate([kp_ref[0], kc_ref[0], kn_ref[0]], axis=0)
    v = jnp.concatenate([vp_ref[0], vc_ref[0], vn_ref[0]], axis=0)
    vext = jnp.concatenate([v, jnp.ones(v.shape, BF16)], axis=1)
    lane = lax.broadcasted_iota(jnp.int32, (BLOCK, LANES), 1)
    lo = lane < HEAD_DIM
    zero = jnp.zeros((BLOCK, LANES), BF16)
    for t in range(A_QBLOCKS):
        if t == 0:
            bias = bias_ref[jnp.where(n == 0, 0, 1)]
        elif t == A_QBLOCKS - 1:
            bias = bias_ref[jnp.where(n == nsteps - 1, 2, 1)]
        else:
            bias = bias_ref[1]
        kw = k[BLOCK * t:BLOCK * (t + 3)]
        vw = vext[BLOCK * t:BLOCK * (t + 3)]
        for blk in range(A_QW // LANES):
            qp = q_ref[0, BLOCK * t:BLOCK * (t + 1), LANES * blk:LANES * (blk + 1)]
            qm = jnp.concatenate([jnp.where(lo, qp, zero), jnp.where(lo, zero, qp)], axis=0)
            s = lax.dot_general(qm, kw, NT_DIMS, preferred_element_type=F32) + bias
            sink = sink_ref[blk]
            m = jnp.maximum(jnp.max(s, axis=-1, keepdims=True), sink)
            p = jnp.exp2(s - m)
            oe = jnp.dot(p.astype(BF16), vw, preferred_element_type=F32)
            o = oe[:, :LANES] / (oe[:, LANES:] + jnp.exp2(sink - m))
            o_ref[0, BLOCK * t:BLOCK * (t + 1), LANES * blk:LANES * (blk + 1)] = jnp.where(
                lo, o[:BLOCK], o[BLOCK:]).astype(BF16)


def _attn_a(qa, ka, va, bias_tab, sink_col):
    bsz, seq, _ = qa.shape
    nb = seq // BLOCK
    qtok = A_QBLOCKS * BLOCK
    nsteps = seq // qtok
    edge = lambda f: pl.BlockSpec((1, BLOCK, A_KVW), f)
    prev = lambda b, n: (b, jnp.maximum(A_QBLOCKS * n - 1, 0), 0)
    cur = lambda b, n: (b, n, 0)
    nxt = lambda b, n: (b, jnp.minimum(A_QBLOCKS * (n + 1), nb - 1), 0)
    mid = pl.BlockSpec((1, qtok, A_KVW), cur)
    return pl.pallas_call(
        functools.partial(_attn_a_kernel, nsteps=nsteps),
        grid=(bsz, nsteps),
        in_specs=[pl.BlockSpec((1, qtok, A_QW), cur), edge(prev), mid, edge(nxt),
                  edge(prev), mid, edge(nxt), _resident(bias_tab.shape), _resident(sink_col.shape)],
        out_specs=pl.BlockSpec((1, qtok, A_QW), cur),
        out_shape=jax.ShapeDtypeStruct((bsz, seq, A_QW), BF16),
        compiler_params=_cparams("parallel", "parallel"),
        name="attn_a",
    )(qa, ka, ka, ka, va, va, va, bias_tab, sink_col)


def _attn_b_kernel(q_ref, kp_ref, kc_ref, kn_ref, vp_ref, vc_ref, vn_ref, bias_ref, o_ref,
                   kbuf, vbuf, *, nrb):
    rb = pl.program_id(1)
    blk_tok = NA_ROWS * GRID_W
    half_tok = blk_tok // 2
    win_tok = NA_MAX_H * GRID_W
    kbuf[0:half_tok] = kp_ref[0, half_tok:blk_tok]
    kbuf[half_tok:half_tok + blk_tok] = kc_ref[0]
    kbuf[half_tok + blk_tok:2 * blk_tok] = kn_ref[0, 0:half_tok]
    vbuf[0:half_tok] = vp_ref[0, half_tok:blk_tok]
    vbuf[half_tok:half_tok + blk_tok] = vc_ref[0]
    vbuf[half_tok + blk_tok:2 * blk_tok] = vn_ref[0, 0:half_tok]
    lane = lax.broadcasted_iota(jnp.int32, (GRID_W, LANES), 1)
    lo = lane < HEAD_DIM
    zero = jnp.zeros((GRID_W, LANES), BF16)
    ones = jnp.ones((win_tok, LANES), BF16)
    mid = NA_MAX_H // 2
    for t in range(NA_ROWS):
        off = jnp.where(rb == 0, max(t, mid), jnp.where(rb == nrb - 1, min(t, mid), t))
        didx = t + mid - off
        kstart = pl.multiple_of(off * GRID_W, GRID_W)
        for blk in range(B_W // LANES):
            qp = q_ref[0, GRID_W * t:GRID_W * (t + 1), LANES * blk:LANES * (blk + 1)]
            qm = jnp.concatenate([jnp.where(lo, qp, zero), jnp.where(lo, zero, qp)], axis=0)
            kw = kbuf[pl.ds(kstart, win_tok), LANES * blk:LANES * (blk + 1)]
            vw = vbuf[pl.ds(kstart, win_tok), LANES * blk:LANES * (blk + 1)]
            s = lax.dot_general(qm, kw, NT_DIMS, preferred_element_type=F32)
            s = s + bias_ref[didx, blk]
            p = jnp.exp2(s - jnp.max(s, axis=-1, keepdims=True))
            oe = jnp.dot(p.astype(BF16), jnp.concatenate([vw, ones], axis=1),
                         preferred_element_type=F32)
            o = oe[:, :LANES] / oe[:, LANES:]
            o_ref[0, GRID_W * t:GRID_W * (t + 1), LANES * blk:LANES * (blk + 1)] = jnp.where(
                lo, o[:GRID_W], o[GRID_W:]).astype(BF16)


def _attn_b(qb, kb, vb, bias_tab):
    bsz, seq, _ = qb.shape
    blk_tok = NA_ROWS * GRID_W
    nrb = seq // blk_tok
    assert nrb >= 2, "neighbourhood attention needs at least two row blocks"
    blk = lambda f: pl.BlockSpec((1, blk_tok, B_W), f)
    prev = lambda b, n: (b, jnp.maximum(n - 1, 0), 0)
    cur = lambda b, n: (b, n, 0)
    nxt = lambda b, n: (b, jnp.minimum(n + 1, nrb - 1), 0)
    return pl.pallas_call(
        functools.partial(_attn_b_kernel, nrb=nrb),
        grid=(bsz, nrb),
        in_specs=[blk(cur), blk(prev), blk(cur), blk(nxt), blk(prev), blk(cur), blk(nxt),
                  _resident(bias_tab.shape)],
        out_specs=blk(cur),
        out_shape=jax.ShapeDtypeStruct((bsz, seq, B_W), BF16),
        scratch_shapes=[pltpu.VMEM((2 * blk_tok, B_W), BF16), pltpu.VMEM((2 * blk_tok, B_W), BF16)],
        compiler_params=_cparams("parallel", "parallel"),
        name="attn_b",
    )(qb, kb, kb, kb, vb, vb, vb, bias_tab)


def _out_ffn_kernel(*refs, n_mix):
    x_ref, o_refs, wo_refs = refs[0], refs[1:1 + n_mix], refs[1 + n_mix:1 + 2 * n_mix]
    g_ref, wg_ref, wu_ref, wd_ref, y_ref = refs[1 + 2 * n_mix:]
    x1 = x_ref[0]
    for o_ref, wo_ref in zip(o_refs, wo_refs):
        x1 = x1 + jnp.dot(o_ref[0], wo_ref[...], preferred_element_type=F32)
    h = _rms_rows(x1, g_ref[...]).astype(BF16)
    gate = jnp.dot(h, wg_ref[...], preferred_element_type=F32)
    up = jnp.dot(h, wu_ref[...], preferred_element_type=F32)
    act = (gate / (1.0 + jnp.exp(-gate)) * up).astype(BF16)
    y_ref[0] = x1 + jnp.dot(act, wd_ref[...], preferred_element_type=F32)


def _out_ffn(x, mix_outs, mix_weights, g, wg, wu, wd):
    bsz, seq, _ = x.shape
    tm = TM_FFN
    tok = lambda width: pl.BlockSpec((1, tm, width), lambda b, i: (b, i, 0))
    consts = (*mix_weights, g, wg, wu, wd)
    return pl.pallas_call(
        functools.partial(_out_ffn_kernel, n_mix=len(mix_outs)),
        grid=(bsz, seq // tm),
        in_specs=[tok(D_MODEL), *[tok(o.shape[-1]) for o in mix_outs],
                  *[_resident(c.shape) for c in consts]],
        out_specs=tok(D_MODEL),
        out_shape=jax.ShapeDtypeStruct(x.shape, F32),
        compiler_params=_cparams("parallel", "parallel"),
        name="out_ffn",
    )(x, *mix_outs, *consts)


def _proj_mla_kernel(shift_ref, x_ref, g_ref, win_ref, gq_ref, gkv_ref, wqt_ref, wvt_ref, wk_ref,
                     gqn_ref, gkn_ref, cosk_ref, sink_ref, cosq_ref, sinq_ref,
                     q_ref, k_ref, v_ref):
    tm = x_ref.shape[1]
    shift = shift_ref[0]
    h = _rms_rows(x_ref[0], g_ref[...]).astype(BF16)
    y = jnp.dot(h, win_ref[...], preferred_element_type=F32)
    cq = _rms_rows(y[:, :Q_LORA], gq_ref[...]).astype(BF16)
    ckv = _rms_rows(y[:, Q_LORA:Q_LORA + KV_LORA], gkv_ref[...]).astype(BF16)
    kpe = y[:, Q_LORA + KV_LORA:]

    qt = lax.dot_general(wqt_ref[...], cq, NT_DIMS, preferred_element_type=F32)
    qt = qt.reshape(C_HEADS, LANES, tm)
    ssq = jnp.sum(qt * qt, axis=1, keepdims=True)
    qn = qt * lax.rsqrt(ssq * (1.0 / C_QK) + EPS) * gqn_ref[...][None]
    half = C_ROPE // 2
    r1 = qn[:, C_NOPE:C_NOPE + half]
    r2 = qn[:, C_NOPE + half:C_QK]
    cos = cosq_ref[...][None]
    sin = sinq_ref[...][None]
    pad_row = lax.broadcasted_iota(jnp.int32, (C_HEADS, LANES - C_QK, tm), 1)
    q_out = jnp.concatenate([qn[:, :C_NOPE], r1 * cos - r2 * sin, r2 * cos + r1 * sin,
                             jnp.where(pad_row == 0, -shift, 0.0)], axis=1)
    q_ref[0] = q_out.astype(BF16)

    vt = lax.dot_general(wvt_ref[...], ckv, NT_DIMS, preferred_element_type=F32)
    vt = vt.reshape(C_HEADS, C_VDIM, tm).astype(BF16)
    row = lax.broadcasted_iota(jnp.int32, (C_HEADS, V_ROWS - C_VDIM, tm), 1)
    v_ref[0, :, 0] = jnp.concatenate([vt, jnp.where(row == 0, 1.0, 0.0).astype(BF16)], axis=1)

    kn = jnp.dot(ckv, wk_ref[...], preferred_element_type=F32)
    gk = gkn_ref[...]
    ss_pe = jnp.sum(kpe * kpe, axis=-1, keepdims=True)
    kg = kpe * gk
    lane = lax.broadcasted_iota(jnp.int32, kg.shape, 1)
    swapped = jnp.where(lane < C_NOPE + half, pltpu.roll(kg, LANES - half, 1),
                        pltpu.roll(kg, half, 1))
    kr = kg * cosk_ref[...] + swapped * sink_ref[...]
    for hd in range(C_HEADS):
        kh = kn[:, LANES * hd:LANES * (hd + 1)]
        ss = jnp.sum(kh * kh, axis=-1, keepdims=True) + ss_pe
        kval = (kh * gk + kr) * lax.rsqrt(ss * (1.0 / C_QK) + EPS)
        k_ref[0, :, LANES * hd:LANES * (hd + 1)] = jnp.where(lane == C_QK, 1.0, kval).astype(BF16)


def _proj_mla(shift, x, g, win, gq, gkv, wqt, wvt, wk, gqn, gkn, cosk, sink, cosq, sinq):
    bsz, seq, _ = x.shape
    tm = TM_PROJ
    nt = seq // tm
    half = C_ROPE // 2
    return pl.pallas_call(
        _proj_mla_kernel,
        grid=(bsz, nt),
        in_specs=[pl.BlockSpec(memory_space=pltpu.SMEM),
                  pl.BlockSpec((1, tm, D_MODEL), lambda b, i: (b, i, 0)),
                  _resident(g.shape), _resident(win.shape), _resident(gq.shape),
                  _resident(gkv.shape), _resident(wqt.shape), _resident(wvt.shape),
                  _resident(wk.shape), _resident(gqn.shape), _resident(gkn.shape),
                  pl.BlockSpec((tm, LANES), lambda b, i: (i, 0)),
                  pl.BlockSpec((tm, LANES), lambda b, i: (i, 0)),
                  pl.BlockSpec((half, tm), lambda b, i: (0, i)),
                  pl.BlockSpec((half, tm), lambda b, i: (0, i))],
        out_specs=[pl.BlockSpec((1, C_HEADS, LANES, tm), lambda b, i: (b, 0, 0, i)),
                   pl.BlockSpec((1, tm, C_HEADS * LANES), lambda b, i: (b, i, 0)),
                   pl.BlockSpec((1, C_HEADS, 1, V_ROWS, tm), lambda b, i: (b, 0, i, 0, 0))],
        out_shape=[jax.ShapeDtypeStruct((bsz, C_HEADS, LANES, seq), BF16),
                   jax.ShapeDtypeStruct((bsz, seq, C_HEADS * LANES), BF16),
                   jax.ShapeDtypeStruct((bsz, C_HEADS, nt, V_ROWS, tm), BF16)],
        compiler_params=_cparams("parallel", "parallel"),
        name="proj_mla",
    )(shift, x, g, win, gq, gkv, wqt, wvt, wk, gqn, gkn, cosk, sink, cosq, sinq)


def _attn_mla_kernel(q_ref, k_ref, v_ref, o_ref, s_buf, cmax_buf, p_buf, alpha_buf, m_buf, acc_buf,
                     *, nchunks, tk):
    heads = range(2)

    def stage_a(c, slot):
        start = pl.multiple_of(c * tk, tk)
        for hh in heads:
            kc = k_ref[0, pl.ds(start, tk), LANES * hh:LANES * (hh + 1)]
            s = jnp.dot(kc, q_ref[0, hh], preferred_element_type=F32)
            s_buf[hh, slot] = s
            cmax_buf[hh, slot] = jnp.max(s, axis=0, keepdims=True)

    def stage_b(slot):
        for hh in heads:
            m_old = m_buf[hh]
            m_new = jnp.maximum(m_old, cmax_buf[hh, slot])
            alpha_buf[hh, slot] = jnp.exp2(m_old - m_new)
            m_buf[hh] = m_new
            p_buf[hh, slot] = jnp.exp2(s_buf[hh, slot] - m_new).astype(BF16)

    def stage_c(c, slot):
        for hh in heads:
            pv = jnp.dot(v_ref[0, hh, c], p_buf[hh, slot], preferred_element_type=F32)
            acc_buf[hh] = alpha_buf[hh, slot] * acc_buf[hh] + pv

    m_buf[...] = jnp.full(m_buf.shape, NEG_INF, F32)
    acc_buf[...] = jnp.zeros(acc_buf.shape, F32)
    p_buf[:, 1] = jnp.zeros(p_buf.shape[:1] + p_buf.shape[2:], BF16)
    alpha_buf[:, 1] = jnp.ones(alpha_buf.shape[:1] + alpha_buf.shape[2:], F32)
    stage_a(0, 0)

    def body(i, carry):
        c0 = 2 * i
        stage_b(0)
        stage_a(c0 + 1, 1)
        stage_c(jnp.maximum(c0 - 1, 0), 1)
        stage_b(1)
        stage_a(jnp.minimum(c0 + 2, nchunks - 1), 0)
        stage_c(c0, 0)
        return carry

    lax.fori_loop(0, nchunks // 2, body, 0)
    stage_c(nchunks - 1, 1)
    outs = [acc_buf[hh, :C_VDIM] / acc_buf[hh, C_VDIM:C_VDIM + 1] for hh in heads]
    o_ref[0] = jnp.concatenate(outs, axis=0).T.astype(BF16)


def _attn_mla_bounded_kernel(q_ref, k_ref, v_ref, o_ref, p_buf, acc_buf, *, nchunks, tk):
    heads = range(2)

    def stage_p(c, slot):
        start = pl.multiple_of(c * tk, tk)
        for hh in heads:
            kc = k_ref[0, pl.ds(start, tk), LANES * hh:LANES * (hh + 1)]
            s = jnp.dot(kc, q_ref[0, hh], preferred_element_type=F32)
            p_buf[hh, slot] = jnp.exp2(s).astype(BF16)

    def stage_c(c, slot):
        for hh in heads:
            acc_buf[hh] += jnp.dot(v_ref[0, hh, c], p_buf[hh, slot], preferred_element_type=F32)

    acc_buf[...] = jnp.zeros(acc_buf.shape, F32)
    stage_p(0, 0)

    def body(i, carry):
        c0 = 2 * i
        stage_p(c0 + 1, 1)
        stage_c(c0, 0)
        stage_p(jnp.minimum(c0 + 2, nchunks - 1), 0)
        stage_c(c0 + 1, 1)
        return carry

    lax.fori_loop(0, nchunks // 2, body, 0)
    outs = [acc_buf[hh, :C_VDIM] / acc_buf[hh, C_VDIM:C_VDIM + 1] for hh in heads]
    o_ref[0] = jnp.concatenate(outs, axis=0).T.astype(BF16)


def _attn_mla(qt, k, vt, bounded):
    bsz, _, _, seq = qt.shape
    nchunks, tk = vt.shape[2], vt.shape[4]
    assert nchunks % 2 == 0, "the pipelined loops handle two key chunks per trip"
    tq = TQ_MLA_BOUNDED if bounded else TQ_MLA
    if bounded:
        body = _attn_mla_bounded_kernel
        scratch = [pltpu.VMEM((2, 2, tk, tq), BF16),
                   pltpu.VMEM((2, V_ROWS, tq), F32)]
    else:
        body = _attn_mla_kernel
        scratch = [pltpu.VMEM((2, 2, tk, tq), F32),
                   pltpu.VMEM((2, 2, 1, tq), F32),
                   pltpu.VMEM((2, 2, tk, tq), BF16),
                   pltpu.VMEM((2, 2, 1, tq), F32),
                   pltpu.VMEM((2, 1, tq), F32),
                   pltpu.VMEM((2, V_ROWS, tq), F32)]
    return pl.pallas_call(
        functools.partial(body, nchunks=nchunks, tk=tk),
        grid=(bsz, C_HEADS // 2, seq // tq),
        in_specs=[pl.BlockSpec((1, 2, LANES, tq), lambda b, h, i: (b, h, 0, i)),
                  pl.BlockSpec((1, seq, 2 * LANES), lambda b, h, i: (b, 0, h)),
                  pl.BlockSpec((1, 2, nchunks, V_ROWS, tk), lambda b, h, i: (b, h, 0, 0, 0))],
        out_specs=pl.BlockSpec((1, tq, 2 * C_VDIM), lambda b, h, i: (b, i, h)),
        out_shape=jax.ShapeDtypeStruct((bsz, seq, C_HEADS * C_VDIM), BF16),
        scratch_shapes=scratch,
        compiler_params=_cparams("parallel", "parallel", "arbitrary"),
        name="attn_mla_bounded" if bounded else "attn_mla",
    )(qt, k, vt)


def _rope_angles(seq, half):
    inv_freq = ROPE_THETA ** (-jnp.arange(half, dtype=F32) / half)
    return jnp.arange(seq, dtype=F32)[:, None] * inv_freq[None, :]


def _prep_ab(ab_w_in, ab_w_out, a_q_norm, a_k_norm, a_sink, b_q_norm, b_k_norm, b_rpb, seq):
    qa_cols = np.concatenate([np.arange(HEAD_DIM * h, HEAD_DIM * (h + 1)) for h in PERM_A])
    o_qb, o_kb, o_vb = A_QW + 2 * A_KVW, A_QW + 2 * A_KVW + B_W, A_QW + 2 * A_KVW + 2 * B_W
    cols = np.concatenate([qa_cols, o_qb + np.arange(B_W), o_kb + np.arange(B_W),
                           A_QW + np.arange(A_KVW), A_QW + A_KVW + np.arange(A_KVW),
                           o_vb + np.arange(B_W)])
    w_in = ab_w_in[:, cols].astype(BF16)
    woa = ab_w_out[qa_cols].astype(BF16)
    wob = ab_w_out[A_QW:].astype(BF16)
    scale = HEAD_DIM ** -0.5 * LOG2E
    gain = jnp.concatenate([jnp.tile(a_q_norm * scale, A_HEADS), jnp.tile(b_q_norm * scale, B_HEADS),
                            jnp.tile(b_k_norm, B_HEADS), jnp.tile(a_k_norm, A_KV_HEADS),
                            jnp.ones((A_KVW,), F32)])[None, :].astype(F32)
    idx = np.arange(2 * LANES) // HEAD_DIM
    gsum = jnp.asarray(idx[:, None] == idx[None, :], BF16)
    ang = _rope_angles(seq, HEAD_DIM // 2)
    cos = jnp.tile(jnp.cos(ang), (1, LANES // (HEAD_DIM // 2)))
    sin = jnp.tile(jnp.concatenate([-jnp.sin(ang), jnp.sin(ang)], axis=1), (1, LANES // HEAD_DIM))
    sink_col = jnp.repeat(a_sink[np.array(PERM_A)] * LOG2E, BLOCK).reshape(
        A_QW // LANES, 2 * BLOCK, 1).astype(F32)
    i = np.arange(2 * BLOCK)[:, None] % BLOCK
    j = np.arange(3 * BLOCK)[None, :]
    band = (j - i >= BLOCK - WINDOW) & (j - i <= BLOCK + WINDOW)
    band = np.stack([band & (j >= BLOCK), band, band & (j < 2 * BLOCK)])
    band_tab = jnp.asarray(np.where(band, 0.0, NEG_INF), F32)

    c = np.arange(GRID_W)
    cs = np.clip(c - NA_W // 2, 0, GRID_W - NA_W)
    inwin = (c[None, :] >= cs[:, None]) & (c[None, :] < cs[:, None] + NA_W)
    dc = np.clip(c[None, :] - c[:, None] + NA_W - 1, 0, 2 * NA_W - 2)
    dr = np.arange(NA_MAX_H)[None, :] + (NA_MAX_H - 1) - np.arange(NA_MAX_H)[:, None]
    tab = b_rpb.astype(F32)[:, dr][:, :, :, dc]
    tab = jnp.where(inwin[None, None, None], tab * LOG2E, NEG_INF)
    tab = tab.transpose(1, 0, 3, 2, 4).reshape(NA_MAX_H, B_HEADS // 2, 2 * GRID_W, NA_MAX_H * GRID_W)
    return w_in, woa, wob, gain, gsum, cos, sin, band_tab, sink_col, tab


def _prep_mla(c_w_in, c_q_lora_norm, c_kv_lora_norm, c_w_q_up, c_w_kv_up, c_q_norm, c_k_norm,
              c_w_out, seq):
    win = jnp.zeros((D_MODEL, Q_LORA + KV_LORA + LANES), F32)
    win = win.at[:, :Q_LORA + KV_LORA].set(c_w_in[:, :Q_LORA + KV_LORA])
    win = win.at[:, Q_LORA + KV_LORA + C_NOPE:Q_LORA + KV_LORA + C_QK].set(c_w_in[:, Q_LORA + KV_LORA:])
    wq = c_w_q_up.reshape(Q_LORA, C_HEADS, C_QK)
    wq = jnp.pad(wq, ((0, 0), (0, 0), (0, LANES - C_QK)))
    wqt = wq.reshape(Q_LORA, C_HEADS * LANES).T.astype(BF16)
    wkv = c_w_kv_up.reshape(KV_LORA, C_HEADS, C_NOPE + C_VDIM)
    wvt = wkv[:, :, C_NOPE:].reshape(KV_LORA, C_HEADS * C_VDIM).T.astype(BF16)
    wk = jnp.pad(wkv[:, :, :C_NOPE], ((0, 0), (0, 0), (0, LANES - C_NOPE)))
    wk = wk.reshape(KV_LORA, C_HEADS * LANES).astype(BF16)
    qscale = C_QK ** -0.5 * LOG2E
    shift = (BF16_NORM_MARGIN * C_QK * jnp.max(jnp.abs(c_q_norm * qscale)) * jnp.max(jnp.abs(c_k_norm)))
    shift = shift.reshape(1).astype(F32)
    gqn = jnp.pad(c_q_norm * qscale, (0, LANES - C_QK))[:, None].astype(F32)
    gkn = jnp.pad(c_k_norm, (0, LANES - C_QK))[None, :].astype(F32)
    half = C_ROPE // 2
    ang = _rope_angles(seq, half)
    cos, sin = jnp.cos(ang), jnp.sin(ang)
    zl = jnp.zeros((seq, C_NOPE), F32)
    zr = jnp.zeros((seq, LANES - C_QK), F32)
    cosk = jnp.concatenate([zl, cos, cos, zr], axis=1)
    sink = jnp.concatenate([zl, -sin, sin, zr], axis=1)
    return (shift, win.astype(BF16), c_q_lora_norm[None, :].astype(F32),
            c_kv_lora_norm[None, :].astype(F32), wqt, wvt, wk, gqn, gkn, cosk, sink, cos.T, sin.T,
            c_w_out.astype(BF16))


def _trunk(x, p):
    seq = x.shape[1]
    w_in, woa, wob, gain, gsum, cos, sin, band_tab, sink_col, bias_tab = _prep_ab(
        p["ab_w_in"][0], p["ab_w_out"][0], p["a_q_norm"][0], p["a_k_norm"][0], p["a_sink"][0],
        p["b_q_norm"][0], p["b_k_norm"][0], p["b_rpb"][0], seq)
    qa, ka, va, qb, kb, vb = _proj_ab(x, p["norm_mix"][0][None, :], w_in, gsum, gain, cos, sin)
    oa = _attn_a(qa, ka, va, band_tab, sink_col)
    ob = _attn_b(qb, kb, vb, bias_tab)
    x = _out_ffn(x, (oa, ob), (woa, wob), p["norm_ffn"][0][None, :],
                 p["ffn_w_gate"][0].astype(BF16), p["ffn_w_up"][0].astype(BF16),
                 p["ffn_w_down"][0].astype(BF16))
    (shift, win, gq, gkv, wqt, wvt, wk, gqn, gkn, cosk, sink, cosq, sinq, wo) = _prep_mla(
        p["c_w_in"][0], p["c_q_lora_norm"][0], p["c_kv_lora_norm"][0], p["c_w_q_up"][0],
        p["c_w_kv_up"][0], p["c_q_norm"][0], p["c_k_norm"][0], p["c_w_out"][0], seq)
    qt, k, vt = _proj_mla(shift, x, p["norm_mix"][1][None, :], win, gq, gkv, wqt, wvt, wk, gqn, gkn,
                          cosk, sink, cosq, sinq)
    o = lax.cond(shift[0] <= SHIFT_MAX,
                 functools.partial(_attn_mla, bounded=True),
                 functools.partial(_attn_mla, bounded=False), qt, k, vt)
    x = _out_ffn(x, (o,), (wo,), p["norm_ffn"][1][None, :], p["ffn_w_gate"][1].astype(BF16),
                 p["ffn_w_up"][1].astype(BF16), p["ffn_w_down"][1].astype(BF16))
    return x


def kernel(x_prompt, x_sample, norm_mix, norm_ffn, ab_w_in, ab_w_out, a_q_norm, a_k_norm, a_sink,
           b_q_norm, b_k_norm, b_rpb, c_w_in, c_q_lora_norm, c_kv_lora_norm, c_w_q_up, c_w_kv_up,
           c_q_norm, c_k_norm, c_w_out, ffn_w_gate, ffn_w_up, ffn_w_down):
    p = dict(norm_mix=norm_mix, norm_ffn=norm_ffn, ab_w_in=ab_w_in, ab_w_out=ab_w_out,
             a_q_norm=a_q_norm, a_k_norm=a_k_norm, a_sink=a_sink, b_q_norm=b_q_norm,
             b_k_norm=b_k_norm, b_rpb=b_rpb, c_w_in=c_w_in, c_q_lora_norm=c_q_lora_norm,
             c_kv_lora_norm=c_kv_lora_norm, c_w_q_up=c_w_q_up, c_w_kv_up=c_w_kv_up,
             c_q_norm=c_q_norm, c_k_norm=c_k_norm, c_w_out=c_w_out, ffn_w_gate=ffn_w_gate,
             ffn_w_up=ffn_w_up, ffn_w_down=ffn_w_down)
    return _trunk(x_prompt, p), _trunk(x_sample, p)
```

```python
import functools
import math

import numpy as np
import jax
import jax.numpy as jnp
from jax import lax
from jax.experimental import pallas as pl
from jax.experimental.pallas import tpu as pltpu

F32 = jnp.float32
BF16 = jnp.bfloat16

D_MODEL = 1024
GRID_W = 64
HEAD_DIM = 64
ROPE_THETA = 10000.0
EPS = 1e-6
NEG_INF = -1e30
BLOCK = 128
A_HEADS = 8
A_KV_HEADS = 2
WINDOW = 128
B_HEADS = 8
NA_MAX_H = 8
NA_W = 16
C_HEADS = 16
C_NOPE = 64
C_ROPE = 32
C_VDIM = 64
C_QK = C_NOPE + C_ROPE
Q_LORA = 384
KV_LORA = 256
A_QW = A_HEADS * HEAD_DIM
A_KVW = A_KV_HEADS * HEAD_DIM
B_W = B_HEADS * HEAD_DIM

LANES = 128
V7X_VMEM_BYTES = 64 * 1024 * 1024
VMEM_LIMIT = V7X_VMEM_BYTES * 3 // 4

TM_PROJ = 512
TM_FFN = 512
TQ_MLA = 1024
TQ_MLA_BOUNDED = 2048
MLA_QSPLIT = 4
MLA_CHUNKS_PER_TRIP = 4
NA_ROWS = 8
A_QBLOCKS = 4
V_ROWS = C_VDIM + 16

PERM_A = (0, 4, 1, 5, 2, 6, 3, 7)

LOG2E = math.log2(math.e)
SHIFT_MAX = 60.0
BF16_NORM_MARGIN = 1.0 + 2.0 ** -6
NT_DIMS = (((1,), (1,)), ((), ()))


def _cparams(*sem):
    return pltpu.CompilerParams(dimension_semantics=sem, vmem_limit_bytes=VMEM_LIMIT)


def _resident(shape):
    nd = len(shape)
    return pl.BlockSpec(shape, lambda *_: (0,) * nd, pipeline_mode=pl.Buffered(1))


def _rms_rows(x, gain):
    ms = jnp.mean(x * x, axis=-1, keepdims=True)
    return x * lax.rsqrt(ms + EPS) * gain


def _proj_ab_kernel(x_ref, g_ref, w_ref, gsum_ref, gain_ref, cos_ref, sin_ref,
                    qa_ref, ka_ref, va_ref, qb_ref, kb_ref, vb_ref):
    h = _rms_rows(x_ref[0], g_ref[...]).astype(BF16)
    y = jnp.dot(h, w_ref[...], preferred_element_type=F32)
    gsum = gsum_ref[...]
    cos = cos_ref[...]
    sin = sin_ref[...]
    lane = lax.broadcasted_iota(jnp.int32, cos.shape, 1)
    first_half = (lane % HEAD_DIM) < (HEAD_DIM // 2)

    def head_norm(c):
        yc = y[:, 2 * LANES * c:2 * LANES * (c + 1)]
        ss = jnp.dot((yc * yc).astype(BF16), gsum, preferred_element_type=F32)
        return yc * lax.rsqrt(ss * (1.0 / HEAD_DIM) + EPS) * gain_ref[:, 2 * LANES * c:2 * LANES * (c + 1)]

    def rope(v):
        swapped = jnp.where(first_half, pltpu.roll(v, LANES - HEAD_DIM // 2, 1),
                            pltpu.roll(v, HEAD_DIM // 2, 1))
        return v * cos + swapped * sin

    for c in range(2):
        yn = head_norm(c)
        for b in range(2):
            qa_ref[0, :, LANES * (2 * c + b):LANES * (2 * c + b + 1)] = rope(
                yn[:, LANES * b:LANES * (b + 1)]).astype(BF16)
    for c in range(2):
        qb_ref[0, :, 2 * LANES * c:2 * LANES * (c + 1)] = head_norm(2 + c).astype(BF16)
    for c in range(2):
        kb_ref[0, :, 2 * LANES * c:2 * LANES * (c + 1)] = head_norm(4 + c).astype(BF16)
    ka_ref[0] = rope(head_norm(6)[:, :LANES]).astype(BF16)
    va_ref[0] = y[:, 13 * LANES:14 * LANES].astype(BF16)
    vb_ref[0] = y[:, 14 * LANES:18 * LANES].astype(BF16)


def _proj_ab(x, g, w, gsum, gain, cos, sin):
    bsz, seq, _ = x.shape
    tm = TM_PROJ
    nt = seq // tm
    tok = lambda width: pl.BlockSpec((1, tm, width), lambda b, i: (b, i, 0))
    tab = pl.BlockSpec((tm, LANES), lambda b, i: (i, 0))
    out = lambda width: jax.ShapeDtypeStruct((bsz, seq, width), BF16)
    return pl.pallas_call(
        _proj_ab_kernel,
        grid=(bsz, nt),
        in_specs=[tok(D_MODEL), _resident(g.shape), _resident(w.shape), _resident(gsum.shape),
                  _resident(gain.shape), tab, tab],
        out_specs=[tok(A_QW), tok(A_KVW), tok(A_KVW), tok(B_W), tok(B_W), tok(B_W)],
        out_shape=[out(A_QW), out(A_KVW), out(A_KVW), out(B_W), out(B_W), out(B_W)],
        compiler_params=_cparams("parallel", "parallel"),
        name="proj_ab",
    )(x, g, w, gsum, gain, cos, sin)


def _attn_a_body(q_ref, kp_ref, kc_ref, kn_ref, vp_ref, vc_ref, vn_ref, bias_ref, sink_ref, o_ref,
                 *, nsteps, lane0):
    n = pl.program_id(1)
    k = jnp.concatenate([kp_ref[0], kc_ref[0], kn_ref[0]], axis=0)
    v = jnp.concatenate([vp_ref[0], vc_ref[0], vn_ref[0]], axis=0)
    vext = jnp.concatenate([v, jnp.ones(v.shape, BF16)], axis=1)
    lane = lax.broadcasted_iota(jnp.int32, (BLOCK, LANES), 1)
    lo = lane < HEAD_DIM
    zero = jnp.zeros((BLOCK, LANES), BF16)
    for t in range(A_QBLOCKS):
        if t == 0:
            bias = bias_ref[jnp.where(n == 0, 0, 1)]
        elif t == A_QBLOCKS - 1:
            bias = bias_ref[jnp.where(n == nsteps - 1, 2, 1)]
        else:
            bias = bias_ref[1]
        kw = k[BLOCK * t:BLOCK * (t + 3)]
        vw = vext[BLOCK * t:BLOCK * (t + 3)]
        for blk in range(A_QW // LANES):
            qp = q_ref[0, BLOCK * t:BLOCK * (t + 1), LANES * blk:LANES * (blk + 1)]
            qm = jnp.concatenate([jnp.where(lo, qp, zero), jnp.where(lo, zero, qp)], axis=0)
            s = lax.dot_general(qm, kw, NT_DIMS, preferred_element_type=F32) + bias
            sink = sink_ref[blk]
            m = jnp.maximum(jnp.max(s, axis=-1, keepdims=True), sink)
            p = jnp.exp2(s - m)
            oe = jnp.dot(p.astype(BF16), vw, preferred_element_type=F32)
            o = oe[:, :LANES] / (oe[:, LANES:] + jnp.exp2(sink - m))
            o_ref[0, BLOCK * t:BLOCK * (t + 1), lane0 + LANES * blk:lane0 + LANES * (blk + 1)] = (
                jnp.where(lo, o[:BLOCK], o[BLOCK:]).astype(BF16))


def _attn_b_body(q_ref, kp_ref, kc_ref, kn_ref, vp_ref, vc_ref, vn_ref, bias_ref, o_ref,
                 kbuf, vbuf, *, nrb, lane0):
    rb = pl.program_id(1)
    blk_tok = NA_ROWS * GRID_W
    half_tok = blk_tok // 2
    win_tok = NA_MAX_H * GRID_W
    kbuf[0:half_tok] = kp_ref[0, half_tok:blk_tok]
    kbuf[half_tok:half_tok + blk_tok] = kc_ref[0]
    kbuf[half_tok + blk_tok:2 * blk_tok] = kn_ref[0, 0:half_tok]
    vbuf[0:half_tok] = vp_ref[0, half_tok:blk_tok]
    vbuf[half_tok:half_tok + blk_tok] = vc_ref[0]
    vbuf[half_tok + blk_tok:2 * blk_tok] = vn_ref[0, 0:half_tok]
    lane = lax.broadcasted_iota(jnp.int32, (GRID_W, LANES), 1)
    lo = lane < HEAD_DIM
    zero = jnp.zeros((GRID_W, LANES), BF16)
    ones = jnp.ones((win_tok, LANES), BF16)
    mid = NA_MAX_H // 2
    for t in range(NA_ROWS):
        off = jnp.where(rb == 0, max(t, mid), jnp.where(rb == nrb - 1, min(t, mid), t))
        didx = t + mid - off
        kstart = pl.multiple_of(off * GRID_W, GRID_W)
        for blk in range(B_W // LANES):
            qp = q_ref[0, GRID_W * t:GRID_W * (t + 1), LANES * blk:LANES * (blk + 1)]
            qm = jnp.concatenate([jnp.where(lo, qp, zero), jnp.where(lo, zero, qp)], axis=0)
            kw = kbuf[pl.ds(kstart, win_tok), LANES * blk:LANES * (blk + 1)]
            vw = vbuf[pl.ds(kstart, win_tok), LANES * blk:LANES * (blk + 1)]
            s = lax.dot_general(qm, kw, NT_DIMS, preferred_element_type=F32)
            s = s + bias_ref[didx, blk]
            p = jnp.exp2(s - jnp.max(s, axis=-1, keepdims=True))
            oe = jnp.dot(p.astype(BF16), jnp.concatenate([vw, ones], axis=1),
                         preferred_element_type=F32)
            o = oe[:, :LANES] / oe[:, LANES:]
            o_ref[0, GRID_W * t:GRID_W * (t + 1), lane0 + LANES * blk:lane0 + LANES * (blk + 1)] = (
                jnp.where(lo, o[:GRID_W], o[GRID_W:]).astype(BF16))


def _attn_ab_kernel(*refs, nsteps):
    a_refs, b_refs, o_ref, scratch = refs[:9], refs[9:17], refs[17], refs[18:]
    _attn_a_body(*a_refs, o_ref, nsteps=nsteps, lane0=0)
    _attn_b_body(*b_refs, o_ref, *scratch, nrb=nsteps, lane0=A_QW)


def _attn_ab(qa, ka, va, band_tab, sink_col, qb, kb, vb, bias_tab):
    bsz, seq, _ = qa.shape
    tok = A_QBLOCKS * BLOCK
    assert tok == NA_ROWS * GRID_W
    nsteps = seq // tok
    assert nsteps >= 2, "neighbourhood attention needs at least two row blocks"
    nb = seq // BLOCK
    cur = lambda b, n: (b, n, 0)
    a_edge = lambda f: pl.BlockSpec((1, BLOCK, A_KVW), f)
    a_prev = lambda b, n: (b, jnp.maximum(A_QBLOCKS * n - 1, 0), 0)
    a_next = lambda b, n: (b, jnp.minimum(A_QBLOCKS * (n + 1), nb - 1), 0)
    a_mid = pl.BlockSpec((1, tok, A_KVW), cur)
    b_blk = lambda f: pl.BlockSpec((1, tok, B_W), f)
    b_prev = lambda b, n: (b, jnp.maximum(n - 1, 0), 0)
    b_next = lambda b, n: (b, jnp.minimum(n + 1, nsteps - 1), 0)
    return pl.pallas_call(
        functools.partial(_attn_ab_kernel, nsteps=nsteps),
        grid=(bsz, nsteps),
        in_specs=[pl.BlockSpec((1, tok, A_QW), cur), a_edge(a_prev), a_mid, a_edge(a_next),
                  a_edge(a_prev), a_mid, a_edge(a_next), _resident(band_tab.shape),
                  _resident(sink_col.shape),
                  b_blk(cur), b_blk(b_prev), b_blk(cur), b_blk(b_next),
                  b_blk(b_prev), b_blk(cur), b_blk(b_next), _resident(bias_tab.shape)],
        out_specs=pl.BlockSpec((1, tok, A_QW + B_W), cur),
        out_shape=jax.ShapeDtypeStruct((bsz, seq, A_QW + B_W), BF16),
        scratch_shapes=[pltpu.VMEM((2 * tok, B_W), BF16), pltpu.VMEM((2 * tok, B_W), BF16)],
        compiler_params=_cparams("parallel", "parallel"),
        name="attn_ab",
    )(qa, ka, ka, ka, va, va, va, band_tab, sink_col, qb, kb, kb, kb, vb, vb, vb, bias_tab)


def _out_ffn_kernel(x_ref, o_ref, wo_ref, g_ref, wg_ref, wu_ref, wd_ref, y_ref):
    x1 = x_ref[0] + jnp.dot(o_ref[0], wo_ref[...], preferred_element_type=F32)
    h = _rms_rows(x1, g_ref[...]).astype(BF16)
    gate = jnp.dot(h, wg_ref[...], preferred_element_type=F32)
    up = jnp.dot(h, wu_ref[...], preferred_element_type=F32)
    act = (gate / (1.0 + jnp.exp(-gate)) * up).astype(BF16)
    y_ref[0] = x1 + jnp.dot(act, wd_ref[...], preferred_element_type=F32)


def _out_ffn(x, o, wo, g, wg, wu, wd):
    bsz, seq, _ = x.shape
    tm = TM_FFN
    tok = lambda width: pl.BlockSpec((1, tm, width), lambda b, i: (b, i, 0))
    consts = (wo, g, wg, wu, wd)
    return pl.pallas_call(
        _out_ffn_kernel,
        grid=(bsz, seq // tm),
        in_specs=[tok(D_MODEL), tok(o.shape[-1]), *[_resident(c.shape) for c in consts]],
        out_specs=tok(D_MODEL),
        out_shape=jax.ShapeDtypeStruct(x.shape, F32),
        compiler_params=_cparams("parallel", "parallel"),
        name="out_ffn",
    )(x, o, *consts)


def _proj_mla_kernel(shift_ref, x_ref, g_ref, win_ref, gq_ref, gkv_ref, wqt_ref, wvt_ref, wk_ref,
                     gqn_ref, gkn_ref, cosk_ref, sink_ref, cosq_ref, sinq_ref,
                     q_ref, k_ref, v_ref):
    tm = x_ref.shape[1]
    shift = shift_ref[0]
    h = _rms_rows(x_ref[0], g_ref[...]).astype(BF16)
    y = jnp.dot(h, win_ref[...], preferred_element_type=F32)
    cq = _rms_rows(y[:, :Q_LORA], gq_ref[...]).astype(BF16)
    ckv = _rms_rows(y[:, Q_LORA:Q_LORA + KV_LORA], gkv_ref[...]).astype(BF16)
    kpe = y[:, Q_LORA + KV_LORA:]

    qt = lax.dot_general(wqt_ref[...], cq, NT_DIMS, preferred_element_type=F32)
    qt = qt.reshape(C_HEADS, LANES, tm)
    ssq = jnp.sum(qt * qt, axis=1, keepdims=True)
    qn = qt * lax.rsqrt(ssq * (1.0 / C_QK) + EPS) * gqn_ref[...][None]
    half = C_ROPE // 2
    r1 = qn[:, C_NOPE:C_NOPE + half]
    r2 = qn[:, C_NOPE + half:C_QK]
    cos = cosq_ref[...][None]
    sin = sinq_ref[...][None]
    pad_row = lax.broadcasted_iota(jnp.int32, (C_HEADS, LANES - C_QK, tm), 1)
    q_out = jnp.concatenate([qn[:, :C_NOPE], r1 * cos - r2 * sin, r2 * cos + r1 * sin,
                             jnp.where(pad_row == 0, -shift, 0.0)], axis=1)
    q_ref[0] = q_out.astype(BF16)

    vt = lax.dot_general(wvt_ref[...], ckv, NT_DIMS, preferred_element_type=F32)
    vt = vt.reshape(C_HEADS, C_VDIM, tm).astype(BF16)
    row = lax.broadcasted_iota(jnp.int32, (C_HEADS, V_ROWS - C_VDIM, tm), 1)
    v_ref[0, :, 0] = jnp.concatenate([vt, jnp.where(row == 0, 1.0, 0.0).astype(BF16)], axis=1)

    kn = jnp.dot(ckv, wk_ref[...], preferred_element_type=F32)
    gk = gkn_ref[...]
    ss_pe = jnp.sum(kpe * kpe, axis=-1, keepdims=True)
    kg = kpe * gk
    lane = lax.broadcasted_iota(jnp.int32, kg.shape, 1)
    swapped = jnp.where(lane < C_NOPE + half, pltpu.roll(kg, LANES - half, 1),
                        pltpu.roll(kg, half, 1))
    kr = kg * cosk_ref[...] + swapped * sink_ref[...]
    for hd in range(C_HEADS):
        kh = kn[:, LANES * hd:LANES * (hd + 1)]
        ss = jnp.sum(kh * kh, axis=-1, keepdims=True) + ss_pe
        kval = (kh * gk + kr) * lax.rsqrt(ss * (1.0 / C_QK) + EPS)
        k_ref[0, :, LANES * hd:LANES * (hd + 1)] = jnp.where(lane == C_QK, 1.0, kval).astype(BF16)


def _proj_mla(shift, x, g, win, gq, gkv, wqt, wvt, wk, gqn, gkn, cosk, sink, cosq, sinq):
    bsz, seq, _ = x.shape
    tm = TM_PROJ
    nt = seq // tm
    half = C_ROPE // 2
    return pl.pallas_call(
        _proj_mla_kernel,
        grid=(bsz, nt),
        in_specs=[pl.BlockSpec(memory_space=pltpu.SMEM),
                  pl.BlockSpec((1, tm, D_MODEL), lambda b, i: (b, i, 0)),
                  _resident(g.shape), _resident(win.shape), _resident(gq.shape),
                  _resident(gkv.shape), _resident(wqt.shape), _resident(wvt.shape),
                  _resident(wk.shape), _resident(gqn.shape), _resident(gkn.shape),
                  pl.BlockSpec((tm, LANES), lambda b, i: (i, 0)),
                  pl.BlockSpec((tm, LANES), lambda b, i: (i, 0)),
                  pl.BlockSpec((half, tm), lambda b, i: (0, i)),
                  pl.BlockSpec((half, tm), lambda b, i: (0, i))],
        out_specs=[pl.BlockSpec((1, C_HEADS, LANES, tm), lambda b, i: (b, 0, 0, i)),
                   pl.BlockSpec((1, tm, C_HEADS * LANES), lambda b, i: (b, i, 0)),
                   pl.BlockSpec((1, C_HEADS, 1, V_ROWS, tm), lambda b, i: (b, 0, i, 0, 0))],
        out_shape=[jax.ShapeDtypeStruct((bsz, C_HEADS, LANES, seq), BF16),
                   jax.ShapeDtypeStruct((bsz, seq, C_HEADS * LANES), BF16),
                   jax.ShapeDtypeStruct((bsz, C_HEADS, nt, V_ROWS, tm), BF16)],
        compiler_params=_cparams("parallel", "parallel"),
        name="proj_mla",
    )(shift, x, g, win, gq, gkv, wqt, wvt, wk, gqn, gkn, cosk, sink, cosq, sinq)


def _attn_mla_kernel(q_ref, k_ref, v_ref, o_ref, s_buf, cmax_buf, p_buf, alpha_buf, m_buf, acc_buf,
                     *, nchunks, tk):
    heads = range(2)

    def stage_a(c, slot):
        start = pl.multiple_of(c * tk, tk)
        for hh in heads:
            kc = k_ref[0, pl.ds(start, tk), LANES * hh:LANES * (hh + 1)]
            s = jnp.dot(kc, q_ref[0, hh], preferred_element_type=F32)
            s_buf[hh, slot] = s
            cmax_buf[hh, slot] = jnp.max(s, axis=0, keepdims=True)

    def stage_b(slot):
        for hh in heads:
            m_old = m_buf[hh]
            m_new = jnp.maximum(m_old, cmax_buf[hh, slot])
            alpha_buf[hh, slot] = jnp.exp2(m_old - m_new)
            m_buf[hh] = m_new
            p_buf[hh, slot] = jnp.exp2(s_buf[hh, slot] - m_new).astype(BF16)

    def stage_c(c, slot):
        for hh in heads:
            pv = jnp.dot(v_ref[0, hh, c], p_buf[hh, slot], preferred_element_type=F32)
            acc_buf[hh] = alpha_buf[hh, slot] * acc_buf[hh] + pv

    m_buf[...] = jnp.full(m_buf.shape, NEG_INF, F32)
    acc_buf[...] = jnp.zeros(acc_buf.shape, F32)
    p_buf[:, 1] = jnp.zeros(p_buf.shape[:1] + p_buf.shape[2:], BF16)
    alpha_buf[:, 1] = jnp.ones(alpha_buf.shape[:1] + alpha_buf.shape[2:], F32)
    stage_a(0, 0)

    def body(i, carry):
        c0 = 2 * i
        stage_b(0)
        stage_a(c0 + 1, 1)
        stage_c(jnp.maximum(c0 - 1, 0), 1)
        stage_b(1)
        stage_a(jnp.minimum(c0 + 2, nchunks - 1), 0)
        stage_c(c0, 0)
        return carry

    lax.fori_loop(0, nchunks // 2, body, 0)
    stage_c(nchunks - 1, 1)
    outs = [acc_buf[hh, :C_VDIM] / acc_buf[hh, C_VDIM:C_VDIM + 1] for hh in heads]
    o_ref[0] = jnp.concatenate(outs, axis=0).T.astype(BF16)


def _attn_mla_bounded_kernel(q_ref, k_ref, v_ref, o_ref, p_buf, acc_buf, *, nchunks, tk):
    heads = range(2)
    width = q_ref.shape[-1] // MLA_QSPLIT
    pieces = [(hh, slice(width * sp, width * (sp + 1))) for hh in heads for sp in range(MLA_QSPLIT)]

    def piece_p(c, slot, hh, cols):
        start = pl.multiple_of(c * tk, tk)
        kc = k_ref[0, pl.ds(start, tk), LANES * hh:LANES * (hh + 1)]
        s = jnp.dot(kc, q_ref[0, hh, :, cols], preferred_element_type=F32)
        p_buf[hh, slot, :, cols] = jnp.exp2(s).astype(BF16)

    def piece_c(c, slot, hh, cols):
        acc_buf[hh, :, cols] += jnp.dot(v_ref[0, hh, c], p_buf[hh, slot, :, cols],
                                        preferred_element_type=F32)

    acc_buf[...] = jnp.zeros(acc_buf.shape, F32)
    for hh, cols in pieces:
        piece_p(0, 0, hh, cols)

    def body(i, carry):
        c0 = MLA_CHUNKS_PER_TRIP * i
        for j in range(MLA_CHUNKS_PER_TRIP):
            nxt = c0 + j + 1
            if j == MLA_CHUNKS_PER_TRIP - 1:
                nxt = jnp.minimum(nxt, nchunks - 1)
            for hh, cols in pieces:
                piece_p(nxt, (j + 1) % 2, hh, cols)
                piece_c(c0 + j, j % 2, hh, cols)
        return carry

    lax.fori_loop(0, nchunks // MLA_CHUNKS_PER_TRIP, body, 0)
    outs = [acc_buf[hh, :C_VDIM] / acc_buf[hh, C_VDIM:C_VDIM + 1] for hh in heads]
    o_ref[0] = jnp.concatenate(outs, axis=0).T.astype(BF16)


def _attn_mla(qt, k, vt, bounded):
    bsz, _, _, seq = qt.shape
    nchunks, tk = vt.shape[2], vt.shape[4]
    assert nchunks % MLA_CHUNKS_PER_TRIP == 0 and MLA_CHUNKS_PER_TRIP % 2 == 0
    tq = TQ_MLA_BOUNDED if bounded else TQ_MLA
    if bounded:
        body = _attn_mla_bounded_kernel
        scratch = [pltpu.VMEM((2, 2, tk, tq), BF16),
                   pltpu.VMEM((2, V_ROWS, tq), F32)]
    else:
        body = _attn_mla_kernel
        scratch = [pltpu.VMEM((2, 2, tk, tq), F32),
                   pltpu.VMEM((2, 2, 1, tq), F32),
                   pltpu.VMEM((2, 2, tk, tq), BF16),
                   pltpu.VMEM((2, 2, 1, tq), F32),
                   pltpu.VMEM((2, 1, tq), F32),
                   pltpu.VMEM((2, V_ROWS, tq), F32)]
    return pl.pallas_call(
        functools.partial(body, nchunks=nchunks, tk=tk),
        grid=(bsz, C_HEADS // 2, seq // tq),
        in_specs=[pl.BlockSpec((1, 2, LANES, tq), lambda b, h, i: (b, h, 0, i)),
                  pl.BlockSpec((1, seq, 2 * LANES), lambda b, h, i: (b, 0, h)),
                  pl.BlockSpec((1, 2, nchunks, V_ROWS, tk), lambda b, h, i: (b, h, 0, 0, 0))],
        out_specs=pl.BlockSpec((1, tq, 2 * C_VDIM), lambda b, h, i: (b, i, h)),
        out_shape=jax.ShapeDtypeStruct((bsz, seq, C_HEADS * C_VDIM), BF16),
        scratch_shapes=scratch,
        compiler_params=_cparams("parallel", "parallel", "arbitrary"),
        name="attn_mla_bounded" if bounded else "attn_mla",
    )(qt, k, vt)


def _rope_angles(seq, half):
    inv_freq = ROPE_THETA ** (-jnp.arange(half, dtype=F32) / half)
    return jnp.arange(seq, dtype=F32)[:, None] * inv_freq[None, :]


def _prep_ab(ab_w_in, ab_w_out, a_q_norm, a_k_norm, a_sink, b_q_norm, b_k_norm, b_rpb, seq):
    o_ka, o_va, o_qb = A_QW, A_QW + A_KVW, A_QW + 2 * A_KVW
    o_kb, o_vb = o_qb + B_W, o_qb + 2 * B_W
    w_in = jnp.concatenate(
        [ab_w_in[:, HEAD_DIM * h:HEAD_DIM * (h + 1)] for h in PERM_A]
        + [ab_w_in[:, o_qb:o_vb], ab_w_in[:, o_ka:o_qb], ab_w_in[:, o_vb:]], axis=1).astype(BF16)
    w_out = jnp.concatenate([ab_w_out[HEAD_DIM * h:HEAD_DIM * (h + 1)] for h in PERM_A]
                            + [ab_w_out[A_QW:]]).astype(BF16)
    scale = HEAD_DIM ** -0.5 * LOG2E
    gain = jnp.concatenate([jnp.tile(a_q_norm * scale, A_HEADS), jnp.tile(b_q_norm * scale, B_HEADS),
                            jnp.tile(b_k_norm, B_HEADS), jnp.tile(a_k_norm, A_KV_HEADS),
                            jnp.ones((A_KVW,), F32)])[None, :].astype(F32)
    idx = np.arange(2 * LANES) // HEAD_DIM
    gsum = jnp.asarray(idx[:, None] == idx[None, :], BF16)
    ang = _rope_angles(seq, HEAD_DIM // 2)
    cos = jnp.tile(jnp.cos(ang), (1, LANES // (HEAD_DIM // 2)))
    sin = jnp.tile(jnp.concatenate([-jnp.sin(ang), jnp.sin(ang)], axis=1), (1, LANES // HEAD_DIM))
    sink_col = jnp.repeat(a_sink[np.array(PERM_A)] * LOG2E, BLOCK).reshape(
        A_QW // LANES, 2 * BLOCK, 1).astype(F32)
    i = np.arange(2 * BLOCK)[:, None] % BLOCK
    j = np.arange(3 * BLOCK)[None, :]
    band = (j - i >= BLOCK - WINDOW) & (j - i <= BLOCK + WINDOW)
    band = np.stack([band & (j >= BLOCK), band, band & (j < 2 * BLOCK)])
    band_tab = jnp.asarray(np.where(band, 0.0, NEG_INF), F32)

    c = np.arange(GRID_W)
    cs = np.clip(c - NA_W // 2, 0, GRID_W - NA_W)
    inwin = (c[None, :] >= cs[:, None]) & (c[None, :] < cs[:, None] + NA_W)
    dc = c[None, :] - c[:, None] + NA_W - 1
    pick = jnp.asarray(dc[:, :, None] == np.arange(2 * NA_W - 1), F32)
    tcol = jnp.sum(b_rpb.astype(F32)[:, :, None, None, :] * pick[None, None], axis=-1)
    tab = jnp.stack([tcol[:, NA_MAX_H - 1 - d:2 * NA_MAX_H - 1 - d] for d in range(NA_MAX_H)],
                    axis=1)
    tab = jnp.where(inwin[None, None, None], tab * LOG2E, NEG_INF)
    tab = tab.transpose(1, 0, 3, 2, 4).reshape(NA_MAX_H, B_HEADS // 2, 2 * GRID_W, NA_MAX_H * GRID_W)
    return w_in, w_out, gain, gsum, cos, sin, band_tab, sink_col, tab


def _prep_mla(c_w_in, c_q_lora_norm, c_kv_lora_norm, c_w_q_up, c_w_kv_up, c_q_norm, c_k_norm,
              c_w_out, seq):
    win = jnp.zeros((D_MODEL, Q_LORA + KV_LORA + LANES), F32)
    win = win.at[:, :Q_LORA + KV_LORA].set(c_w_in[:, :Q_LORA + KV_LORA])
    win = win.at[:, Q_LORA + KV_LORA + C_NOPE:Q_LORA + KV_LORA + C_QK].set(c_w_in[:, Q_LORA + KV_LORA:])
    wq = c_w_q_up.reshape(Q_LORA, C_HEADS, C_QK)
    wq = jnp.pad(wq, ((0, 0), (0, 0), (0, LANES - C_QK)))
    wqt = wq.reshape(Q_LORA, C_HEADS * LANES).T.astype(BF16)
    wkv = c_w_kv_up.reshape(KV_LORA, C_HEADS, C_NOPE + C_VDIM)
    wvt = wkv[:, :, C_NOPE:].reshape(KV_LORA, C_HEADS * C_VDIM).T.astype(BF16)
    wk = jnp.pad(wkv[:, :, :C_NOPE], ((0, 0), (0, 0), (0, LANES - C_NOPE)))
    wk = wk.reshape(KV_LORA, C_HEADS * LANES).astype(BF16)
    qscale = C_QK ** -0.5 * LOG2E
    shift = (BF16_NORM_MARGIN * C_QK * jnp.max(jnp.abs(c_q_norm * qscale)) * jnp.max(jnp.abs(c_k_norm)))
    shift = shift.reshape(1).astype(F32)
    gqn = jnp.pad(c_q_norm * qscale, (0, LANES - C_QK))[:, None].astype(F32)
    gkn = jnp.pad(c_k_norm, (0, LANES - C_QK))[None, :].astype(F32)
    half = C_ROPE // 2
    ang = _rope_angles(seq, half)
    cos, sin = jnp.cos(ang), jnp.sin(ang)
    zl = jnp.zeros((seq, C_NOPE), F32)
    zr = jnp.zeros((seq, LANES - C_QK), F32)
    cosk = jnp.concatenate([zl, cos, cos, zr], axis=1)
    sink = jnp.concatenate([zl, -sin, sin, zr], axis=1)
    return (shift, win.astype(BF16), c_q_lora_norm[None, :].astype(F32),
            c_kv_lora_norm[None, :].astype(F32), wqt, wvt, wk, gqn, gkn, cosk, sink, cos.T, sin.T,
            c_w_out.astype(BF16))


def _trunk(x, p):
    seq = x.shape[1]
    w_in, w_out, gain, gsum, cos, sin, band_tab, sink_col, bias_tab = _prep_ab(
        p["ab_w_in"][0], p["ab_w_out"][0], p["a_q_norm"][0], p["a_k_norm"][0], p["a_sink"][0],
        p["b_q_norm"][0], p["b_k_norm"][0], p["b_rpb"][0], seq)
    qa, ka, va, qb, kb, vb = _proj_ab(x, p["norm_mix"][0][None, :], w_in, gsum, gain, cos, sin)
    o = _attn_ab(qa, ka, va, band_tab, sink_col, qb, kb, vb, bias_tab)
    x = _out_ffn(x, o, w_out, p["norm_ffn"][0][None, :],
                 p["ffn_w_gate"][0].astype(BF16), p["ffn_w_up"][0].astype(BF16),
                 p["ffn_w_down"][0].astype(BF16))
    (shift, win, gq, gkv, wqt, wvt, wk, gqn, gkn, cosk, sink, cosq, sinq, wo) = _prep_mla(
        p["c_w_in"][0], p["c_q_lora_norm"][0], p["c_kv_lora_norm"][0], p["c_w_q_up"][0],
        p["c_w_kv_up"][0], p["c_q_norm"][0], p["c_k_norm"][0], p["c_w_out"][0], seq)
    qt, k, vt = _proj_mla(shift, x, p["norm_mix"][1][None, :], win, gq, gkv, wqt, wvt, wk, gqn, gkn,
                          cosk, sink, cosq, sinq)
    o = lax.cond(shift[0] <= SHIFT_MAX,
                 functools.partial(_attn_mla, bounded=True),
                 functools.partial(_attn_mla, bounded=False), qt, k, vt)
    x = _out_ffn(x, o, wo, p["norm_ffn"][1][None, :], p["ffn_w_gate"][1].astype(BF16),
                 p["ffn_w_up"][1].astype(BF16), p["ffn_w_down"][1].astype(BF16))
    return x


def kernel(x_prompt, x_sample, norm_mix, norm_ffn, ab_w_in, ab_w_out, a_q_norm, a_k_norm, a_sink,
           b_q_norm, b_k_norm, b_rpb, c_w_in, c_q_lora_norm, c_kv_lora_norm, c_w_q_up, c_w_kv_up,
           c_q_norm, c_k_norm, c_w_out, ffn_w_gate, ffn_w_up, ffn_w_down):
    p = dict(norm_mix=norm_mix, norm_ffn=norm_ffn, ab_w_in=ab_w_in, ab_w_out=ab_w_out,
             a_q_norm=a_q_norm, a_k_norm=a_k_norm, a_sink=a_sink, b_q_norm=b_q_norm,
             b_k_norm=b_k_norm, b_rpb=b_rpb, c_w_in=c_w_in, c_q_lora_norm=c_q_lora_norm,
             c_kv_lora_norm=c_kv_lora_norm, c_w_q_up=c_w_q_up, c_w_kv_up=c_w_kv_up,
             c_q_norm=c_q_norm, c_k_norm=c_k_norm, c_w_out=c_w_out, ffn_w_gate=ffn_w_gate,
             ffn_w_up=ffn_w_up, ffn_w_down=ffn_w_down)
    return _trunk(x_prompt, p), _trunk(x_sample, p)
```

```python
import functools
import math

import numpy as np
import jax
import jax.numpy as jnp
from jax import lax
from jax.experimental import pallas as pl
from jax.experimental.pallas import tpu as pltpu

F32 = jnp.float32
BF16 = jnp.bfloat16

D_MODEL = 1024
GRID_W = 64
HEAD_DIM = 64
ROPE_THETA = 10000.0
EPS = 1e-6
NEG_INF = -1e30
BLOCK = 128
A_HEADS = 8
A_KV_HEADS = 2
WINDOW = 128
B_HEADS = 8
NA_MAX_H = 8
NA_W = 16
C_HEADS = 16
C_NOPE = 64
C_ROPE = 32
C_VDIM = 64
C_QK = C_NOPE + C_ROPE
Q_LORA = 384
KV_LORA = 256
A_QW = A_HEADS * HEAD_DIM
A_KVW = A_KV_HEADS * HEAD_DIM
B_W = B_HEADS * HEAD_DIM

LANES = 128
V7X_VMEM_BYTES = 64 * 1024 * 1024
VMEM_LIMIT = V7X_VMEM_BYTES * 3 // 4

TM_PROJ = 512
TM_FFN = 512
TQ_MLA = 1024
TQ_MLA_BOUNDED = 2048
MLA_CHUNKS_PER_TRIP = 4
NA_ROWS = 8
A_QBLOCKS = 4
V_ROWS = C_VDIM + 16

PERM_A = (0, 4, 1, 5, 2, 6, 3, 7)

LOG2E = math.log2(math.e)
SHIFT_MAX = 60.0
BF16_NORM_MARGIN = 1.0 + 2.0 ** -6
NT_DIMS = (((1,), (1,)), ((), ()))


def _cparams(*sem):
    return pltpu.CompilerParams(dimension_semantics=sem, vmem_limit_bytes=VMEM_LIMIT)


def _resident(shape):
    nd = len(shape)
    return pl.BlockSpec(shape, lambda *_: (0,) * nd, pipeline_mode=pl.Buffered(1))


def _rms_rows(x, gain):
    ms = jnp.mean(x * x, axis=-1, keepdims=True)
    return x * lax.rsqrt(ms + EPS) * gain


def _proj_ab_kernel(x_ref, g_ref, w_ref, gsum_ref, gain_ref, cos_ref, sin_ref,
                    qa_ref, ka_ref, va_ref, qb_ref, kb_ref, vb_ref):
    h = _rms_rows(x_ref[0], g_ref[...]).astype(BF16)
    y = jnp.dot(h, w_ref[...], preferred_element_type=F32)
    gsum = gsum_ref[...]
    cos = cos_ref[...]
    sin = sin_ref[...]
    lane = lax.broadcasted_iota(jnp.int32, cos.shape, 1)
    first_half = (lane % HEAD_DIM) < (HEAD_DIM // 2)

    def head_norm(c):
        yc = y[:, 2 * LANES * c:2 * LANES * (c + 1)]
        ss = jnp.dot((yc * yc).astype(BF16), gsum, preferred_element_type=F32)
        return yc * lax.rsqrt(ss * (1.0 / HEAD_DIM) + EPS) * gain_ref[:, 2 * LANES * c:2 * LANES * (c + 1)]

    def rope(v):
        swapped = jnp.where(first_half, pltpu.roll(v, LANES - HEAD_DIM // 2, 1),
                            pltpu.roll(v, HEAD_DIM // 2, 1))
        return v * cos + swapped * sin

    for c in range(2):
        yn = head_norm(c)
        for b in range(2):
            qa_ref[0, :, LANES * (2 * c + b):LANES * (2 * c + b + 1)] = rope(
                yn[:, LANES * b:LANES * (b + 1)]).astype(BF16)
    for c in range(2):
        qb_ref[0, :, 2 * LANES * c:2 * LANES * (c + 1)] = head_norm(2 + c).astype(BF16)
    for c in range(2):
        kb_ref[0, :, 2 * LANES * c:2 * LANES * (c + 1)] = head_norm(4 + c).astype(BF16)
    ka_ref[0] = rope(head_norm(6)[:, :LANES]).astype(BF16)
    va_ref[0] = y[:, 13 * LANES:14 * LANES].astype(BF16)
    vb_ref[0] = y[:, 14 * LANES:18 * LANES].astype(BF16)


def _proj_ab(x, g, w, gsum, gain, cos, sin):
    bsz, seq, _ = x.shape
    tm = TM_PROJ
    nt = seq // tm
    tok = lambda width: pl.BlockSpec((1, tm, width), lambda b, i: (b, i, 0))
    tab = pl.BlockSpec((tm, LANES), lambda b, i: (i, 0))
    out = lambda width: jax.ShapeDtypeStruct((bsz, seq, width), BF16)
    return pl.pallas_call(
        _proj_ab_kernel,
        grid=(bsz, nt),
        in_specs=[tok(D_MODEL), _resident(g.shape), _resident(w.shape), _resident(gsum.shape),
                  _resident(gain.shape), tab, tab],
        out_specs=[tok(A_QW), tok(A_KVW), tok(A_KVW), tok(B_W), tok(B_W), tok(B_W)],
        out_shape=[out(A_QW), out(A_KVW), out(A_KVW), out(B_W), out(B_W), out(B_W)],
        compiler_params=_cparams("parallel", "parallel"),
        name="proj_ab",
    )(x, g, w, gsum, gain, cos, sin)


def _attn_a_body(q_ref, kp_ref, kc_ref, kn_ref, vp_ref, vc_ref, vn_ref, bias_ref, sink_ref, o_ref,
                 *, nsteps, lane0):
    n = pl.program_id(1)
    k = jnp.concatenate([kp_ref[0], kc_ref[0], kn_ref[0]], axis=0)
    v = jnp.concatenate([vp_ref[0], vc_ref[0], vn_ref[0]], axis=0)
    vext = jnp.concatenate([v, jnp.ones(v.shape, BF16)], axis=1)
    lane = lax.broadcasted_iota(jnp.int32, (BLOCK, LANES), 1)
    lo = lane < HEAD_DIM
    zero = jnp.zeros((BLOCK, LANES), BF16)
    for t in range(A_QBLOCKS):
        if t == 0:
            bias = bias_ref[jnp.where(n == 0, 0, 1)]
        elif t == A_QBLOCKS - 1:
            bias = bias_ref[jnp.where(n == nsteps - 1, 2, 1)]
        else:
            bias = bias_ref[1]
        kw = k[BLOCK * t:BLOCK * (t + 3)]
        vw = vext[BLOCK * t:BLOCK * (t + 3)]
        for blk in range(A_QW // LANES):
            qp = q_ref[0, BLOCK * t:BLOCK * (t + 1), LANES * blk:LANES * (blk + 1)]
            qm = jnp.concatenate([jnp.where(lo, qp, zero), jnp.where(lo, zero, qp)], axis=0)
            s = lax.dot_general(qm, kw, NT_DIMS, preferred_element_type=F32) + bias
            sink = sink_ref[blk]
            m = jnp.maximum(jnp.max(s, axis=-1, keepdims=True), sink)
            p = jnp.exp2(s - m)
            oe = jnp.dot(p.astype(BF16), vw, preferred_element_type=F32)
            o = oe[:, :LANES] / (oe[:, LANES:] + jnp.exp2(sink - m))
            o_ref[0, BLOCK * t:BLOCK * (t + 1), lane0 + LANES * blk:lane0 + LANES * (blk + 1)] = (
                jnp.where(lo, o[:BLOCK], o[BLOCK:]).astype(BF16))


def _attn_b_body(q_ref, kp_ref, kc_ref, kn_ref, vp_ref, vc_ref, vn_ref, bias_ref, o_ref,
                 kbuf, vbuf, *, nrb, lane0):
    rb = pl.program_id(1)
    blk_tok = NA_ROWS * GRID_W
    half_tok = blk_tok // 2
    win_tok = NA_MAX_H * GRID_W
    kbuf[0:half_tok] = kp_ref[0, half_tok:blk_tok]
    kbuf[half_tok:half_tok + blk_tok] = kc_ref[0]
    kbuf[half_tok + blk_tok:2 * blk_tok] = kn_ref[0, 0:half_tok]
    vbuf[0:half_tok] = vp_ref[0, half_tok:blk_tok]
    vbuf[half_tok:half_tok + blk_tok] = vc_ref[0]
    vbuf[half_tok + blk_tok:2 * blk_tok] = vn_ref[0, 0:half_tok]
    lane = lax.broadcasted_iota(jnp.int32, (GRID_W, LANES), 1)
    lo = lane < HEAD_DIM
    zero = jnp.zeros((GRID_W, LANES), BF16)
    ones = jnp.ones((win_tok, LANES), BF16)
    mid = NA_MAX_H // 2
    for t in range(NA_ROWS):
        off = jnp.where(rb == 0, max(t, mid), jnp.where(rb == nrb - 1, min(t, mid), t))
        didx = t + mid - off
        kstart = pl.multiple_of(off * GRID_W, GRID_W)
        for blk in range(B_W // LANES):
            qp = q_ref[0, GRID_W * t:GRID_W * (t + 1), LANES * blk:LANES * (blk + 1)]
            qm = jnp.concatenate([jnp.where(lo, qp, zero), jnp.where(lo, zero, qp)], axis=0)
            kw = kbuf[pl.ds(kstart, win_tok), LANES * blk:LANES * (blk + 1)]
            vw = vbuf[pl.ds(kstart, win_tok), LANES * blk:LANES * (blk + 1)]
            s = lax.dot_general(qm, kw, NT_DIMS, preferred_element_type=F32)
            s = s + bias_ref[didx, blk]
            p = jnp.exp2(s - jnp.max(s, axis=-1, keepdims=True))
            oe = jnp.dot(p.astype(BF16), jnp.concatenate([vw, ones], axis=1),
                         preferred_element_type=F32)
            o = oe[:, :LANES] / oe[:, LANES:]
            o_ref[0, GRID_W * t:GRID_W * (t + 1), lane0 + LANES * blk:lane0 + LANES * (blk + 1)] = (
                jnp.where(lo, o[:GRID_W], o[GRID_W:]).astype(BF16))


def _attn_ab_kernel(*refs, nsteps):
    a_refs, b_refs, o_ref, scratch = refs[:9], refs[9:17], refs[17], refs[18:]
    _attn_a_body(*a_refs, o_ref, nsteps=nsteps, lane0=0)
    _attn_b_body(*b_refs, o_ref, *scratch, nrb=nsteps, lane0=A_QW)


def _attn_ab(qa, ka, va, band_tab, sink_col, qb, kb, vb, bias_tab):
    bsz, seq, _ = qa.shape
    tok = A_QBLOCKS * BLOCK
    assert tok == NA_ROWS * GRID_W
    nsteps = seq // tok
    assert nsteps >= 2, "neighbourhood attention needs at least two row blocks"
    nb = seq // BLOCK
    cur = lambda b, n: (b, n, 0)
    a_edge = lambda f: pl.BlockSpec((1, BLOCK, A_KVW), f)
    a_prev = lambda b, n: (b, jnp.maximum(A_QBLOCKS * n - 1, 0), 0)
    a_next = lambda b, n: (b, jnp.minimum(A_QBLOCKS * (n + 1), nb - 1), 0)
    a_mid = pl.BlockSpec((1, tok, A_KVW), cur)
    b_blk = lambda f: pl.BlockSpec((1, tok, B_W), f)
    b_prev = lambda b, n: (b, jnp.maximum(n - 1, 0), 0)
    b_next = lambda b, n: (b, jnp.minimum(n + 1, nsteps - 1), 0)
    return pl.pallas_call(
        functools.partial(_attn_ab_kernel, nsteps=nsteps),
        grid=(bsz, nsteps),
        in_specs=[pl.BlockSpec((1, tok, A_QW), cur), a_edge(a_prev), a_mid, a_edge(a_next),
                  a_edge(a_prev), a_mid, a_edge(a_next), _resident(band_tab.shape),
                  _resident(sink_col.shape),
                  b_blk(cur), b_blk(b_prev), b_blk(cur), b_blk(b_next),
                  b_blk(b_prev), b_blk(cur), b_blk(b_next), _resident(bias_tab.shape)],
        out_specs=pl.BlockSpec((1, tok, A_QW + B_W), cur),
        out_shape=jax.ShapeDtypeStruct((bsz, seq, A_QW + B_W), BF16),
        scratch_shapes=[pltpu.VMEM((2 * tok, B_W), BF16), pltpu.VMEM((2 * tok, B_W), BF16)],
        compiler_params=_cparams("parallel", "parallel"),
        name="attn_ab",
    )(qa, ka, ka, ka, va, va, va, band_tab, sink_col, qb, kb, kb, kb, vb, vb, vb, bias_tab)


def _out_ffn_kernel(x_ref, o_ref, wo_ref, g_ref, wg_ref, wu_ref, wd_ref, y_ref):
    x1 = x_ref[0] + jnp.dot(o_ref[0], wo_ref[...], preferred_element_type=F32)
    h = _rms_rows(x1, g_ref[...]).astype(BF16)
    gate = jnp.dot(h, wg_ref[...], preferred_element_type=F32)
    up = jnp.dot(h, wu_ref[...], preferred_element_type=F32)
    act = (gate / (1.0 + jnp.exp(-gate)) * up).astype(BF16)
    y_ref[0] = x1 + jnp.dot(act, wd_ref[...], preferred_element_type=F32)


def _out_ffn(x, o, wo, g, wg, wu, wd):
    bsz, seq, _ = x.shape
    tm = TM_FFN
    tok = lambda width: pl.BlockSpec((1, tm, width), lambda b, i: (b, i, 0))
    consts = (wo, g, wg, wu, wd)
    return pl.pallas_call(
        _out_ffn_kernel,
        grid=(bsz, seq // tm),
        in_specs=[tok(D_MODEL), tok(o.shape[-1]), *[_resident(c.shape) for c in consts]],
        out_specs=tok(D_MODEL),
        out_shape=jax.ShapeDtypeStruct(x.shape, F32),
        compiler_params=_cparams("parallel", "parallel"),
        name="out_ffn",
    )(x, o, *consts)


def _proj_mla_kernel(shift_ref, x_ref, g_ref, win_ref, gq_ref, gkv_ref, wqt_ref, wvt_ref, wk_ref,
                     gqn_ref, gkn_ref, cosk_ref, sink_ref, cosq_ref, sinq_ref,
                     q_ref, k_ref, v_ref):
    tm = x_ref.shape[1]
    shift = shift_ref[0]
    h = _rms_rows(x_ref[0], g_ref[...]).astype(BF16)
    y = jnp.dot(h, win_ref[...], preferred_element_type=F32)
    cq = _rms_rows(y[:, :Q_LORA], gq_ref[...]).astype(BF16)
    ckv = _rms_rows(y[:, Q_LORA:Q_LORA + KV_LORA], gkv_ref[...]).astype(BF16)
    kpe = y[:, Q_LORA + KV_LORA:]

    qt = lax.dot_general(wqt_ref[...], cq, NT_DIMS, preferred_element_type=F32)
    qt = qt.reshape(C_HEADS, LANES, tm)
    ssq = jnp.sum(qt * qt, axis=1, keepdims=True)
    qn = qt * lax.rsqrt(ssq * (1.0 / C_QK) + EPS) * gqn_ref[...][None]
    half = C_ROPE // 2
    r1 = qn[:, C_NOPE:C_NOPE + half]
    r2 = qn[:, C_NOPE + half:C_QK]
    cos = cosq_ref[...][None]
    sin = sinq_ref[...][None]
    pad_row = lax.broadcasted_iota(jnp.int32, (C_HEADS, LANES - C_QK, tm), 1)
    q_out = jnp.concatenate([qn[:, :C_NOPE], r1 * cos - r2 * sin, r2 * cos + r1 * sin,
                             jnp.where(pad_row == 0, -shift, 0.0)], axis=1)
    q_ref[0] = q_out.astype(BF16)

    vt = lax.dot_general(wvt_ref[...], ckv, NT_DIMS, preferred_element_type=F32)
    vt = vt.reshape(C_HEADS, C_VDIM, tm).astype(BF16)
    row = lax.broadcasted_iota(jnp.int32, (C_HEADS, V_ROWS - C_VDIM, tm), 1)
    v_ref[0, :, 0] = jnp.concatenate([vt, jnp.where(row == 0, 1.0, 0.0).astype(BF16)], axis=1)

    kn = jnp.dot(ckv, wk_ref[...], preferred_element_type=F32)
    gk = gkn_ref[...]
    ss_pe = jnp.sum(kpe * kpe, axis=-1, keepdims=True)
    kg = kpe * gk
    lane = lax.broadcasted_iota(jnp.int32, kg.shape, 1)
    swapped = jnp.where(lane < C_NOPE + half, pltpu.roll(kg, LANES - half, 1),
                        pltpu.roll(kg, half, 1))
    kr = kg * cosk_ref[...] + swapped * sink_ref[...]
    for hd in range(C_HEADS):
        kh = kn[:, LANES * hd:LANES * (hd + 1)]
        ss = jnp.sum(kh * kh, axis=-1, keepdims=True) + ss_pe
        kval = (kh * gk + kr) * lax.rsqrt(ss * (1.0 / C_QK) + EPS)
        k_ref[0, :, LANES * hd:LANES * (hd + 1)] = jnp.where(lane == C_QK, 1.0, kval).astype(BF16)


def _proj_mla(shift, x, g, win, gq, gkv, wqt, wvt, wk, gqn, gkn, cosk, sink, cosq, sinq):
    bsz, seq, _ = x.shape
    tm = TM_PROJ
    nt = seq // tm
    half = C_ROPE // 2
    return pl.pallas_call(
        _proj_mla_kernel,
        grid=(bsz, nt),
        in_specs=[pl.BlockSpec(memory_space=pltpu.SMEM),
                  pl.BlockSpec((1, tm, D_MODEL), lambda b, i: (b, i, 0)),
                  _resident(g.shape), _resident(win.shape), _resident(gq.shape),
                  _resident(gkv.shape), _resident(wqt.shape), _resident(wvt.shape),
                  _resident(wk.shape), _resident(gqn.shape), _resident(gkn.shape),
                  pl.BlockSpec((tm, LANES), lambda b, i: (i, 0)),
                  pl.BlockSpec((tm, LANES), lambda b, i: (i, 0)),
                  pl.BlockSpec((half, tm), lambda b, i: (0, i)),
                  pl.BlockSpec((half, tm), lambda b, i: (0, i))],
        out_specs=[pl.BlockSpec((1, C_HEADS, LANES, tm), lambda b, i: (b, 0, 0, i)),
                   pl.BlockSpec((1, tm, C_HEADS * LANES), lambda b, i: (b, i, 0)),
                   pl.BlockSpec((1, C_HEADS, 1, V_ROWS, tm), lambda b, i: (b, 0, i, 0, 0))],
        out_shape=[jax.ShapeDtypeStruct((bsz, C_HEADS, LANES, seq), BF16),
                   jax.ShapeDtypeStruct((bsz, seq, C_HEADS * LANES), BF16),
                   jax.ShapeDtypeStruct((bsz, C_HEADS, nt, V_ROWS, tm), BF16)],
        compiler_params=_cparams("parallel", "parallel"),
        name="proj_mla",
    )(shift, x, g, win, gq, gkv, wqt, wvt, wk, gqn, gkn, cosk, sink, cosq, sinq)


def _attn_mla_kernel(q_ref, k_ref, v_ref, o_ref, s_buf, cmax_buf, p_buf, alpha_buf, m_buf, acc_buf,
                     *, nchunks, tk):
    heads = range(2)

    def stage_a(c, slot):
        start = pl.multiple_of(c * tk, tk)
        for hh in heads:
            kc = k_ref[0, pl.ds(start, tk), LANES * hh:LANES * (hh + 1)]
            s = jnp.dot(kc, q_ref[0, hh], preferred_element_type=F32)
            s_buf[hh, slot] = s
            cmax_buf[hh, slot] = jnp.max(s, axis=0, keepdims=True)

    def stage_b(slot):
        for hh in heads:
            m_old = m_buf[hh]
            m_new = jnp.maximum(m_old, cmax_buf[hh, slot])
            alpha_buf[hh, slot] = jnp.exp2(m_old - m_new)
            m_buf[hh] = m_new
            p_buf[hh, slot] = jnp.exp2(s_buf[hh, slot] - m_new).astype(BF16)

    def stage_c(c, slot):
        for hh in heads:
            pv = jnp.dot(v_ref[0, hh, c], p_buf[hh, slot], preferred_element_type=F32)
            acc_buf[hh] = alpha_buf[hh, slot] * acc_buf[hh] + pv

    m_buf[...] = jnp.full(m_buf.shape, NEG_INF, F32)
    acc_buf[...] = jnp.zeros(acc_buf.shape, F32)
    p_buf[:, 1] = jnp.zeros(p_buf.shape[:1] + p_buf.shape[2:], BF16)
    alpha_buf[:, 1] = jnp.ones(alpha_buf.shape[:1] + alpha_buf.shape[2:], F32)
    stage_a(0, 0)

    def body(i, carry):
        c0 = 2 * i
        stage_b(0)
        stage_a(c0 + 1, 1)
        stage_c(jnp.maximum(c0 - 1, 0), 1)
        stage_b(1)
        stage_a(jnp.minimum(c0 + 2, nchunks - 1), 0)
        stage_c(c0, 0)
        return carry

    lax.fori_loop(0, nchunks // 2, body, 0)
    stage_c(nchunks - 1, 1)
    outs = [acc_buf[hh, :C_VDIM] / acc_buf[hh, C_VDIM:C_VDIM + 1] for hh in heads]
    o_ref[0] = jnp.concatenate(outs, axis=0).T.astype(BF16)


def _attn_mla_bounded_kernel(q_ref, k_ref, v_ref, o_ref, p_buf, acc_buf, *, nchunks, tk):
    heads = range(2)

    def stage_p(c, slot):
        start = pl.multiple_of(c * tk, tk)
        for hh in heads:
            kc = k_ref[0, pl.ds(start, tk), LANES * hh:LANES * (hh + 1)]
            s = jnp.dot(kc, q_ref[0, hh], preferred_element_type=F32)
            p_buf[hh, slot] = jnp.exp2(s).astype(BF16)

    def stage_c(c, slot):
        for hh in heads:
            acc_buf[hh] += jnp.dot(v_ref[0, hh, c], p_buf[hh, slot], preferred_element_type=F32)

    acc_buf[...] = jnp.zeros(acc_buf.shape, F32)
    stage_p(0, 0)

    def body(i, carry):
        c0 = MLA_CHUNKS_PER_TRIP * i
        for j in range(MLA_CHUNKS_PER_TRIP):
            nxt = c0 + j + 1
            if j == MLA_CHUNKS_PER_TRIP - 1:
                nxt = jnp.minimum(nxt, nchunks - 1)
            stage_p(nxt, (j + 1) % 2)
            stage_c(c0 + j, j % 2)
        return carry

    lax.fori_loop(0, nchunks // MLA_CHUNKS_PER_TRIP, body, 0)
    outs = [acc_buf[hh, :C_VDIM] / acc_buf[hh, C_VDIM:C_VDIM + 1] for hh in heads]
    o_ref[0] = jnp.concatenate(outs, axis=0).T.astype(BF16)


def _attn_mla(qt, k, vt, bounded):
    bsz, _, _, seq = qt.shape
    nchunks, tk = vt.shape[2], vt.shape[4]
    assert nchunks % MLA_CHUNKS_PER_TRIP == 0 and MLA_CHUNKS_PER_TRIP % 2 == 0
    tq = TQ_MLA_BOUNDED if bounded else TQ_MLA
    if bounded:
        body = _attn_mla_bounded_kernel
        scratch = [pltpu.VMEM((2, 2, tk, tq), BF16),
                   pltpu.VMEM((2, V_ROWS, tq), F32)]
    else:
        body = _attn_mla_kernel
        scratch = [pltpu.VMEM((2, 2, tk, tq), F32),
                   pltpu.VMEM((2, 2, 1, tq), F32),
                   pltpu.VMEM((2, 2, tk, tq), BF16),
                   pltpu.VMEM((2, 2, 1, tq), F32),
                   pltpu.VMEM((2, 1, tq), F32),
                   pltpu.VMEM((2, V_ROWS, tq), F32)]
    return pl.pallas_call(
        functools.partial(body, nchunks=nchunks, tk=tk),
        grid=(bsz, C_HEADS // 2, seq // tq),
        in_specs=[pl.BlockSpec((1, 2, LANES, tq), lambda b, h, i: (b, h, 0, i)),
                  pl.BlockSpec((1, seq, 2 * LANES), lambda b, h, i: (b, 0, h)),
                  pl.BlockSpec((1, 2, nchunks, V_ROWS, tk), lambda b, h, i: (b, h, 0, 0, 0))],
        out_specs=pl.BlockSpec((1, tq, 2 * C_VDIM), lambda b, h, i: (b, i, h)),
        out_shape=jax.ShapeDtypeStruct((bsz, seq, C_HEADS * C_VDIM), BF16),
        scratch_shapes=scratch,
        compiler_params=_cparams("parallel", "parallel", "arbitrary"),
        name="attn_mla_bounded" if bounded else "attn_mla",
    )(qt, k, vt)


def _rope_angles(seq, half):
    inv_freq = ROPE_THETA ** (-jnp.arange(half, dtype=F32) / half)
    return jnp.arange(seq, dtype=F32)[:, None] * inv_freq[None, :]


def _prep_ab(ab_w_in, ab_w_out, a_q_norm, a_k_norm, a_sink, b_q_norm, b_k_norm, b_rpb, seq):
    o_ka, o_va, o_qb = A_QW, A_QW + A_KVW, A_QW + 2 * A_KVW
    o_kb, o_vb = o_qb + B_W, o_qb + 2 * B_W
    w_in = jnp.concatenate(
        [ab_w_in[:, HEAD_DIM * h:HEAD_DIM * (h + 1)] for h in PERM_A]
        + [ab_w_in[:, o_qb:o_vb], ab_w_in[:, o_ka:o_qb], ab_w_in[:, o_vb:]], axis=1).astype(BF16)
    w_out = jnp.concatenate([ab_w_out[HEAD_DIM * h:HEAD_DIM * (h + 1)] for h in PERM_A]
                            + [ab_w_out[A_QW:]]).astype(BF16)
    scale = HEAD_DIM ** -0.5 * LOG2E
    gain = jnp.concatenate([jnp.tile(a_q_norm * scale, A_HEADS), jnp.tile(b_q_norm * scale, B_HEADS),
                            jnp.tile(b_k_norm, B_HEADS), jnp.tile(a_k_norm, A_KV_HEADS),
                            jnp.ones((A_KVW,), F32)])[None, :].astype(F32)
    idx = np.arange(2 * LANES) // HEAD_DIM
    gsum = jnp.asarray(idx[:, None] == idx[None, :], BF16)
    ang = _rope_angles(seq, HEAD_DIM // 2)
    cos = jnp.tile(jnp.cos(ang), (1, LANES // (HEAD_DIM // 2)))
    sin = jnp.tile(jnp.concatenate([-jnp.sin(ang), jnp.sin(ang)], axis=1), (1, LANES // HEAD_DIM))
    sink_col = jnp.repeat(a_sink[np.array(PERM_A)] * LOG2E, BLOCK).reshape(
        A_QW // LANES, 2 * BLOCK, 1).astype(F32)
    i = np.arange(2 * BLOCK)[:, None] % BLOCK
    j = np.arange(3 * BLOCK)[None, :]
    band = (j - i >= BLOCK - WINDOW) & (j - i <= BLOCK + WINDOW)
    band = np.stack([band & (j >= BLOCK), band, band & (j < 2 * BLOCK)])
    band_tab = jnp.asarray(np.where(band, 0.0, NEG_INF), F32)

    c = np.arange(GRID_W)
    cs = np.clip(c - NA_W // 2, 0, GRID_W - NA_W)
    inwin = (c[None, :] >= cs[:, None]) & (c[None, :] < cs[:, None] + NA_W)
    dc = c[None, :] - c[:, None] + NA_W - 1
    pick = jnp.asarray(dc[:, :, None] == np.arange(2 * NA_W - 1), F32)
    tcol = jnp.sum(b_rpb.astype(F32)[:, :, None, None, :] * pick[None, None], axis=-1)
    tab = jnp.stack([tcol[:, NA_MAX_H - 1 - d:2 * NA_MAX_H - 1 - d] for d in range(NA_MAX_H)],
                    axis=1)
    tab = jnp.where(inwin[None, None, None], tab * LOG2E, NEG_INF)
    tab = tab.transpose(1, 0, 3, 2, 4).reshape(NA_MAX_H, B_HEADS // 2, 2 * GRID_W, NA_MAX_H * GRID_W)
    return w_in, w_out, gain, gsum, cos, sin, band_tab, sink_col, tab


def _prep_mla(c_w_in, c_q_lora_norm, c_kv_lora_norm, c_w_q_up, c_w_kv_up, c_q_norm, c_k_norm,
              c_w_out, seq):
    win = jnp.zeros((D_MODEL, Q_LORA + KV_LORA + LANES), F32)
    win = win.at[:, :Q_LORA + KV_LORA].set(c_w_in[:, :Q_LORA + KV_LORA])
    win = win.at[:, Q_LORA + KV_LORA + C_NOPE:Q_LORA + KV_LORA + C_QK].set(c_w_in[:, Q_LORA + KV_LORA:])
    wq = c_w_q_up.reshape(Q_LORA, C_HEADS, C_QK)
    wq = jnp.pad(wq, ((0, 0), (0, 0), (0, LANES - C_QK)))
    wqt = wq.reshape(Q_LORA, C_HEADS * LANES).T.astype(BF16)
    wkv = c_w_kv_up.reshape(KV_LORA, C_HEADS, C_NOPE + C_VDIM)
    wvt = wkv[:, :, C_NOPE:].reshape(KV_LORA, C_HEADS * C_VDIM).T.astype(BF16)
    wk = jnp.pad(wkv[:, :, :C_NOPE], ((0, 0), (0, 0), (0, LANES - C_NOPE)))
    wk = wk.reshape(KV_LORA, C_HEADS * LANES).astype(BF16)
    qscale = C_QK ** -0.5 * LOG2E
    shift = (BF16_NORM_MARGIN * C_QK * jnp.max(jnp.abs(c_q_norm * qscale)) * jnp.max(jnp.abs(c_k_norm)))
    shift = shift.reshape(1).astype(F32)
    gqn = jnp.pad(c_q_norm * qscale, (0, LANES - C_QK))[:, None].astype(F32)
    gkn = jnp.pad(c_k_norm, (0, LANES - C_QK))[None, :].astype(F32)
    half = C_ROPE // 2
    ang = _rope_angles(seq, half)
    cos, sin = jnp.cos(ang), jnp.sin(ang)
    zl = jnp.zeros((seq, C_NOPE), F32)
    zr = jnp.zeros((seq, LANES - C_QK), F32)
    cosk = jnp.concatenate([zl, cos, cos, zr], axis=1)
    sink = jnp.concatenate([zl, -sin, sin, zr], axis=1)
    return (shift, win.astype(BF16), c_q_lora_norm[None, :].astype(F32),
            c_kv_lora_norm[None, :].astype(F32), wqt, wvt, wk, gqn, gkn, cosk, sink, cos.T, sin.T,
            c_w_out.astype(BF16))


def _trunk(x, p):
    seq = x.shape[1]
    w_in, w_out, gain, gsum, cos, sin, band_tab, sink_col, bias_tab = _prep_ab(
        p["ab_w_in"][0], p["ab_w_out"][0], p["a_q_norm"][0], p["a_k_norm"][0], p["a_sink"][0],
        p["b_q_norm"][0], p["b_k_norm"][0], p["b_rpb"][0], seq)
    qa, ka, va, qb, kb, vb = _proj_ab(x, p["norm_mix"][0][None, :], w_in, gsum, gain, cos, sin)
    o = _attn_ab(qa, ka, va, band_tab, sink_col, qb, kb, vb, bias_tab)
    x = _out_ffn(x, o, w_out, p["norm_ffn"][0][None, :],
                 p["ffn_w_gate"][0].astype(BF16), p["ffn_w_up"][0].astype(BF16),
                 p["ffn_w_down"][0].astype(BF16))
    (shift, win, gq, gkv, wqt, wvt, wk, gqn, gkn, cosk, sink, cosq, sinq, wo) = _prep_mla(
        p["c_w_in"][0], p["c_q_lora_norm"][0], p["c_kv_lora_norm"][0], p["c_w_q_up"][0],
        p["c_w_kv_up"][0], p["c_q_norm"][0], p["c_k_norm"][0], p["c_w_out"][0], seq)
    qt, k, vt = _proj_mla(shift, x, p["norm_mix"][1][None, :], win, gq, gkv, wqt, wvt, wk, gqn, gkn,
                          cosk, sink, cosq, sinq)
    o = lax.cond(shift[0] <= SHIFT_MAX,
                 functools.partial(_attn_mla, bounded=True),
                 functools.partial(_attn_mla, bounded=False), qt, k, vt)
    x = _out_ffn(x, o, wo, p["norm_ffn"][1][None, :], p["ffn_w_gate"][1].astype(BF16),
                 p["ffn_w_up"][1].astype(BF16), p["ffn_w_down"][1].astype(BF16))
    return x


def kernel(x_prompt, x_sample, norm_mix, norm_ffn, ab_w_in, ab_w_out, a_q_norm, a_k_norm, a_sink,
           b_q_norm, b_k_norm, b_rpb, c_w_in, c_q_lora_norm, c_kv_lora_norm, c_w_q_up, c_w_kv_up,
           c_q_norm, c_k_norm, c_w_out, ffn_w_gate, ffn_w_up, ffn_w_down):
    p = dict(norm_mix=norm_mix, norm_ffn=norm_ffn, ab_w_in=ab_w_in, ab_w_out=ab_w_out,
             a_q_norm=a_q_norm, a_k_norm=a_k_norm, a_sink=a_sink, b_q_norm=b_q_norm,
             b_k_norm=b_k_norm, b_rpb=b_rpb, c_w_in=c_w_in, c_q_lora_norm=c_q_lora_norm,
             c_kv_lora_norm=c_kv_lora_norm, c_w_q_up=c_w_q_up, c_w_kv_up=c_w_kv_up,
             c_q_norm=c_q_norm, c_k_norm=c_k_norm, c_w_out=c_w_out, ffn_w_gate=ffn_w_gate,
             ffn_w_up=ffn_w_up, ffn_w_down=ffn_w_down)
    return _trunk(x_prompt, p), _trunk(x_sample, p)
```

```python
import functools
import math

import numpy as np
import jax
import jax.numpy as jnp
from jax import lax
from jax.experimental import pallas as pl
from jax.experimental.pallas import tpu as pltpu

F32 = jnp.float32
BF16 = jnp.bfloat16

D_MODEL = 1024
GRID_W = 64
HEAD_DIM = 64
ROPE_THETA = 10000.0
EPS = 1e-6
NEG_INF = -1e30
BLOCK = 128
A_HEADS = 8
A_KV_HEADS = 2
WINDOW = 128
B_HEADS = 8
NA_MAX_H = 8
NA_W = 16
C_HEADS = 16
C_NOPE = 64
C_ROPE = 32
C_VDIM = 64
C_QK = C_NOPE + C_ROPE
Q_LORA = 384
KV_LORA = 256
A_QW = A_HEADS * HEAD_DIM
A_KVW = A_KV_HEADS * HEAD_DIM
B_W = B_HEADS * HEAD_DIM

LANES = 128
V7X_VMEM_BYTES = 64 * 1024 * 1024
VMEM_LIMIT = V7X_VMEM_BYTES * 3 // 4

TM_PROJ = 512
TM_FFN = 512
TQ_MLA = 1024
TQ_MLA_BOUNDED = 2048
MLA_CHUNKS_PER_TRIP = 4
NA_ROWS = 8
A_QBLOCKS = 4
V_ROWS = C_VDIM + 16

PERM_A = (0, 4, 1, 5, 2, 6, 3, 7)

LOG2E = math.log2(math.e)
SHIFT_MAX = 60.0
BF16_NORM_MARGIN = 1.0 + 2.0 ** -6
NT_DIMS = (((1,), (1,)), ((), ()))


def _cparams(*sem):
    return pltpu.CompilerParams(dimension_semantics=sem, vmem_limit_bytes=VMEM_LIMIT)


def _resident(shape):
    nd = len(shape)
    return pl.BlockSpec(shape, lambda *_: (0,) * nd, pipeline_mode=pl.Buffered(1))


def _rms_rows(x, gain):
    ms = jnp.mean(x * x, axis=-1, keepdims=True)
    return x * lax.rsqrt(ms + EPS) * gain


def _proj_ab_kernel(x_ref, g_ref, w_ref, gsum_ref, gain_ref, cos_ref, sin_ref,
                    qa_ref, ka_ref, va_ref, qb_ref, kb_ref, vb_ref):
    h = _rms_rows(x_ref[0], g_ref[...]).astype(BF16)
    y = jnp.dot(h, w_ref[...], preferred_element_type=F32)
    gsum = gsum_ref[...]
    cos = cos_ref[...]
    sin = sin_ref[...]
    lane = lax.broadcasted_iota(jnp.int32, cos.shape, 1)
    first_half = (lane % HEAD_DIM) < (HEAD_DIM // 2)

    def head_norm(c):
        yc = y[:, 2 * LANES * c:2 * LANES * (c + 1)]
        ss = jnp.dot((yc * yc).astype(BF16), gsum, preferred_element_type=F32)
        return yc * lax.rsqrt(ss * (1.0 / HEAD_DIM) + EPS) * gain_ref[:, 2 * LANES * c:2 * LANES * (c + 1)]

    def rope(v):
        swapped = jnp.where(first_half, pltpu.roll(v, LANES - HEAD_DIM // 2, 1),
                            pltpu.roll(v, HEAD_DIM // 2, 1))
        return v * cos + swapped * sin

    for c in range(2):
        yn = head_norm(c)
        for b in range(2):
            qa_ref[0, :, LANES * (2 * c + b):LANES * (2 * c + b + 1)] = rope(
                yn[:, LANES * b:LANES * (b + 1)]).astype(BF16)
    for c in range(2):
        qb_ref[0, :, 2 * LANES * c:2 * LANES * (c + 1)] = head_norm(2 + c).astype(BF16)
    for c in range(2):
        kb_ref[0, :, 2 * LANES * c:2 * LANES * (c + 1)] = head_norm(4 + c).astype(BF16)
    ka_ref[0] = rope(head_norm(6)[:, :LANES]).astype(BF16)
    va_ref[0] = y[:, 13 * LANES:14 * LANES].astype(BF16)
    vb_ref[0] = y[:, 14 * LANES:18 * LANES].astype(BF16)


def _proj_ab(x, g, w, gsum, gain, cos, sin):
    bsz, seq, _ = x.shape
    tm = TM_PROJ
    nt = seq // tm
    tok = lambda width: pl.BlockSpec((1, tm, width), lambda b, i: (b, i, 0))
    tab = pl.BlockSpec((tm, LANES), lambda b, i: (i, 0))
    out = lambda width: jax.ShapeDtypeStruct((bsz, seq, width), BF16)
    return pl.pallas_call(
        _proj_ab_kernel,
        grid=(bsz, nt),
        in_specs=[tok(D_MODEL), _resident(g.shape), _resident(w.shape), _resident(gsum.shape),
                  _resident(gain.shape), tab, tab],
        out_specs=[tok(A_QW), tok(A_KVW), tok(A_KVW), tok(B_W), tok(B_W), tok(B_W)],
        out_shape=[out(A_QW), out(A_KVW), out(A_KVW), out(B_W), out(B_W), out(B_W)],
        compiler_params=_cparams("parallel", "parallel"),
        name="proj_ab",
    )(x, g, w, gsum, gain, cos, sin)


def _attn_a_body(q_ref, kp_ref, kc_ref, kn_ref, vp_ref, vc_ref, vn_ref, bias_ref, sink_ref, o_ref,
                 *, nsteps, lane0, bounded):
    n = pl.program_id(1)
    k = jnp.concatenate([kp_ref[0], kc_ref[0], kn_ref[0]], axis=0)
    v = jnp.concatenate([vp_ref[0], vc_ref[0], vn_ref[0]], axis=0)
    vext = jnp.concatenate([v, jnp.ones(v.shape, BF16)], axis=1)
    lane = lax.broadcasted_iota(jnp.int32, (BLOCK, LANES), 1)
    lo = lane < HEAD_DIM
    zero = jnp.zeros((BLOCK, LANES), BF16)
    for t in range(A_QBLOCKS):
        if t == 0:
            bias = bias_ref[jnp.where(n == 0, 0, 1)]
        elif t == A_QBLOCKS - 1:
            bias = bias_ref[jnp.where(n == nsteps - 1, 2, 1)]
        else:
            bias = bias_ref[1]
        kw = k[BLOCK * t:BLOCK * (t + 3)]
        vw = vext[BLOCK * t:BLOCK * (t + 3)]
        for blk in range(A_QW // LANES):
            qp = q_ref[0, BLOCK * t:BLOCK * (t + 1), LANES * blk:LANES * (blk + 1)]
            qm = jnp.concatenate([jnp.where(lo, qp, zero), jnp.where(lo, zero, qp)], axis=0)
            s = lax.dot_general(qm, kw, NT_DIMS, preferred_element_type=F32) + bias
            if bounded:
                p = jnp.exp2(s)
                sink_term = sink_ref[blk]
            else:
                sink = sink_ref[blk]
                m = jnp.maximum(jnp.max(s, axis=-1, keepdims=True), sink)
                p = jnp.exp2(s - m)
                sink_term = jnp.exp2(sink - m)
            oe = jnp.dot(p.astype(BF16), vw, preferred_element_type=F32)
            o = oe[:, :LANES] / (oe[:, LANES:] + sink_term)
            o_ref[0, BLOCK * t:BLOCK * (t + 1), lane0 + LANES * blk:lane0 + LANES * (blk + 1)] = (
                jnp.where(lo, o[:BLOCK], o[BLOCK:]).astype(BF16))


def _attn_b_body(q_ref, kp_ref, kc_ref, kn_ref, vp_ref, vc_ref, vn_ref, bias_ref, o_ref,
                 kbuf, vbuf, *, nrb, lane0, bounded):
    rb = pl.program_id(1)
    blk_tok = NA_ROWS * GRID_W
    half_tok = blk_tok // 2
    win_tok = NA_MAX_H * GRID_W
    kbuf[0:half_tok] = kp_ref[0, half_tok:blk_tok]
    kbuf[half_tok:half_tok + blk_tok] = kc_ref[0]
    kbuf[half_tok + blk_tok:2 * blk_tok] = kn_ref[0, 0:half_tok]
    vbuf[0:half_tok] = vp_ref[0, half_tok:blk_tok]
    vbuf[half_tok:half_tok + blk_tok] = vc_ref[0]
    vbuf[half_tok + blk_tok:2 * blk_tok] = vn_ref[0, 0:half_tok]
    lane = lax.broadcasted_iota(jnp.int32, (GRID_W, LANES), 1)
    lo = lane < HEAD_DIM
    zero = jnp.zeros((GRID_W, LANES), BF16)
    ones = jnp.ones((win_tok, LANES), BF16)
    mid = NA_MAX_H // 2
    for t in range(NA_ROWS):
        off = jnp.where(rb == 0, max(t, mid), jnp.where(rb == nrb - 1, min(t, mid), t))
        didx = t + mid - off
        kstart = pl.multiple_of(off * GRID_W, GRID_W)
        for blk in range(B_W // LANES):
            qp = q_ref[0, GRID_W * t:GRID_W * (t + 1), LANES * blk:LANES * (blk + 1)]
            qm = jnp.concatenate([jnp.where(lo, qp, zero), jnp.where(lo, zero, qp)], axis=0)
            kw = kbuf[pl.ds(kstart, win_tok), LANES * blk:LANES * (blk + 1)]
            vw = vbuf[pl.ds(kstart, win_tok), LANES * blk:LANES * (blk + 1)]
            s = lax.dot_general(qm, kw, NT_DIMS, preferred_element_type=F32)
            s = s + bias_ref[didx, blk]
            p = jnp.exp2(s) if bounded else jnp.exp2(s - jnp.max(s, axis=-1, keepdims=True))
            oe = jnp.dot(p.astype(BF16), jnp.concatenate([vw, ones], axis=1),
                         preferred_element_type=F32)
            o = oe[:, :LANES] / oe[:, LANES:]
            o_ref[0, GRID_W * t:GRID_W * (t + 1), lane0 + LANES * blk:lane0 + LANES * (blk + 1)] = (
                jnp.where(lo, o[:GRID_W], o[GRID_W:]).astype(BF16))


def _attn_ab_kernel(*refs, nsteps, bounded):
    a_refs, b_refs, o_ref, scratch = refs[:9], refs[9:17], refs[17], refs[18:]
    _attn_a_body(*a_refs, o_ref, nsteps=nsteps, lane0=0, bounded=bounded)
    _attn_b_body(*b_refs, o_ref, *scratch, nrb=nsteps, lane0=A_QW, bounded=bounded)


def _attn_ab(qa, ka, va, band_tab, sink_col, qb, kb, vb, bias_tab, bounded):
    bsz, seq, _ = qa.shape
    tok = A_QBLOCKS * BLOCK
    assert tok == NA_ROWS * GRID_W
    nsteps = seq // tok
    assert nsteps >= 2, "neighbourhood attention needs at least two row blocks"
    nb = seq // BLOCK
    cur = lambda b, n: (b, n, 0)
    a_edge = lambda f: pl.BlockSpec((1, BLOCK, A_KVW), f)
    a_prev = lambda b, n: (b, jnp.maximum(A_QBLOCKS * n - 1, 0), 0)
    a_next = lambda b, n: (b, jnp.minimum(A_QBLOCKS * (n + 1), nb - 1), 0)
    a_mid = pl.BlockSpec((1, tok, A_KVW), cur)
    b_blk = lambda f: pl.BlockSpec((1, tok, B_W), f)
    b_prev = lambda b, n: (b, jnp.maximum(n - 1, 0), 0)
    b_next = lambda b, n: (b, jnp.minimum(n + 1, nsteps - 1), 0)
    return pl.pallas_call(
        functools.partial(_attn_ab_kernel, nsteps=nsteps, bounded=bounded),
        grid=(bsz, nsteps),
        in_specs=[pl.BlockSpec((1, tok, A_QW), cur), a_edge(a_prev), a_mid, a_edge(a_next),
                  a_edge(a_prev), a_mid, a_edge(a_next), _resident(band_tab.shape),
                  _resident(sink_col.shape),
                  b_blk(cur), b_blk(b_prev), b_blk(cur), b_blk(b_next),
                  b_blk(b_prev), b_blk(cur), b_blk(b_next), _resident(bias_tab.shape)],
        out_specs=pl.BlockSpec((1, tok, A_QW + B_W), cur),
        out_shape=jax.ShapeDtypeStruct((bsz, seq, A_QW + B_W), BF16),
        scratch_shapes=[pltpu.VMEM((2 * tok, B_W), BF16), pltpu.VMEM((2 * tok, B_W), BF16)],
        compiler_params=_cparams("parallel", "parallel"),
        name="attn_ab_bounded" if bounded else "attn_ab",
    )(qa, ka, ka, ka, va, va, va, band_tab, sink_col, qb, kb, kb, kb, vb, vb, vb, bias_tab)


def _out_ffn_kernel(x_ref, o_ref, wo_ref, g_ref, wg_ref, wu_ref, wd_ref, y_ref):
    x1 = x_ref[0] + jnp.dot(o_ref[0], wo_ref[...], preferred_element_type=F32)
    h = _rms_rows(x1, g_ref[...]).astype(BF16)
    gate = jnp.dot(h, wg_ref[...], preferred_element_type=F32)
    up = jnp.dot(h, wu_ref[...], preferred_element_type=F32)
    act = (gate / (1.0 + jnp.exp(-gate)) * up).astype(BF16)
    y_ref[0] = x1 + jnp.dot(act, wd_ref[...], preferred_element_type=F32)


def _out_ffn(x, o, wo, g, wg, wu, wd):
    bsz, seq, _ = x.shape
    tm = TM_FFN
    tok = lambda width: pl.BlockSpec((1, tm, width), lambda b, i: (b, i, 0))
    consts = (wo, g, wg, wu, wd)
    return pl.pallas_call(
        _out_ffn_kernel,
        grid=(bsz, seq // tm),
        in_specs=[tok(D_MODEL), tok(o.shape[-1]), *[_resident(c.shape) for c in consts]],
        out_specs=tok(D_MODEL),
        out_shape=jax.ShapeDtypeStruct(x.shape, F32),
        compiler_params=_cparams("parallel", "parallel"),
        name="out_ffn",
    )(x, o, *consts)


def _proj_mla_kernel(shift_ref, x_ref, g_ref, win_ref, gq_ref, gkv_ref, wqt_ref, wvt_ref, wk_ref,
                     gqn_ref, gkn_ref, cosk_ref, sink_ref, cosq_ref, sinq_ref,
                     q_ref, k_ref, v_ref):
    tm = x_ref.shape[1]
    shift = shift_ref[0]
    h = _rms_rows(x_ref[0], g_ref[...]).astype(BF16)
    y = jnp.dot(h, win_ref[...], preferred_element_type=F32)
    cq = _rms_rows(y[:, :Q_LORA], gq_ref[...]).astype(BF16)
    ckv = _rms_rows(y[:, Q_LORA:Q_LORA + KV_LORA], gkv_ref[...]).astype(BF16)
    kpe = y[:, Q_LORA + KV_LORA:]

    qt = lax.dot_general(wqt_ref[...], cq, NT_DIMS, preferred_element_type=F32)
    qt = qt.reshape(C_HEADS, LANES, tm)
    ssq = jnp.sum(qt * qt, axis=1, keepdims=True)
    qn = qt * lax.rsqrt(ssq * (1.0 / C_QK) + EPS) * gqn_ref[...][None]
    half = C_ROPE // 2
    r1 = qn[:, C_NOPE:C_NOPE + half]
    r2 = qn[:, C_NOPE + half:C_QK]
    cos = cosq_ref[...][None]
    sin = sinq_ref[...][None]
    pad_row = lax.broadcasted_iota(jnp.int32, (C_HEADS, LANES - C_QK, tm), 1)
    q_out = jnp.concatenate([qn[:, :C_NOPE], r1 * cos - r2 * sin, r2 * cos + r1 * sin,
                             jnp.where(pad_row == 0, -shift, 0.0)], axis=1)
    q_ref[0] = q_out.astype(BF16)

    vt = lax.dot_general(wvt_ref[...], ckv, NT_DIMS, preferred_element_type=F32)
    vt = vt.reshape(C_HEADS, C_VDIM, tm).astype(BF16)
    row = lax.broadcasted_iota(jnp.int32, (C_HEADS, V_ROWS - C_VDIM, tm), 1)
    v_ref[0, :, 0] = jnp.concatenate([vt, jnp.where(row == 0, 1.0, 0.0).astype(BF16)], axis=1)

    kn = jnp.dot(ckv, wk_ref[...], preferred_element_type=F32)
    gk = gkn_ref[...]
    ss_pe = jnp.sum(kpe * kpe, axis=-1, keepdims=True)
    kg = kpe * gk
    lane = lax.broadcasted_iota(jnp.int32, kg.shape, 1)
    swapped = jnp.where(lane < C_NOPE + half, pltpu.roll(kg, LANES - half, 1),
                        pltpu.roll(kg, half, 1))
    kr = kg * cosk_ref[...] + swapped * sink_ref[...]
    for hd in range(C_HEADS):
        kh = kn[:, LANES * hd:LANES * (hd + 1)]
        ss = jnp.sum(kh * kh, axis=-1, keepdims=True) + ss_pe
        kval = (kh * gk + kr) * lax.rsqrt(ss * (1.0 / C_QK) + EPS)
        k_ref[0, :, LANES * hd:LANES * (hd + 1)] = jnp.where(lane == C_QK, 1.0, kval).astype(BF16)


def _proj_mla(shift, x, g, win, gq, gkv, wqt, wvt, wk, gqn, gkn, cosk, sink, cosq, sinq):
    bsz, seq, _ = x.shape
    tm = TM_PROJ
    nt = seq // tm
    half = C_ROPE // 2
    return pl.pallas_call(
        _proj_mla_kernel,
        grid=(bsz, nt),
        in_specs=[pl.BlockSpec(memory_space=pltpu.SMEM),
                  pl.BlockSpec((1, tm, D_MODEL), lambda b, i: (b, i, 0)),
                  _resident(g.shape), _resident(win.shape), _resident(gq.shape),
                  _resident(gkv.shape), _resident(wqt.shape), _resident(wvt.shape),
                  _resident(wk.shape), _resident(gqn.shape), _resident(gkn.shape),
                  pl.BlockSpec((tm, LANES), lambda b, i: (i, 0)),
                  pl.BlockSpec((tm, LANES), lambda b, i: (i, 0)),
                  pl.BlockSpec((half, tm), lambda b, i: (0, i)),
                  pl.BlockSpec((half, tm), lambda b, i: (0, i))],
        out_specs=[pl.BlockSpec((1, C_HEADS, LANES, tm), lambda b, i: (b, 0, 0, i)),
                   pl.BlockSpec((1, tm, C_HEADS * LANES), lambda b, i: (b, i, 0)),
                   pl.BlockSpec((1, C_HEADS, 1, V_ROWS, tm), lambda b, i: (b, 0, i, 0, 0))],
        out_shape=[jax.ShapeDtypeStruct((bsz, C_HEADS, LANES, seq), BF16),
                   jax.ShapeDtypeStruct((bsz, seq, C_HEADS * LANES), BF16),
                   jax.ShapeDtypeStruct((bsz, C_HEADS, nt, V_ROWS, tm), BF16)],
        compiler_params=_cparams("parallel", "parallel"),
        name="proj_mla",
    )(shift, x, g, win, gq, gkv, wqt, wvt, wk, gqn, gkn, cosk, sink, cosq, sinq)


def _attn_mla_kernel(q_ref, k_ref, v_ref, o_ref, s_buf, cmax_buf, p_buf, alpha_buf, m_buf, acc_buf,
                     *, nchunks, tk):
    heads = range(2)

    def stage_a(c, slot):
        start = pl.multiple_of(c * tk, tk)
        for hh in heads:
            kc = k_ref[0, pl.ds(start, tk), LANES * hh:LANES * (hh + 1)]
            s = jnp.dot(kc, q_ref[0, hh], preferred_element_type=F32)
            s_buf[hh, slot] = s
            cmax_buf[hh, slot] = jnp.max(s, axis=0, keepdims=True)

    def stage_b(slot):
        for hh in heads:
            m_old = m_buf[hh]
            m_new = jnp.maximum(m_old, cmax_buf[hh, slot])
            alpha_buf[hh, slot] = jnp.exp2(m_old - m_new)
            m_buf[hh] = m_new
            p_buf[hh, slot] = jnp.exp2(s_buf[hh, slot] - m_new).astype(BF16)

    def stage_c(c, slot):
        for hh in heads:
            pv = jnp.dot(v_ref[0, hh, c], p_buf[hh, slot], preferred_element_type=F32)
            acc_buf[hh] = alpha_buf[hh, slot] * acc_buf[hh] + pv

    m_buf[...] = jnp.full(m_buf.shape, NEG_INF, F32)
    acc_buf[...] = jnp.zeros(acc_buf.shape, F32)
    p_buf[:, 1] = jnp.zeros(p_buf.shape[:1] + p_buf.shape[2:], BF16)
    alpha_buf[:, 1] = jnp.ones(alpha_buf.shape[:1] + alpha_buf.shape[2:], F32)
    stage_a(0, 0)

    def body(i, carry):
        c0 = 2 * i
        stage_b(0)
        stage_a(c0 + 1, 1)
        stage_c(jnp.maximum(c0 - 1, 0), 1)
        stage_b(1)
        stage_a(jnp.minimum(c0 + 2, nchunks - 1), 0)
        stage_c(c0, 0)
        return carry

    lax.fori_loop(0, nchunks // 2, body, 0)
    stage_c(nchunks - 1, 1)
    outs = [acc_buf[hh, :C_VDIM] / acc_buf[hh, C_VDIM:C_VDIM + 1] for hh in heads]
    o_ref[0] = jnp.concatenate(outs, axis=0).T.astype(BF16)


def _attn_mla_bounded_kernel(q_ref, k_ref, v_ref, o_ref, p_buf, acc_buf, *, nchunks, tk):
    heads = range(2)

    def stage_p(c, slot):
        start = pl.multiple_of(c * tk, tk)
        for hh in heads:
            kc = k_ref[0, pl.ds(start, tk), LANES * hh:LANES * (hh + 1)]
            s = jnp.dot(kc, q_ref[0, hh], preferred_element_type=F32)
            p_buf[hh, slot] = jnp.exp2(s).astype(BF16)

    def stage_c(c, slot):
        for hh in heads:
            acc_buf[hh] += jnp.dot(v_ref[0, hh, c], p_buf[hh, slot], preferred_element_type=F32)

    acc_buf[...] = jnp.zeros(acc_buf.shape, F32)
    stage_p(0, 0)

    def body(i, carry):
        c0 = MLA_CHUNKS_PER_TRIP * i
        for j in range(MLA_CHUNKS_PER_TRIP):
            nxt = c0 + j + 1
            if j == MLA_CHUNKS_PER_TRIP - 1:
                nxt = jnp.minimum(nxt, nchunks - 1)
            stage_p(nxt, (j + 1) % 2)
            stage_c(c0 + j, j % 2)
        return carry

    lax.fori_loop(0, nchunks // MLA_CHUNKS_PER_TRIP, body, 0)
    outs = [acc_buf[hh, :C_VDIM] / acc_buf[hh, C_VDIM:C_VDIM + 1] for hh in heads]
    o_ref[0] = jnp.concatenate(outs, axis=0).T.astype(BF16)


def _attn_mla(qt, k, vt, bounded):
    bsz, _, _, seq = qt.shape
    nchunks, tk = vt.shape[2], vt.shape[4]
    assert nchunks % MLA_CHUNKS_PER_TRIP == 0 and MLA_CHUNKS_PER_TRIP % 2 == 0
    tq = TQ_MLA_BOUNDED if bounded else TQ_MLA
    if bounded:
        body = _attn_mla_bounded_kernel
        scratch = [pltpu.VMEM((2, 2, tk, tq), BF16),
                   pltpu.VMEM((2, V_ROWS, tq), F32)]
    else:
        body = _attn_mla_kernel
        scratch = [pltpu.VMEM((2, 2, tk, tq), F32),
                   pltpu.VMEM((2, 2, 1, tq), F32),
                   pltpu.VMEM((2, 2, tk, tq), BF16),
                   pltpu.VMEM((2, 2, 1, tq), F32),
                   pltpu.VMEM((2, 1, tq), F32),
                   pltpu.VMEM((2, V_ROWS, tq), F32)]
    return pl.pallas_call(
        functools.partial(body, nchunks=nchunks, tk=tk),
        grid=(bsz, C_HEADS // 2, seq // tq),
        in_specs=[pl.BlockSpec((1, 2, LANES, tq), lambda b, h, i: (b, h, 0, i)),
                  pl.BlockSpec((1, seq, 2 * LANES), lambda b, h, i: (b, 0, h)),
                  pl.BlockSpec((1, 2, nchunks, V_ROWS, tk), lambda b, h, i: (b, h, 0, 0, 0))],
        out_specs=pl.BlockSpec((1, tq, 2 * C_VDIM), lambda b, h, i: (b, i, h)),
        out_shape=jax.ShapeDtypeStruct((bsz, seq, C_HEADS * C_VDIM), BF16),
        scratch_shapes=scratch,
        compiler_params=_cparams("parallel", "parallel", "arbitrary"),
        name="attn_mla_bounded" if bounded else "attn_mla",
    )(qt, k, vt)


def _rope_angles(seq, half):
    inv_freq = ROPE_THETA ** (-jnp.arange(half, dtype=F32) / half)
    return jnp.arange(seq, dtype=F32)[:, None] * inv_freq[None, :]


def _prep_ab(ab_w_in, ab_w_out, a_q_norm, a_k_norm, a_sink, b_q_norm, b_k_norm, b_rpb, seq):
    o_ka, o_va, o_qb = A_QW, A_QW + A_KVW, A_QW + 2 * A_KVW
    o_kb, o_vb = o_qb + B_W, o_qb + 2 * B_W
    w_in = jnp.concatenate(
        [ab_w_in[:, HEAD_DIM * h:HEAD_DIM * (h + 1)] for h in PERM_A]
        + [ab_w_in[:, o_qb:o_vb], ab_w_in[:, o_ka:o_qb], ab_w_in[:, o_vb:]], axis=1).astype(BF16)
    w_out = jnp.concatenate([ab_w_out[HEAD_DIM * h:HEAD_DIM * (h + 1)] for h in PERM_A]
                            + [ab_w_out[A_QW:]]).astype(BF16)
    scale = HEAD_DIM ** -0.5 * LOG2E
    gain = jnp.concatenate([jnp.tile(a_q_norm * scale, A_HEADS), jnp.tile(b_q_norm * scale, B_HEADS),
                            jnp.tile(b_k_norm, B_HEADS), jnp.tile(a_k_norm, A_KV_HEADS),
                            jnp.ones((A_KVW,), F32)])[None, :].astype(F32)
    idx = np.arange(2 * LANES) // HEAD_DIM
    gsum = jnp.asarray(idx[:, None] == idx[None, :], BF16)
    ang = _rope_angles(seq, HEAD_DIM // 2)
    cos = jnp.tile(jnp.cos(ang), (1, LANES // (HEAD_DIM // 2)))
    sin = jnp.tile(jnp.concatenate([-jnp.sin(ang), jnp.sin(ang)], axis=1), (1, LANES // HEAD_DIM))
    sink_col = jnp.repeat(a_sink[np.array(PERM_A)] * LOG2E, BLOCK).reshape(
        A_QW // LANES, 2 * BLOCK, 1).astype(F32)
    i = np.arange(2 * BLOCK)[:, None] % BLOCK
    j = np.arange(3 * BLOCK)[None, :]
    band = (j - i >= BLOCK - WINDOW) & (j - i <= BLOCK + WINDOW)
    band = np.stack([band & (j >= BLOCK), band, band & (j < 2 * BLOCK)])
    band_tab = jnp.asarray(np.where(band, 0.0, NEG_INF), F32)

    c = np.arange(GRID_W)
    cs = np.clip(c - NA_W // 2, 0, GRID_W - NA_W)
    inwin = (c[None, :] >= cs[:, None]) & (c[None, :] < cs[:, None] + NA_W)
    dc = c[None, :] - c[:, None] + NA_W - 1
    pick = jnp.asarray(dc[:, :, None] == np.arange(2 * NA_W - 1), F32)
    tcol = jnp.sum(b_rpb.astype(F32)[:, :, None, None, :] * pick[None, None], axis=-1)
    tab = jnp.stack([tcol[:, NA_MAX_H - 1 - d:2 * NA_MAX_H - 1 - d] for d in range(NA_MAX_H)],
                    axis=1)
    tab = jnp.where(inwin[None, None, None], tab * LOG2E, NEG_INF)
    tab = tab.transpose(1, 0, 3, 2, 4).reshape(NA_MAX_H, B_HEADS // 2, 2 * GRID_W, NA_MAX_H * GRID_W)
    return w_in, w_out, gain, gsum, cos, sin, band_tab, sink_col, tab


def _bounded_tables(a_q_norm, a_k_norm, a_sink, b_q_norm, b_k_norm, b_rpb, band_tab, sink_col, bias_tab):
    scale = HEAD_DIM ** -0.5 * LOG2E
    bound_a = BF16_NORM_MARGIN * HEAD_DIM * jnp.max(jnp.abs(a_q_norm * scale)) * jnp.max(jnp.abs(a_k_norm))
    shift_a = jnp.maximum(bound_a, jnp.max(a_sink) * LOG2E)
    ok_a = bound_a + shift_a <= 2 * SHIFT_MAX
    bound_b = BF16_NORM_MARGIN * HEAD_DIM * jnp.max(jnp.abs(b_q_norm * scale)) * jnp.max(jnp.abs(b_k_norm))
    rpb = b_rpb.astype(F32) * LOG2E
    shift_b = bound_b + jnp.max(rpb)
    ok_b = bound_b + shift_b - jnp.min(rpb) <= 2 * SHIFT_MAX
    sink_term = jnp.broadcast_to(jnp.exp2(sink_col - shift_a), sink_col.shape[:2] + (LANES,))
    return ok_a & ok_b, band_tab - shift_a, sink_term, bias_tab - shift_b


def _prep_mla(c_w_in, c_q_lora_norm, c_kv_lora_norm, c_w_q_up, c_w_kv_up, c_q_norm, c_k_norm,
              c_w_out, seq):
    win = jnp.zeros((D_MODEL, Q_LORA + KV_LORA + LANES), F32)
    win = win.at[:, :Q_LORA + KV_LORA].set(c_w_in[:, :Q_LORA + KV_LORA])
    win = win.at[:, Q_LORA + KV_LORA + C_NOPE:Q_LORA + KV_LORA + C_QK].set(c_w_in[:, Q_LORA + KV_LORA:])
    wq = c_w_q_up.reshape(Q_LORA, C_HEADS, C_QK)
    wq = jnp.pad(wq, ((0, 0), (0, 0), (0, LANES - C_QK)))
    wqt = wq.reshape(Q_LORA, C_HEADS * LANES).T.astype(BF16)
    wkv = c_w_kv_up.reshape(KV_LORA, C_HEADS, C_NOPE + C_VDIM)
    wvt = wkv[:, :, C_NOPE:].reshape(KV_LORA, C_HEADS * C_VDIM).T.astype(BF16)
    wk = jnp.pad(wkv[:, :, :C_NOPE], ((0, 0), (0, 0), (0, LANES - C_NOPE)))
    wk = wk.reshape(KV_LORA, C_HEADS * LANES).astype(BF16)
    qscale = C_QK ** -0.5 * LOG2E
    shift = (BF16_NORM_MARGIN * C_QK * jnp.max(jnp.abs(c_q_norm * qscale)) * jnp.max(jnp.abs(c_k_norm)))
    shift = shift.reshape(1).astype(F32)
    gqn = jnp.pad(c_q_norm * qscale, (0, LANES - C_QK))[:, None].astype(F32)
    gkn = jnp.pad(c_k_norm, (0, LANES - C_QK))[None, :].astype(F32)
    half = C_ROPE // 2
    ang = _rope_angles(seq, half)
    cos, sin = jnp.cos(ang), jnp.sin(ang)
    zl = jnp.zeros((seq, C_NOPE), F32)
    zr = jnp.zeros((seq, LANES - C_QK), F32)
    cosk = jnp.concatenate([zl, cos, cos, zr], axis=1)
    sink = jnp.concatenate([zl, -sin, sin, zr], axis=1)
    return (shift, win.astype(BF16), c_q_lora_norm[None, :].astype(F32),
            c_kv_lora_norm[None, :].astype(F32), wqt, wvt, wk, gqn, gkn, cosk, sink, cos.T, sin.T,
            c_w_out.astype(BF16))


def _trunk(x, p):
    seq = x.shape[1]
    w_in, w_out, gain, gsum, cos, sin, band_tab, sink_col, bias_tab = _prep_ab(
        p["ab_w_in"][0], p["ab_w_out"][0], p["a_q_norm"][0], p["a_k_norm"][0], p["a_sink"][0],
        p["b_q_norm"][0], p["b_k_norm"][0], p["b_rpb"][0], seq)
    qa, ka, va, qb, kb, vb = _proj_ab(x, p["norm_mix"][0][None, :], w_in, gsum, gain, cos, sin)
    ok, band_sh, sink_term, bias_sh = _bounded_tables(
        p["a_q_norm"][0], p["a_k_norm"][0], p["a_sink"][0], p["b_q_norm"][0], p["b_k_norm"][0],
        p["b_rpb"][0], band_tab, sink_col, bias_tab)
    o = lax.cond(
        ok,
        lambda: _attn_ab(qa, ka, va, band_sh, sink_term, qb, kb, vb, bias_sh, bounded=True),
        lambda: _attn_ab(qa, ka, va, band_tab, sink_col, qb, kb, vb, bias_tab, bounded=False))
    x = _out_ffn(x, o, w_out, p["norm_ffn"][0][None, :],
                 p["ffn_w_gate"][0].astype(BF16), p["ffn_w_up"][0].astype(BF16),
                 p["ffn_w_down"][0].astype(BF16))
    (shift, win, gq, gkv, wqt, wvt, wk, gqn, gkn, cosk, sink, cosq, sinq, wo) = _prep_mla(
        p["c_w_in"][0], p["c_q_lora_norm"][0], p["c_kv_lora_norm"][0], p["c_w_q_up"][0],
        p["c_w_kv_up"][0], p["c_q_norm"][0], p["c_k_norm"][0], p["c_w_out"][0], seq)
    qt, k, vt = _proj_mla(shift, x, p["norm_mix"][1][None, :], win, gq, gkv, wqt, wvt, wk, gqn, gkn,
                          cosk, sink, cosq, sinq)
    o = lax.cond(shift[0] <= SHIFT_MAX,
                 functools.partial(_attn_mla, bounded=True),
                 functools.partial(_attn_mla, bounded=False), qt, k, vt)
    x = _out_ffn(x, o, wo, p["norm_ffn"][1][None, :], p["ffn_w_gate"][1].astype(BF16),
                 p["ffn_w_up"][1].astype(BF16), p["ffn_w_down"][1].astype(BF16))
    return x


def kernel(x_prompt, x_sample, norm_mix, norm_ffn, ab_w_in, ab_w_out, a_q_norm, a_k_norm, a_sink,
           b_q_norm, b_k_norm, b_rpb, c_w_in, c_q_lora_norm, c_kv_lora_norm, c_w_q_up, c_w_kv_up,
           c_q_norm, c_k_norm, c_w_out, ffn_w_gate, ffn_w_up, ffn_w_down):
    p = dict(norm_mix=norm_mix, norm_ffn=norm_ffn, ab_w_in=ab_w_in, ab_w_out=ab_w_out,
             a_q_norm=a_q_norm, a_k_norm=a_k_norm, a_sink=a_sink, b_q_norm=b_q_norm,
             b_k_norm=b_k_norm, b_rpb=b_rpb, c_w_in=c_w_in, c_q_lora_norm=c_q_lora_norm,
             c_kv_lora_norm=c_kv_lora_norm, c_w_q_up=c_w_q_up, c_w_kv_up=c_w_kv_up,
             c_q_norm=c_q_norm, c_k_norm=c_k_norm, c_w_out=c_w_out, ffn_w_gate=ffn_w_gate,
             ffn_w_up=ffn_w_up, ffn_w_down=ffn_w_down)
    return _trunk(x_prompt, p), _trunk(x_sample, p)
```

```python
import functools
import math

import numpy as np
import jax
import jax.numpy as jnp
from jax import lax
from jax.experimental import pallas as pl
from jax.experimental.pallas import tpu as pltpu

F32 = jnp.float32
BF16 = jnp.bfloat16

D_MODEL = 1024
GRID_W = 64
HEAD_DIM = 64
ROPE_THETA = 10000.0
EPS = 1e-6
NEG_INF = -1e30
BLOCK = 128
A_HEADS = 8
A_KV_HEADS = 2
WINDOW = 128
B_HEADS = 8
NA_MAX_H = 8
NA_W = 16
C_HEADS = 16
C_NOPE = 64
C_ROPE = 32
C_VDIM = 64
C_QK = C_NOPE + C_ROPE
Q_LORA = 384
KV_LORA = 256
A_QW = A_HEADS * HEAD_DIM
A_KVW = A_KV_HEADS * HEAD_DIM
B_W = B_HEADS * HEAD_DIM

LANES = 128
V7X_VMEM_BYTES = 64 * 1024 * 1024
VMEM_LIMIT = V7X_VMEM_BYTES * 3 // 4

TM_PROJ = 512
TM_FFN = 512
TQ_MLA = 1024
TQ_MLA_BOUNDED = 2048
MLA_CHUNKS_PER_TRIP = 8
NA_ROWS = 8
A_QBLOCKS = 4
V_ROWS = C_VDIM + 16

PERM_A = (0, 4, 1, 5, 2, 6, 3, 7)

LOG2E = math.log2(math.e)
SHIFT_MAX = 60.0
BF16_NORM_MARGIN = 1.0 + 2.0 ** -6
NT_DIMS = (((1,), (1,)), ((), ()))


def _cparams(*sem):
    return pltpu.CompilerParams(dimension_semantics=sem, vmem_limit_bytes=VMEM_LIMIT)


def _resident(shape):
    nd = len(shape)
    return pl.BlockSpec(shape, lambda *_: (0,) * nd, pipeline_mode=pl.Buffered(1))


def _rms_rows(x, gain):
    ms = jnp.mean(x * x, axis=-1, keepdims=True)
    return x * lax.rsqrt(ms + EPS) * gain


def _proj_ab_kernel(x_ref, g_ref, w_ref, gsum_ref, gain_ref, cos_ref, sin_ref,
                    qa_ref, ka_ref, va_ref, qb_ref, kb_ref, vb_ref):
    h = _rms_rows(x_ref[0], g_ref[...]).astype(BF16)
    y = jnp.dot(h, w_ref[...], preferred_element_type=F32)
    gsum = gsum_ref[...]
    cos = cos_ref[...]
    sin = sin_ref[...]
    lane = lax.broadcasted_iota(jnp.int32, cos.shape, 1)
    first_half = (lane % HEAD_DIM) < (HEAD_DIM // 2)

    def head_norm(c):
        yc = y[:, 2 * LANES * c:2 * LANES * (c + 1)]
        ss = jnp.dot((yc * yc).astype(BF16), gsum, preferred_element_type=F32)
        return yc * lax.rsqrt(ss * (1.0 / HEAD_DIM) + EPS) * gain_ref[:, 2 * LANES * c:2 * LANES * (c + 1)]

    def rope(v):
        swapped = jnp.where(first_half, pltpu.roll(v, LANES - HEAD_DIM // 2, 1),
                            pltpu.roll(v, HEAD_DIM // 2, 1))
        return v * cos + swapped * sin

    for c in range(2):
        yn = head_norm(c)
        for b in range(2):
            qa_ref[0, :, LANES * (2 * c + b):LANES * (2 * c + b + 1)] = rope(
                yn[:, LANES * b:LANES * (b + 1)]).astype(BF16)
    for c in range(2):
        qb_ref[0, :, 2 * LANES * c:2 * LANES * (c + 1)] = head_norm(2 + c).astype(BF16)
    for c in range(2):
        kb_ref[0, :, 2 * LANES * c:2 * LANES * (c + 1)] = head_norm(4 + c).astype(BF16)
    ka_ref[0] = rope(head_norm(6)[:, :LANES]).astype(BF16)
    va_ref[0] = y[:, 13 * LANES:14 * LANES].astype(BF16)
    vb_ref[0] = y[:, 14 * LANES:18 * LANES].astype(BF16)


def _proj_ab(x, g, w, gsum, gain, cos, sin):
    bsz, seq, _ = x.shape
    tm = TM_PROJ
    nt = seq // tm
    tok = lambda width: pl.BlockSpec((1, tm, width), lambda b, i: (b, i, 0))
    tab = pl.BlockSpec((tm, LANES), lambda b, i: (i, 0))
    out = lambda width: jax.ShapeDtypeStruct((bsz, seq, width), BF16)
    return pl.pallas_call(
        _proj_ab_kernel,
        grid=(bsz, nt),
        in_specs=[tok(D_MODEL), _resident(g.shape), _resident(w.shape), _resident(gsum.shape),
                  _resident(gain.shape), tab, tab],
        out_specs=[tok(A_QW), tok(A_KVW), tok(A_KVW), tok(B_W), tok(B_W), tok(B_W)],
        out_shape=[out(A_QW), out(A_KVW), out(A_KVW), out(B_W), out(B_W), out(B_W)],
        compiler_params=_cparams("parallel", "parallel"),
        name="proj_ab",
    )(x, g, w, gsum, gain, cos, sin)


def _attn_a_body(q_ref, kp_ref, kc_ref, kn_ref, vp_ref, vc_ref, vn_ref, bias_ref, sink_ref, o_ref,
                 *, nsteps, lane0, bounded):
    n = pl.program_id(1)
    k = jnp.concatenate([kp_ref[0], kc_ref[0], kn_ref[0]], axis=0)
    v = jnp.concatenate([vp_ref[0], vc_ref[0], vn_ref[0]], axis=0)
    vext = jnp.concatenate([v, jnp.ones(v.shape, BF16)], axis=1)
    lane = lax.broadcasted_iota(jnp.int32, (BLOCK, LANES), 1)
    lo = lane < HEAD_DIM
    zero = jnp.zeros((BLOCK, LANES), BF16)
    for t in range(A_QBLOCKS):
        if t == 0:
            bias = bias_ref[jnp.where(n == 0, 0, 1)]
        elif t == A_QBLOCKS - 1:
            bias = bias_ref[jnp.where(n == nsteps - 1, 2, 1)]
        else:
            bias = bias_ref[1]
        kw = k[BLOCK * t:BLOCK * (t + 3)]
        vw = vext[BLOCK * t:BLOCK * (t + 3)]
        for blk in range(A_QW // LANES):
            qp = q_ref[0, BLOCK * t:BLOCK * (t + 1), LANES * blk:LANES * (blk + 1)]
            qm = jnp.concatenate([jnp.where(lo, qp, zero), jnp.where(lo, zero, qp)], axis=0)
            s = lax.dot_general(qm, kw, NT_DIMS, preferred_element_type=F32) + bias
            if bounded:
                p = jnp.exp2(s)
                sink_term = sink_ref[blk]
            else:
                sink = sink_ref[blk]
                m = jnp.maximum(jnp.max(s, axis=-1, keepdims=True), sink)
                p = jnp.exp2(s - m)
                sink_term = jnp.exp2(sink - m)
            oe = jnp.dot(p.astype(BF16), vw, preferred_element_type=F32)
            o = oe[:, :LANES] / (oe[:, LANES:] + sink_term)
            o_ref[0, BLOCK * t:BLOCK * (t + 1), lane0 + LANES * blk:lane0 + LANES * (blk + 1)] = (
                jnp.where(lo, o[:BLOCK], o[BLOCK:]).astype(BF16))


def _attn_b_body(q_ref, kp_ref, kc_ref, kn_ref, vp_ref, vc_ref, vn_ref, bias_ref, o_ref,
                 kbuf, vbuf, *, nrb, lane0, bounded):
    rb = pl.program_id(1)
    blk_tok = NA_ROWS * GRID_W
    half_tok = blk_tok // 2
    win_tok = NA_MAX_H * GRID_W
    kbuf[0:half_tok] = kp_ref[0, half_tok:blk_tok]
    kbuf[half_tok:half_tok + blk_tok] = kc_ref[0]
    kbuf[half_tok + blk_tok:2 * blk_tok] = kn_ref[0, 0:half_tok]
    vbuf[0:half_tok] = vp_ref[0, half_tok:blk_tok]
    vbuf[half_tok:half_tok + blk_tok] = vc_ref[0]
    vbuf[half_tok + blk_tok:2 * blk_tok] = vn_ref[0, 0:half_tok]
    lane = lax.broadcasted_iota(jnp.int32, (GRID_W, LANES), 1)
    lo = lane < HEAD_DIM
    zero = jnp.zeros((GRID_W, LANES), BF16)
    ones = jnp.ones((win_tok, LANES), BF16)
    mid = NA_MAX_H // 2
    for t in range(NA_ROWS):
        off = jnp.where(rb == 0, max(t, mid), jnp.where(rb == nrb - 1, min(t, mid), t))
        didx = t + mid - off
        kstart = pl.multiple_of(off * GRID_W, GRID_W)
        for blk in range(B_W // LANES):
            qp = q_ref[0, GRID_W * t:GRID_W * (t + 1), LANES * blk:LANES * (blk + 1)]
            qm = jnp.concatenate([jnp.where(lo, qp, zero), jnp.where(lo, zero, qp)], axis=0)
            kw = kbuf[pl.ds(kstart, win_tok), LANES * blk:LANES * (blk + 1)]
            vw = vbuf[pl.ds(kstart, win_tok), LANES * blk:LANES * (blk + 1)]
            s = lax.dot_general(qm, kw, NT_DIMS, preferred_element_type=F32)
            s = s + bias_ref[didx, blk]
            p = jnp.exp2(s) if bounded else jnp.exp2(s - jnp.max(s, axis=-1, keepdims=True))
            oe = jnp.dot(p.astype(BF16), jnp.concatenate([vw, ones], axis=1),
                         preferred_element_type=F32)
            o = oe[:, :LANES] / oe[:, LANES:]
            o_ref[0, GRID_W * t:GRID_W * (t + 1), lane0 + LANES * blk:lane0 + LANES * (blk + 1)] = (
                jnp.where(lo, o[:GRID_W], o[GRID_W:]).astype(BF16))


def _attn_ab_kernel(*refs, nsteps, bounded):
    a_refs, b_refs, o_ref, scratch = refs[:9], refs[9:17], refs[17], refs[18:]
    _attn_a_body(*a_refs, o_ref, nsteps=nsteps, lane0=0, bounded=bounded)
    _attn_b_body(*b_refs, o_ref, *scratch, nrb=nsteps, lane0=A_QW, bounded=bounded)


def _attn_ab(qa, ka, va, band_tab, sink_col, qb, kb, vb, bias_tab, bounded):
    bsz, seq, _ = qa.shape
    tok = A_QBLOCKS * BLOCK
    assert tok == NA_ROWS * GRID_W
    nsteps = seq // tok
    assert nsteps >= 2, "neighbourhood attention needs at least two row blocks"
    nb = seq // BLOCK
    cur = lambda b, n: (b, n, 0)
    a_edge = lambda f: pl.BlockSpec((1, BLOCK, A_KVW), f)
    a_prev = lambda b, n: (b, jnp.maximum(A_QBLOCKS * n - 1, 0), 0)
    a_next = lambda b, n: (b, jnp.minimum(A_QBLOCKS * (n + 1), nb - 1), 0)
    a_mid = pl.BlockSpec((1, tok, A_KVW), cur)
    b_blk = lambda f: pl.BlockSpec((1, tok, B_W), f)
    b_prev = lambda b, n: (b, jnp.maximum(n - 1, 0), 0)
    b_next = lambda b, n: (b, jnp.minimum(n + 1, nsteps - 1), 0)
    return pl.pallas_call(
        functools.partial(_attn_ab_kernel, nsteps=nsteps, bounded=bounded),
        grid=(bsz, nsteps),
        in_specs=[pl.BlockSpec((1, tok, A_QW), cur), a_edge(a_prev), a_mid, a_edge(a_next),
                  a_edge(a_prev), a_mid, a_edge(a_next), _resident(band_tab.shape),
                  _resident(sink_col.shape),
                  b_blk(cur), b_blk(b_prev), b_blk(cur), b_blk(b_next),
                  b_blk(b_prev), b_blk(cur), b_blk(b_next), _resident(bias_tab.shape)],
        out_specs=pl.BlockSpec((1, tok, A_QW + B_W), cur),
        out_shape=jax.ShapeDtypeStruct((bsz, seq, A_QW + B_W), BF16),
        scratch_shapes=[pltpu.VMEM((2 * tok, B_W), BF16), pltpu.VMEM((2 * tok, B_W), BF16)],
        compiler_params=_cparams("parallel", "parallel"),
        name="attn_ab_bounded" if bounded else "attn_ab",
    )(qa, ka, ka, ka, va, va, va, band_tab, sink_col, qb, kb, kb, kb, vb, vb, vb, bias_tab)


def _out_ffn_kernel(x_ref, o_ref, wo_ref, g_ref, wg_ref, wu_ref, wd_ref, y_ref):
    x1 = x_ref[0] + jnp.dot(o_ref[0], wo_ref[...], preferred_element_type=F32)
    h = _rms_rows(x1, g_ref[...]).astype(BF16)
    gate = jnp.dot(h, wg_ref[...], preferred_element_type=F32)
    up = jnp.dot(h, wu_ref[...], preferred_element_type=F32)
    act = (gate / (1.0 + jnp.exp(-gate)) * up).astype(BF16)
    y_ref[0] = x1 + jnp.dot(act, wd_ref[...], preferred_element_type=F32)


def _out_ffn(x, o, wo, g, wg, wu, wd):
    bsz, seq, _ = x.shape
    tm = TM_FFN
    tok = lambda width: pl.BlockSpec((1, tm, width), lambda b, i: (b, i, 0))
    consts = (wo, g, wg, wu, wd)
    return pl.pallas_call(
        _out_ffn_kernel,
        grid=(bsz, seq // tm),
        in_specs=[tok(D_MODEL), tok(o.shape[-1]), *[_resident(c.shape) for c in consts]],
        out_specs=tok(D_MODEL),
        out_shape=jax.ShapeDtypeStruct(x.shape, F32),
        compiler_params=_cparams("parallel", "parallel"),
        name="out_ffn",
    )(x, o, *consts)


def _proj_mla_kernel(shift_ref, x_ref, g_ref, win_ref, gq_ref, gkv_ref, wqt_ref, wvt_ref, wk_ref,
                     gqn_ref, gkn_ref, cosk_ref, sink_ref, cosq_ref, sinq_ref,
                     q_ref, k_ref, v_ref):
    tm = x_ref.shape[1]
    shift = shift_ref[0]
    h = _rms_rows(x_ref[0], g_ref[...]).astype(BF16)
    y = jnp.dot(h, win_ref[...], preferred_element_type=F32)
    cq = _rms_rows(y[:, :Q_LORA], gq_ref[...]).astype(BF16)
    ckv = _rms_rows(y[:, Q_LORA:Q_LORA + KV_LORA], gkv_ref[...]).astype(BF16)
    kpe = y[:, Q_LORA + KV_LORA:]

    qt = lax.dot_general(wqt_ref[...], cq, NT_DIMS, preferred_element_type=F32)
    qt = qt.reshape(C_HEADS, LANES, tm)
    ssq = jnp.sum(qt * qt, axis=1, keepdims=True)
    qn = qt * lax.rsqrt(ssq * (1.0 / C_QK) + EPS) * gqn_ref[...][None]
    half = C_ROPE // 2
    r1 = qn[:, C_NOPE:C_NOPE + half]
    r2 = qn[:, C_NOPE + half:C_QK]
    cos = cosq_ref[...][None]
    sin = sinq_ref[...][None]
    pad_row = lax.broadcasted_iota(jnp.int32, (C_HEADS, LANES - C_QK, tm), 1)
    q_out = jnp.concatenate([qn[:, :C_NOPE], r1 * cos - r2 * sin, r2 * cos + r1 * sin,
                             jnp.where(pad_row == 0, -shift, 0.0)], axis=1)
    q_ref[0] = q_out.astype(BF16)

    vt = lax.dot_general(wvt_ref[...], ckv, NT_DIMS, preferred_element_type=F32)
    vt = vt.reshape(C_HEADS, C_VDIM, tm).astype(BF16)
    row = lax.broadcasted_iota(jnp.int32, (C_HEADS, V_ROWS - C_VDIM, tm), 1)
    v_ref[0, :, 0] = jnp.concatenate([vt, jnp.where(row == 0, 1.0, 0.0).astype(BF16)], axis=1)

    kn = jnp.dot(ckv, wk_ref[...], preferred_element_type=F32)
    gk = gkn_ref[...]
    ss_pe = jnp.sum(kpe * kpe, axis=-1, keepdims=True)
    kg = kpe * gk
    lane = lax.broadcasted_iota(jnp.int32, kg.shape, 1)
    swapped = jnp.where(lane < C_NOPE + half, pltpu.roll(kg, LANES - half, 1),
                        pltpu.roll(kg, half, 1))
    kr = kg * cosk_ref[...] + swapped * sink_ref[...]
    for hd in range(C_HEADS):
        kh = kn[:, LANES * hd:LANES * (hd + 1)]
        ss = jnp.sum(kh * kh, axis=-1, keepdims=True) + ss_pe
        kval = (kh * gk + kr) * lax.rsqrt(ss * (1.0 / C_QK) + EPS)
        k_ref[0, :, LANES * hd:LANES * (hd + 1)] = jnp.where(lane == C_QK, 1.0, kval).astype(BF16)


def _proj_mla(shift, x, g, win, gq, gkv, wqt, wvt, wk, gqn, gkn, cosk, sink, cosq, sinq):
    bsz, seq, _ = x.shape
    tm = TM_PROJ
    nt = seq // tm
    half = C_ROPE // 2
    return pl.pallas_call(
        _proj_mla_kernel,
        grid=(bsz, nt),
        in_specs=[pl.BlockSpec(memory_space=pltpu.SMEM),
                  pl.BlockSpec((1, tm, D_MODEL), lambda b, i: (b, i, 0)),
                  _resident(g.shape), _resident(win.shape), _resident(gq.shape),
                  _resident(gkv.shape), _resident(wqt.shape), _resident(wvt.shape),
                  _resident(wk.shape), _resident(gqn.shape), _resident(gkn.shape),
                  pl.BlockSpec((tm, LANES), lambda b, i: (i, 0)),
                  pl.BlockSpec((tm, LANES), lambda b, i: (i, 0)),
                  pl.BlockSpec((half, tm), lambda b, i: (0, i)),
                  pl.BlockSpec((half, tm), lambda b, i: (0, i))],
        out_specs=[pl.BlockSpec((1, C_HEADS, LANES, tm), lambda b, i: (b, 0, 0, i)),
                   pl.BlockSpec((1, tm, C_HEADS * LANES), lambda b, i: (b, i, 0)),
                   pl.BlockSpec((1, C_HEADS, 1, V_ROWS, tm), lambda b, i: (b, 0, i, 0, 0))],
        out_shape=[jax.ShapeDtypeStruct((bsz, C_HEADS, LANES, seq), BF16),
                   jax.ShapeDtypeStruct((bsz, seq, C_HEADS * LANES), BF16),
                   jax.ShapeDtypeStruct((bsz, C_HEADS, nt, V_ROWS, tm), BF16)],
        compiler_params=_cparams("parallel", "parallel"),
        name="proj_mla",
    )(shift, x, g, win, gq, gkv, wqt, wvt, wk, gqn, gkn, cosk, sink, cosq, sinq)


def _attn_mla_kernel(q_ref, k_ref, v_ref, o_ref, s_buf, cmax_buf, p_buf, alpha_buf, m_buf, acc_buf,
                     *, nchunks, tk):
    heads = range(2)

    def stage_a(c, slot):
        start = pl.multiple_of(c * tk, tk)
        for hh in heads:
            kc = k_ref[0, pl.ds(start, tk), LANES * hh:LANES * (hh + 1)]
            s = jnp.dot(kc, q_ref[0, hh], preferred_element_type=F32)
            s_buf[hh, slot] = s
            cmax_buf[hh, slot] = jnp.max(s, axis=0, keepdims=True)

    def stage_b(slot):
        for hh in heads:
            m_old = m_buf[hh]
            m_new = jnp.maximum(m_old, cmax_buf[hh, slot])
            alpha_buf[hh, slot] = jnp.exp2(m_old - m_new)
            m_buf[hh] = m_new
            p_buf[hh, slot] = jnp.exp2(s_buf[hh, slot] - m_new).astype(BF16)

    def stage_c(c, slot):
        for hh in heads:
            pv = jnp.dot(v_ref[0, hh, c], p_buf[hh, slot], preferred_element_type=F32)
            acc_buf[hh] = alpha_buf[hh, slot] * acc_buf[hh] + pv

    m_buf[...] = jnp.full(m_buf.shape, NEG_INF, F32)
    acc_buf[...] = jnp.zeros(acc_buf.shape, F32)
    p_buf[:, 1] = jnp.zeros(p_buf.shape[:1] + p_buf.shape[2:], BF16)
    alpha_buf[:, 1] = jnp.ones(alpha_buf.shape[:1] + alpha_buf.shape[2:], F32)
    stage_a(0, 0)

    def body(i, carry):
        c0 = 2 * i
        stage_b(0)
        stage_a(c0 + 1, 1)
        stage_c(jnp.maximum(c0 - 1, 0), 1)
        stage_b(1)
        stage_a(jnp.minimum(c0 + 2, nchunks - 1), 0)
        stage_c(c0, 0)
        return carry

    lax.fori_loop(0, nchunks // 2, body, 0)
    stage_c(nchunks - 1, 1)
    outs = [acc_buf[hh, :C_VDIM] / acc_buf[hh, C_VDIM:C_VDIM + 1] for hh in heads]
    o_ref[0] = jnp.concatenate(outs, axis=0).T.astype(BF16)


def _attn_mla_bounded_kernel(q_ref, k_ref, v_ref, o_ref, p_buf, acc_buf, *, nchunks, tk):
    heads = range(2)

    def stage_p(c, slot):
        start = pl.multiple_of(c * tk, tk)
        for hh in heads:
            kc = k_ref[0, pl.ds(start, tk), LANES * hh:LANES * (hh + 1)]
            s = jnp.dot(kc, q_ref[0, hh], preferred_element_type=F32)
            p_buf[hh, slot] = jnp.exp2(s).astype(BF16)

    def stage_c(c, slot):
        for hh in heads:
            acc_buf[hh] += jnp.dot(v_ref[0, hh, c], p_buf[hh, slot], preferred_element_type=F32)

    acc_buf[...] = jnp.zeros(acc_buf.shape, F32)
    stage_p(0, 0)

    def body(i, carry):
        c0 = MLA_CHUNKS_PER_TRIP * i
        for j in range(MLA_CHUNKS_PER_TRIP):
            nxt = c0 + j + 1
            if j == MLA_CHUNKS_PER_TRIP - 1:
                nxt = jnp.minimum(nxt, nchunks - 1)
            stage_p(nxt, (j + 1) % 2)
            stage_c(c0 + j, j % 2)
        return carry

    lax.fori_loop(0, nchunks // MLA_CHUNKS_PER_TRIP, body, 0)
    outs = [acc_buf[hh, :C_VDIM] / acc_buf[hh, C_VDIM:C_VDIM + 1] for hh in heads]
    o_ref[0] = jnp.concatenate(outs, axis=0).T.astype(BF16)


def _attn_mla(qt, k, vt, bounded):
    bsz, _, _, seq = qt.shape
    nchunks, tk = vt.shape[2], vt.shape[4]
    assert nchunks % MLA_CHUNKS_PER_TRIP == 0 and MLA_CHUNKS_PER_TRIP % 2 == 0
    tq = TQ_MLA_BOUNDED if bounded else TQ_MLA
    if bounded:
        body = _attn_mla_bounded_kernel
        scratch = [pltpu.VMEM((2, 2, tk, tq), BF16),
                   pltpu.VMEM((2, V_ROWS, tq), F32)]
    else:
        body = _attn_mla_kernel
        scratch = [pltpu.VMEM((2, 2, tk, tq), F32),
                   pltpu.VMEM((2, 2, 1, tq), F32),
                   pltpu.VMEM((2, 2, tk, tq), BF16),
                   pltpu.VMEM((2, 2, 1, tq), F32),
                   pltpu.VMEM((2, 1, tq), F32),
                   pltpu.VMEM((2, V_ROWS, tq), F32)]
    return pl.pallas_call(
        functools.partial(body, nchunks=nchunks, tk=tk),
        grid=(bsz, C_HEADS // 2, seq // tq),
        in_specs=[pl.BlockSpec((1, 2, LANES, tq), lambda b, h, i: (b, h, 0, i)),
                  pl.BlockSpec((1, seq, 2 * LANES), lambda b, h, i: (b, 0, h)),
                  pl.BlockSpec((1, 2, nchunks, V_ROWS, tk), lambda b, h, i: (b, h, 0, 0, 0))],
        out_specs=pl.BlockSpec((1, tq, 2 * C_VDIM), lambda b, h, i: (b, i, h)),
        out_shape=jax.ShapeDtypeStruct((bsz, seq, C_HEADS * C_VDIM), BF16),
        scratch_shapes=scratch,
        compiler_params=_cparams("parallel", "parallel", "arbitrary"),
        name="attn_mla_bounded" if bounded else "attn_mla",
    )(qt, k, vt)


def _rope_angles(seq, half):
    inv_freq = ROPE_THETA ** (-jnp.arange(half, dtype=F32) / half)
    return jnp.arange(seq, dtype=F32)[:, None] * inv_freq[None, :]


def _prep_ab(ab_w_in, ab_w_out, a_q_norm, a_k_norm, a_sink, b_q_norm, b_k_norm, b_rpb, seq):
    o_ka, o_va, o_qb = A_QW, A_QW + A_KVW, A_QW + 2 * A_KVW
    o_kb, o_vb = o_qb + B_W, o_qb + 2 * B_W
    w_in = jnp.concatenate(
        [ab_w_in[:, HEAD_DIM * h:HEAD_DIM * (h + 1)] for h in PERM_A]
        + [ab_w_in[:, o_qb:o_vb], ab_w_in[:, o_ka:o_qb], ab_w_in[:, o_vb:]], axis=1).astype(BF16)
    w_out = jnp.concatenate([ab_w_out[HEAD_DIM * h:HEAD_DIM * (h + 1)] for h in PERM_A]
                            + [ab_w_out[A_QW:]]).astype(BF16)
    scale = HEAD_DIM ** -0.5 * LOG2E
    gain = jnp.concatenate([jnp.tile(a_q_norm * scale, A_HEADS), jnp.tile(b_q_norm * scale, B_HEADS),
                            jnp.tile(b_k_norm, B_HEADS), jnp.tile(a_k_norm, A_KV_HEADS),
                            jnp.ones((A_KVW,), F32)])[None, :].astype(F32)
    idx = np.arange(2 * LANES) // HEAD_DIM
    gsum = jnp.asarray(idx[:, None] == idx[None, :], BF16)
    ang = _rope_angles(seq, HEAD_DIM // 2)
    cos = jnp.tile(jnp.cos(ang), (1, LANES // (HEAD_DIM // 2)))
    sin = jnp.tile(jnp.concatenate([-jnp.sin(ang), jnp.sin(ang)], axis=1), (1, LANES // HEAD_DIM))
    sink_col = jnp.repeat(a_sink[np.array(PERM_A)] * LOG2E, BLOCK).reshape(
        A_QW // LANES, 2 * BLOCK, 1).astype(F32)
    i = np.arange(2 * BLOCK)[:, None] % BLOCK
    j = np.arange(3 * BLOCK)[None, :]
    band = (j - i >= BLOCK - WINDOW) & (j - i <= BLOCK + WINDOW)
    band = np.stack([band & (j >= BLOCK), band, band & (j < 2 * BLOCK)])
    band_tab = jnp.asarray(np.where(band, 0.0, NEG_INF), F32)

    c = np.arange(GRID_W)
    cs = np.clip(c - NA_W // 2, 0, GRID_W - NA_W)
    inwin = (c[None, :] >= cs[:, None]) & (c[None, :] < cs[:, None] + NA_W)
    dc = c[None, :] - c[:, None] + NA_W - 1
    pick = jnp.asarray(dc[:, :, None] == np.arange(2 * NA_W - 1), F32)
    tcol = jnp.sum(b_rpb.astype(F32)[:, :, None, None, :] * pick[None, None], axis=-1)
    tab = jnp.stack([tcol[:, NA_MAX_H - 1 - d:2 * NA_MAX_H - 1 - d] for d in range(NA_MAX_H)],
                    axis=1)
    tab = jnp.where(inwin[None, None, None], tab * LOG2E, NEG_INF)
    tab = tab.transpose(1, 0, 3, 2, 4).reshape(NA_MAX_H, B_HEADS // 2, 2 * GRID_W, NA_MAX_H * GRID_W)
    return w_in, w_out, gain, gsum, cos, sin, band_tab, sink_col, tab


def _bounded_tables(a_q_norm, a_k_norm, a_sink, b_q_norm, b_k_norm, b_rpb, band_tab, sink_col, bias_tab):
    scale = HEAD_DIM ** -0.5 * LOG2E
    bound_a = BF16_NORM_MARGIN * HEAD_DIM * jnp.max(jnp.abs(a_q_norm * scale)) * jnp.max(jnp.abs(a_k_norm))
    shift_a = jnp.maximum(bound_a, jnp.max(a_sink) * LOG2E)
    ok_a = bound_a + shift_a <= 2 * SHIFT_MAX
    bound_b = BF16_NORM_MARGIN * HEAD_DIM * jnp.max(jnp.abs(b_q_norm * scale)) * jnp.max(jnp.abs(b_k_norm))
    rpb = b_rpb.astype(F32) * LOG2E
    shift_b = bound_b + jnp.max(rpb)
    ok_b = bound_b + shift_b - jnp.min(rpb) <= 2 * SHIFT_MAX
    sink_term = jnp.broadcast_to(jnp.exp2(sink_col - shift_a), sink_col.shape[:2] + (LANES,))
    return ok_a & ok_b, band_tab - shift_a, sink_term, bias_tab - shift_b


def _prep_mla(c_w_in, c_q_lora_norm, c_kv_lora_norm, c_w_q_up, c_w_kv_up, c_q_norm, c_k_norm,
              c_w_out, seq):
    win = jnp.zeros((D_MODEL, Q_LORA + KV_LORA + LANES), F32)
    win = win.at[:, :Q_LORA + KV_LORA].set(c_w_in[:, :Q_LORA + KV_LORA])
    win = win.at[:, Q_LORA + KV_LORA + C_NOPE:Q_LORA + KV_LORA + C_QK].set(c_w_in[:, Q_LORA + KV_LORA:])
    wq = c_w_q_up.reshape(Q_LORA, C_HEADS, C_QK)
    wq = jnp.pad(wq, ((0, 0), (0, 0), (0, LANES - C_QK)))
    wqt = wq.reshape(Q_LORA, C_HEADS * LANES).T.astype(BF16)
    wkv = c_w_kv_up.reshape(KV_LORA, C_HEADS, C_NOPE + C_VDIM)
    wvt = wkv[:, :, C_NOPE:].reshape(KV_LORA, C_HEADS * C_VDIM).T.astype(BF16)
    wk = jnp.pad(wkv[:, :, :C_NOPE], ((0, 0), (0, 0), (0, LANES - C_NOPE)))
    wk = wk.reshape(KV_LORA, C_HEADS * LANES).astype(BF16)
    qscale = C_QK ** -0.5 * LOG2E
    shift = (BF16_NORM_MARGIN * C_QK * jnp.max(jnp.abs(c_q_norm * qscale)) * jnp.max(jnp.abs(c_k_norm)))
    shift = shift.reshape(1).astype(F32)
    gqn = jnp.pad(c_q_norm * qscale, (0, LANES - C_QK))[:, None].astype(F32)
    gkn = jnp.pad(c_k_norm, (0, LANES - C_QK))[None, :].astype(F32)
    half = C_ROPE // 2
    ang = _rope_angles(seq, half)
    cos, sin = jnp.cos(ang), jnp.sin(ang)
    zl = jnp.zeros((seq, C_NOPE), F32)
    zr = jnp.zeros((seq, LANES - C_QK), F32)
    cosk = jnp.concatenate([zl, cos, cos, zr], axis=1)
    sink = jnp.concatenate([zl, -sin, sin, zr], axis=1)
    return (shift, win.astype(BF16), c_q_lora_norm[None, :].astype(F32),
            c_kv_lora_norm[None, :].astype(F32), wqt, wvt, wk, gqn, gkn, cosk, sink, cos.T, sin.T,
            c_w_out.astype(BF16))


def _trunk(x, p):
    seq = x.shape[1]
    w_in, w_out, gain, gsum, cos, sin, band_tab, sink_col, bias_tab = _prep_ab(
        p["ab_w_in"][0], p["ab_w_out"][0], p["a_q_norm"][0], p["a_k_norm"][0], p["a_sink"][0],
        p["b_q_norm"][0], p["b_k_norm"][0], p["b_rpb"][0], seq)
    qa, ka, va, qb, kb, vb = _proj_ab(x, p["norm_mix"][0][None, :], w_in, gsum, gain, cos, sin)
    ok, band_sh, sink_term, bias_sh = _bounded_tables(
        p["a_q_norm"][0], p["a_k_norm"][0], p["a_sink"][0], p["b_q_norm"][0], p["b_k_norm"][0],
        p["b_rpb"][0], band_tab, sink_col, bias_tab)
    o = lax.cond(
        ok,
        lambda: _attn_ab(qa, ka, va, band_sh, sink_term, qb, kb, vb, bias_sh, bounded=True),
        lambda: _attn_ab(qa, ka, va, band_tab, sink_col, qb, kb, vb, bias_tab, bounded=False))
    x = _out_ffn(x, o, w_out, p["norm_ffn"][0][None, :],
                 p["ffn_w_gate"][0].astype(BF16), p["ffn_w_up"][0].astype(BF16),
                 p["ffn_w_down"][0].astype(BF16))
    (shift, win, gq, gkv, wqt, wvt, wk, gqn, gkn, cosk, sink, cosq, sinq, wo) = _prep_mla(
        p["c_w_in"][0], p["c_q_lora_norm"][0], p["c_kv_lora_norm"][0], p["c_w_q_up"][0],
        p["c_w_kv_up"][0], p["c_q_norm"][0], p["c_k_norm"][0], p["c_w_out"][0], seq)
    qt, k, vt = _proj_mla(shift, x, p["norm_mix"][1][None, :], win, gq, gkv, wqt, wvt, wk, gqn, gkn,
                          cosk, sink, cosq, sinq)
    o = lax.cond(shift[0] <= SHIFT_MAX,
                 functools.partial(_attn_mla, bounded=True),
                 functools.partial(_attn_mla, bounded=False), qt, k, vt)
    x = _out_ffn(x, o, wo, p["norm_ffn"][1][None, :], p["ffn_w_gate"][1].astype(BF16),
                 p["ffn_w_up"][1].astype(BF16), p["ffn_w_down"][1].astype(BF16))
    return x


def kernel(x_prompt, x_sample, norm_mix, norm_ffn, ab_w_in, ab_w_out, a_q_norm, a_k_norm, a_sink,
           b_q_norm, b_k_norm, b_rpb, c_w_in, c_q_lora_norm, c_kv_lora_norm, c_w_q_up, c_w_kv_up,
           c_q_norm, c_k_norm, c_w_out, ffn_w_gate, ffn_w_up, ffn_w_down):
    p = dict(norm_mix=norm_mix, norm_ffn=norm_ffn, ab_w_in=ab_w_in, ab_w_out=ab_w_out,
             a_q_norm=a_q_norm, a_k_norm=a_k_norm, a_sink=a_sink, b_q_norm=b_q_norm,
             b_k_norm=b_k_norm, b_rpb=b_rpb, c_w_in=c_w_in, c_q_lora_norm=c_q_lora_norm,
             c_kv_lora_norm=c_kv_lora_norm, c_w_q_up=c_w_q_up, c_w_kv_up=c_w_kv_up,
             c_q_norm=c_q_norm, c_k_norm=c_k_norm, c_w_out=c_w_out, ffn_w_gate=ffn_w_gate,
             ffn_w_up=ffn_w_up, ffn_w_down=ffn_w_down)
    return _trunk(x_prompt, p), _trunk(x_sample, p)
```

```python
import functools
import math

import numpy as np
import jax
import jax.numpy as jnp
from jax import lax
from jax.experimental import pallas as pl
from jax.experimental.pallas import tpu as pltpu

F32 = jnp.float32
BF16 = jnp.bfloat16

D_MODEL = 1024
GRID_W = 64
HEAD_DIM = 64
ROPE_THETA = 10000.0
EPS = 1e-6
NEG_INF = -1e30
BLOCK = 128
A_HEADS = 8
A_KV_HEADS = 2
WINDOW = 128
B_HEADS = 8
NA_MAX_H = 8
NA_W = 16
C_HEADS = 16
C_NOPE = 64
C_ROPE = 32
C_VDIM = 64
C_QK = C_NOPE + C_ROPE
Q_LORA = 384
KV_LORA = 256
A_QW = A_HEADS * HEAD_DIM
A_KVW = A_KV_HEADS * HEAD_DIM
B_W = B_HEADS * HEAD_DIM

LANES = 128
V7X_VMEM_BYTES = 64 * 1024 * 1024
VMEM_LIMIT = V7X_VMEM_BYTES * 3 // 4

TM_PROJ = 512
TM_FFN = 512
TQ_MLA = 1024
TQ_MLA_BOUNDED = 2048
MLA_CHUNKS_PER_TRIP = 8
NA_ROWS = 8
A_QBLOCKS = 4
V_ROWS = C_VDIM + 16

PERM_A = (0, 4, 1, 5, 2, 6, 3, 7)

LOG2E = math.log2(math.e)
SHIFT_MAX = 60.0
BF16_NORM_MARGIN = 1.0 + 2.0 ** -6
NT_DIMS = (((1,), (1,)), ((), ()))


def _cparams(*sem):
    return pltpu.CompilerParams(dimension_semantics=sem, vmem_limit_bytes=VMEM_LIMIT)


def _resident(shape):
    nd = len(shape)
    return pl.BlockSpec(shape, lambda *_: (0,) * nd, pipeline_mode=pl.Buffered(1))


def _rms_rows(x, gain):
    ms = jnp.mean(x * x, axis=-1, keepdims=True)
    return x * lax.rsqrt(ms + EPS) * gain


def _proj_ab_kernel(x_ref, g_ref, w_ref, gsum_ref, gain_ref, cos_ref, sin_ref,
                    qa_ref, ka_ref, va_ref, qb_ref, kb_ref, vb_ref):
    h = _rms_rows(x_ref[0], g_ref[...]).astype(BF16)
    y = jnp.dot(h, w_ref[...], preferred_element_type=F32)
    gsum = gsum_ref[...]
    cos = cos_ref[...]
    sin = sin_ref[...]
    lane = lax.broadcasted_iota(jnp.int32, cos.shape, 1)
    first_half = (lane % HEAD_DIM) < (HEAD_DIM // 2)

    def head_norm(c):
        yc = y[:, 2 * LANES * c:2 * LANES * (c + 1)]
        ss = jnp.dot((yc * yc).astype(BF16), gsum, preferred_element_type=F32)
        return yc * lax.rsqrt(ss * (1.0 / HEAD_DIM) + EPS) * gain_ref[:, 2 * LANES * c:2 * LANES * (c + 1)]

    def rope(v):
        swapped = jnp.where(first_half, pltpu.roll(v, LANES - HEAD_DIM // 2, 1),
                            pltpu.roll(v, HEAD_DIM // 2, 1))
        return v * cos + swapped * sin

    for c in range(2):
        yn = head_norm(c)
        for b in range(2):
            qa_ref[0, :, LANES * (2 * c + b):LANES * (2 * c + b + 1)] = rope(
                yn[:, LANES * b:LANES * (b + 1)]).astype(BF16)
    for c in range(2):
        qb_ref[0, :, 2 * LANES * c:2 * LANES * (c + 1)] = head_norm(2 + c).astype(BF16)
    for c in range(2):
        kb_ref[0, :, 2 * LANES * c:2 * LANES * (c + 1)] = head_norm(4 + c).astype(BF16)
    ka_ref[0] = rope(head_norm(6)[:, :LANES]).astype(BF16)
    va_ref[0] = y[:, 13 * LANES:14 * LANES].astype(BF16)
    vb_ref[0] = y[:, 14 * LANES:18 * LANES].astype(BF16)


def _proj_ab(x, g, w, gsum, gain, cos, sin):
    bsz, seq, _ = x.shape
    tm = TM_PROJ
    nt = seq // tm
    tok = lambda width: pl.BlockSpec((1, tm, width), lambda b, i: (b, i, 0))
    tab = pl.BlockSpec((tm, LANES), lambda b, i: (i, 0))
    out = lambda width: jax.ShapeDtypeStruct((bsz, seq, width), BF16)
    return pl.pallas_call(
        _proj_ab_kernel,
        grid=(bsz, nt),
        in_specs=[tok(D_MODEL), _resident(g.shape), _resident(w.shape), _resident(gsum.shape),
                  _resident(gain.shape), tab, tab],
        out_specs=[tok(A_QW), tok(A_KVW), tok(A_KVW), tok(B_W), tok(B_W), tok(B_W)],
        out_shape=[out(A_QW), out(A_KVW), out(A_KVW), out(B_W), out(B_W), out(B_W)],
        compiler_params=_cparams("parallel", "parallel"),
        name="proj_ab",
    )(x, g, w, gsum, gain, cos, sin)


def _attn_a_body(q_ref, kp_ref, kc_ref, kn_ref, vp_ref, vc_ref, vn_ref, bias_ref, sink_ref, o_ref,
                 *, nsteps, lane0, bounded):
    n = pl.program_id(1)
    k = jnp.concatenate([kp_ref[0], kc_ref[0], kn_ref[0]], axis=0)
    v = jnp.concatenate([vp_ref[0], vc_ref[0], vn_ref[0]], axis=0)
    vext = jnp.concatenate([v, jnp.ones(v.shape, BF16)], axis=1)
    lane = lax.broadcasted_iota(jnp.int32, (BLOCK, LANES), 1)
    lo = lane < HEAD_DIM
    zero = jnp.zeros((BLOCK, LANES), BF16)
    for t in range(A_QBLOCKS):
        if t == 0:
            bias = bias_ref[jnp.where(n == 0, 0, 1)]
        elif t == A_QBLOCKS - 1:
            bias = bias_ref[jnp.where(n == nsteps - 1, 2, 1)]
        else:
            bias = bias_ref[1]
        kw = k[BLOCK * t:BLOCK * (t + 3)]
        vw = vext[BLOCK * t:BLOCK * (t + 3)]
        for blk in range(A_QW // LANES):
            qp = q_ref[0, BLOCK * t:BLOCK * (t + 1), LANES * blk:LANES * (blk + 1)]
            qm = jnp.concatenate([jnp.where(lo, qp, zero), jnp.where(lo, zero, qp)], axis=0)
            s = lax.dot_general(qm, kw, NT_DIMS, preferred_element_type=F32) + bias
            if bounded:
                p = jnp.exp2(s)
                sink_term = sink_ref[blk]
            else:
                sink = sink_ref[blk]
                m = jnp.maximum(jnp.max(s, axis=-1, keepdims=True), sink)
                p = jnp.exp2(s - m)
                sink_term = jnp.exp2(sink - m)
            oe = jnp.dot(p.astype(BF16), vw, preferred_element_type=F32)
            o = oe[:, :LANES] / (oe[:, LANES:] + sink_term)
            o_ref[0, BLOCK * t:BLOCK * (t + 1), lane0 + LANES * blk:lane0 + LANES * (blk + 1)] = (
                jnp.where(lo, o[:BLOCK], o[BLOCK:]).astype(BF16))


def _attn_b_body(q_ref, kp_ref, kc_ref, kn_ref, vp_ref, vc_ref, vn_ref, bias_ref, o_ref,
                 kbuf, vbuf, *, nrb, lane0, bounded):
    rb = pl.program_id(1)
    blk_tok = NA_ROWS * GRID_W
    half_tok = blk_tok // 2
    win_tok = NA_MAX_H * GRID_W
    kbuf[0:half_tok] = kp_ref[0, half_tok:blk_tok]
    kbuf[half_tok:half_tok + blk_tok] = kc_ref[0]
    kbuf[half_tok + blk_tok:2 * blk_tok] = kn_ref[0, 0:half_tok]
    vbuf[0:half_tok] = vp_ref[0, half_tok:blk_tok]
    vbuf[half_tok:half_tok + blk_tok] = vc_ref[0]
    vbuf[half_tok + blk_tok:2 * blk_tok] = vn_ref[0, 0:half_tok]
    lane = lax.broadcasted_iota(jnp.int32, (GRID_W, LANES), 1)
    lo = lane < HEAD_DIM
    zero = jnp.zeros((GRID_W, LANES), BF16)
    ones = jnp.ones((win_tok, LANES), BF16)
    mid = NA_MAX_H // 2
    for t in range(NA_ROWS):
        off = jnp.where(rb == 0, max(t, mid), jnp.where(rb == nrb - 1, min(t, mid), t))
        didx = t + mid - off
        kstart = pl.multiple_of(off * GRID_W, GRID_W)
        for blk in range(B_W // LANES):
            qp = q_ref[0, GRID_W * t:GRID_W * (t + 1), LANES * blk:LANES * (blk + 1)]
            qm = jnp.concatenate([jnp.where(lo, qp, zero), jnp.where(lo, zero, qp)], axis=0)
            kw = kbuf[pl.ds(kstart, win_tok), LANES * blk:LANES * (blk + 1)]
            vw = vbuf[pl.ds(kstart, win_tok), LANES * blk:LANES * (blk + 1)]
            s = lax.dot_general(qm, kw, NT_DIMS, preferred_element_type=F32)
            s = s + bias_ref[didx, blk]
            p = jnp.exp2(s) if bounded else jnp.exp2(s - jnp.max(s, axis=-1, keepdims=True))
            oe = jnp.dot(p.astype(BF16), jnp.concatenate([vw, ones], axis=1),
                         preferred_element_type=F32)
            o = oe[:, :LANES] / oe[:, LANES:]
            o_ref[0, GRID_W * t:GRID_W * (t + 1), lane0 + LANES * blk:lane0 + LANES * (blk + 1)] = (
                jnp.where(lo, o[:GRID_W], o[GRID_W:]).astype(BF16))


def _attn_ab_kernel(*refs, nsteps, bounded):
    a_refs, b_refs, o_ref, scratch = refs[:9], refs[9:17], refs[17], refs[18:]
    _attn_a_body(*a_refs, o_ref, nsteps=nsteps, lane0=0, bounded=bounded)
    _attn_b_body(*b_refs, o_ref, *scratch, nrb=nsteps, lane0=A_QW, bounded=bounded)


def _attn_ab(qa, ka, va, band_tab, sink_col, qb, kb, vb, bias_tab, bounded):
    bsz, seq, _ = qa.shape
    tok = A_QBLOCKS * BLOCK
    assert tok == NA_ROWS * GRID_W
    nsteps = seq // tok
    assert nsteps >= 2, "neighbourhood attention needs at least two row blocks"
    nb = seq // BLOCK
    cur = lambda b, n: (b, n, 0)
    a_edge = lambda f: pl.BlockSpec((1, BLOCK, A_KVW), f)
    a_prev = lambda b, n: (b, jnp.maximum(A_QBLOCKS * n - 1, 0), 0)
    a_next = lambda b, n: (b, jnp.minimum(A_QBLOCKS * (n + 1), nb - 1), 0)
    a_mid = pl.BlockSpec((1, tok, A_KVW), cur)
    b_blk = lambda f: pl.BlockSpec((1, tok, B_W), f)
    b_prev = lambda b, n: (b, jnp.maximum(n - 1, 0), 0)
    b_next = lambda b, n: (b, jnp.minimum(n + 1, nsteps - 1), 0)
    return pl.pallas_call(
        functools.partial(_attn_ab_kernel, nsteps=nsteps, bounded=bounded),
        grid=(bsz, nsteps),
        in_specs=[pl.BlockSpec((1, tok, A_QW), cur), a_edge(a_prev), a_mid, a_edge(a_next),
                  a_edge(a_prev), a_mid, a_edge(a_next), _resident(band_tab.shape),
                  _resident(sink_col.shape),
                  b_blk(cur), b_blk(b_prev), b_blk(cur), b_blk(b_next),
                  b_blk(b_prev), b_blk(cur), b_blk(b_next), _resident(bias_tab.shape)],
        out_specs=pl.BlockSpec((1, tok, A_QW + B_W), cur),
        out_shape=jax.ShapeDtypeStruct((bsz, seq, A_QW + B_W), BF16),
        scratch_shapes=[pltpu.VMEM((2 * tok, B_W), BF16), pltpu.VMEM((2 * tok, B_W), BF16)],
        compiler_params=_cparams("parallel", "parallel"),
        name="attn_ab_bounded" if bounded else "attn_ab",
    )(qa, ka, ka, ka, va, va, va, band_tab, sink_col, qb, kb, kb, kb, vb, vb, vb, bias_tab)


def _out_ffn_kernel(x_ref, o_ref, wo_ref, g_ref, wg_ref, wu_ref, wd_ref, y_ref):
    x1 = x_ref[0] + jnp.dot(o_ref[0], wo_ref[...], preferred_element_type=F32)
    h = _rms_rows(x1, g_ref[...]).astype(BF16)
    gate = jnp.dot(h, wg_ref[...], preferred_element_type=F32)
    up = jnp.dot(h, wu_ref[...], preferred_element_type=F32)
    act = (gate / (1.0 + jnp.exp(-gate)) * up).astype(BF16)
    y_ref[0] = x1 + jnp.dot(act, wd_ref[...], preferred_element_type=F32)


def _out_ffn(x, o, wo, g, wg, wu, wd):
    bsz, seq, _ = x.shape
    tm = TM_FFN
    tok = lambda width: pl.BlockSpec((1, tm, width), lambda b, i: (b, i, 0))
    consts = (wo, g, wg, wu, wd)
    return pl.pallas_call(
        _out_ffn_kernel,
        grid=(bsz, seq // tm),
        in_specs=[tok(D_MODEL), tok(o.shape[-1]), *[_resident(c.shape) for c in consts]],
        out_specs=tok(D_MODEL),
        out_shape=jax.ShapeDtypeStruct(x.shape, F32),
        compiler_params=_cparams("parallel", "parallel"),
        name="out_ffn",
    )(x, o, *consts)


def _proj_mla_kernel(shift_ref, x_ref, g_ref, win_ref, gq_ref, gkv_ref, wqt_ref, wvt_ref, wk_ref,
                     gqn_ref, gkn_ref, cosk_ref, sink_ref, cosq_ref, sinq_ref,
                     q_ref, k_ref, v_ref):
    tm = x_ref.shape[1]
    shift = shift_ref[0]
    h = _rms_rows(x_ref[0], g_ref[...]).astype(BF16)
    y = jnp.dot(h, win_ref[...], preferred_element_type=F32)
    cq = _rms_rows(y[:, :Q_LORA], gq_ref[...]).astype(BF16)
    ckv = _rms_rows(y[:, Q_LORA:Q_LORA + KV_LORA], gkv_ref[...]).astype(BF16)
    kpe = y[:, Q_LORA + KV_LORA:]

    qt = lax.dot_general(wqt_ref[...], cq, NT_DIMS, preferred_element_type=F32)
    qt = qt.reshape(C_HEADS, LANES, tm)
    ssq = jnp.sum(qt * qt, axis=1, keepdims=True)
    qn = qt * lax.rsqrt(ssq * (1.0 / C_QK) + EPS) * gqn_ref[...][None]
    half = C_ROPE // 2
    r1 = qn[:, C_NOPE:C_NOPE + half]
    r2 = qn[:, C_NOPE + half:C_QK]
    cos = cosq_ref[...][None]
    sin = sinq_ref[...][None]
    pad_row = lax.broadcasted_iota(jnp.int32, (C_HEADS, LANES - C_QK, tm), 1)
    q_out = jnp.concatenate([qn[:, :C_NOPE], r1 * cos - r2 * sin, r2 * cos + r1 * sin,
                             jnp.where(pad_row == 0, -shift, 0.0)], axis=1)
    q_ref[0] = q_out.astype(BF16)

    vt = lax.dot_general(wvt_ref[...], ckv, NT_DIMS, preferred_element_type=F32)
    vt = vt.reshape(C_HEADS, C_VDIM, tm).astype(BF16)
    row = lax.broadcasted_iota(jnp.int32, (C_HEADS, V_ROWS - C_VDIM, tm), 1)
    v_ref[0, :, 0] = jnp.concatenate([vt, jnp.where(row == 0, 1.0, 0.0).astype(BF16)], axis=1)

    kn = jnp.dot(ckv, wk_ref[...], preferred_element_type=F32)
    gk = gkn_ref[...]
    ss_pe = jnp.sum(kpe * kpe, axis=-1, keepdims=True)
    kg = kpe * gk
    lane = lax.broadcasted_iota(jnp.int32, kg.shape, 1)
    swapped = jnp.where(lane < C_NOPE + half, pltpu.roll(kg, LANES - half, 1),
                        pltpu.roll(kg, half, 1))
    kr = kg * cosk_ref[...] + swapped * sink_ref[...]
    for hd in range(C_HEADS):
        kh = kn[:, LANES * hd:LANES * (hd + 1)]
        ss = jnp.sum(kh * kh, axis=-1, keepdims=True) + ss_pe
        kval = (kh * gk + kr) * lax.rsqrt(ss * (1.0 / C_QK) + EPS)
        k_ref[0, :, LANES * hd:LANES * (hd + 1)] = jnp.where(lane == C_QK, 1.0, kval).astype(BF16)


def _proj_mla(shift, x, g, win, gq, gkv, wqt, wvt, wk, gqn, gkn, cosk, sink, cosq, sinq):
    bsz, seq, _ = x.shape
    tm = TM_PROJ
    nt = seq // tm
    half = C_ROPE // 2
    return pl.pallas_call(
        _proj_mla_kernel,
        grid=(bsz, nt),
        in_specs=[pl.BlockSpec(memory_space=pltpu.SMEM),
                  pl.BlockSpec((1, tm, D_MODEL), lambda b, i: (b, i, 0)),
                  _resident(g.shape), _resident(win.shape), _resident(gq.shape),
                  _resident(gkv.shape), _resident(wqt.shape), _resident(wvt.shape),
                  _resident(wk.shape), _resident(gqn.shape), _resident(gkn.shape),
                  pl.BlockSpec((tm, LANES), lambda b, i: (i, 0)),
                  pl.BlockSpec((tm, LANES), lambda b, i: (i, 0)),
                  pl.BlockSpec((half, tm), lambda b, i: (0, i)),
                  pl.BlockSpec((half, tm), lambda b, i: (0, i))],
        out_specs=[pl.BlockSpec((1, C_HEADS, LANES, tm), lambda b, i: (b, 0, 0, i)),
                   pl.BlockSpec((1, tm, C_HEADS * LANES), lambda b, i: (b, i, 0)),
                   pl.BlockSpec((1, C_HEADS, 1, V_ROWS, tm), lambda b, i: (b, 0, i, 0, 0))],
        out_shape=[jax.ShapeDtypeStruct((bsz, C_HEADS, LANES, seq), BF16),
                   jax.ShapeDtypeStruct((bsz, seq, C_HEADS * LANES), BF16),
                   jax.ShapeDtypeStruct((bsz, C_HEADS, nt, V_ROWS, tm), BF16)],
        compiler_params=_cparams("parallel", "parallel"),
        name="proj_mla",
    )(shift, x, g, win, gq, gkv, wqt, wvt, wk, gqn, gkn, cosk, sink, cosq, sinq)


def _attn_mla_kernel(q_ref, k_ref, v_ref, o_ref, s_buf, cmax_buf, p_buf, alpha_buf, m_buf, acc_buf,
                     *, nchunks, tk):
    heads = range(2)

    def stage_a(c, slot):
        start = pl.multiple_of(c * tk, tk)
        for hh in heads:
            kc = k_ref[0, pl.ds(start, tk), LANES * hh:LANES * (hh + 1)]
            s = jnp.dot(kc, q_ref[0, hh], preferred_element_type=F32)
            s_buf[hh, slot] = s
            cmax_buf[hh, slot] = jnp.max(s, axis=0, keepdims=True)

    def stage_b(slot):
        for hh in heads:
            m_old = m_buf[hh]
            m_new = jnp.maximum(m_old, cmax_buf[hh, slot])
            alpha_buf[hh, slot] = jnp.exp2(m_old - m_new)
            m_buf[hh] = m_new
            p_buf[hh, slot] = jnp.exp2(s_buf[hh, slot] - m_new).astype(BF16)

    def stage_c(c, slot):
        for hh in heads:
            pv = jnp.dot(v_ref[0, hh, c], p_buf[hh, slot], preferred_element_type=F32)
            acc_buf[hh] = alpha_buf[hh, slot] * acc_buf[hh] + pv

    m_buf[...] = jnp.full(m_buf.shape, NEG_INF, F32)
    acc_buf[...] = jnp.zeros(acc_buf.shape, F32)
    p_buf[:, 1] = jnp.zeros(p_buf.shape[:1] + p_buf.shape[2:], BF16)
    alpha_buf[:, 1] = jnp.ones(alpha_buf.shape[:1] + alpha_buf.shape[2:], F32)
    stage_a(0, 0)

    def body(i, carry):
        c0 = 2 * i
        stage_b(0)
        stage_a(c0 + 1, 1)
        stage_c(jnp.maximum(c0 - 1, 0), 1)
        stage_b(1)
        stage_a(jnp.minimum(c0 + 2, nchunks - 1), 0)
        stage_c(c0, 0)
        return carry

    lax.fori_loop(0, nchunks // 2, body, 0)
    stage_c(nchunks - 1, 1)
    outs = [acc_buf[hh, :C_VDIM] / acc_buf[hh, C_VDIM:C_VDIM + 1] for hh in heads]
    o_ref[0] = jnp.concatenate(outs, axis=0).T.astype(BF16)


def _attn_mla_bounded_kernel(q_ref, k_ref, v_ref, o_ref, p_buf, acc_buf, *, nchunks, tk):
    heads = range(2)
    per_trip = math.gcd(nchunks, MLA_CHUNKS_PER_TRIP)
    assert per_trip % 2 == 0, "chunk c lives in buffer slot c % 2 of every trip"

    def stage_p(c, slot):
        start = pl.multiple_of(c * tk, tk)
        for hh in heads:
            kc = k_ref[0, pl.ds(start, tk), LANES * hh:LANES * (hh + 1)]
            s = jnp.dot(kc, q_ref[0, hh], preferred_element_type=F32)
            p_buf[hh, slot] = jnp.exp2(s).astype(BF16)

    def stage_c(c, slot):
        for hh in heads:
            acc_buf[hh] += jnp.dot(v_ref[0, hh, c], p_buf[hh, slot], preferred_element_type=F32)

    acc_buf[...] = jnp.zeros(acc_buf.shape, F32)
    stage_p(0, 0)

    def trip(c0, last):
        for j in range(per_trip):
            if not (last and j == per_trip - 1):
                stage_p(c0 + j + 1, (j + 1) % 2)
            stage_c(c0 + j, j % 2)

    def body(i, carry):
        trip(per_trip * i, last=False)
        return carry

    ntrips = nchunks // per_trip
    lax.fori_loop(0, ntrips - 1, body, 0)
    trip(per_trip * (ntrips - 1), last=True)
    outs = [acc_buf[hh, :C_VDIM] / acc_buf[hh, C_VDIM:C_VDIM + 1] for hh in heads]
    o_ref[0] = jnp.concatenate(outs, axis=0).T.astype(BF16)


def _attn_mla(qt, k, vt, bounded):
    bsz, _, _, seq = qt.shape
    nchunks, tk = vt.shape[2], vt.shape[4]
    assert nchunks % 2 == 0, "both pipelined loops handle key chunks in pairs"
    tq = TQ_MLA_BOUNDED if bounded else TQ_MLA
    if bounded:
        body = _attn_mla_bounded_kernel
        scratch = [pltpu.VMEM((2, 2, tk, tq), BF16),
                   pltpu.VMEM((2, V_ROWS, tq), F32)]
    else:
        body = _attn_mla_kernel
        scratch = [pltpu.VMEM((2, 2, tk, tq), F32),
                   pltpu.VMEM((2, 2, 1, tq), F32),
                   pltpu.VMEM((2, 2, tk, tq), BF16),
                   pltpu.VMEM((2, 2, 1, tq), F32),
                   pltpu.VMEM((2, 1, tq), F32),
                   pltpu.VMEM((2, V_ROWS, tq), F32)]
    return pl.pallas_call(
        functools.partial(body, nchunks=nchunks, tk=tk),
        grid=(bsz, C_HEADS // 2, seq // tq),
        in_specs=[pl.BlockSpec((1, 2, LANES, tq), lambda b, h, i: (b, h, 0, i)),
                  pl.BlockSpec((1, seq, 2 * LANES), lambda b, h, i: (b, 0, h)),
                  pl.BlockSpec((1, 2, nchunks, V_ROWS, tk), lambda b, h, i: (b, h, 0, 0, 0))],
        out_specs=pl.BlockSpec((1, tq, 2 * C_VDIM), lambda b, h, i: (b, i, h)),
        out_shape=jax.ShapeDtypeStruct((bsz, seq, C_HEADS * C_VDIM), BF16),
        scratch_shapes=scratch,
        compiler_params=_cparams("parallel", "parallel", "arbitrary"),
        name="attn_mla_bounded" if bounded else "attn_mla",
    )(qt, k, vt)


def _rope_angles(seq, half):
    inv_freq = ROPE_THETA ** (-jnp.arange(half, dtype=F32) / half)
    return jnp.arange(seq, dtype=F32)[:, None] * inv_freq[None, :]


def _prep_ab(ab_w_in, ab_w_out, a_q_norm, a_k_norm, a_sink, b_q_norm, b_k_norm, b_rpb, seq):
    o_ka, o_va, o_qb = A_QW, A_QW + A_KVW, A_QW + 2 * A_KVW
    o_kb, o_vb = o_qb + B_W, o_qb + 2 * B_W
    w_in = jnp.concatenate(
        [ab_w_in[:, HEAD_DIM * h:HEAD_DIM * (h + 1)] for h in PERM_A]
        + [ab_w_in[:, o_qb:o_vb], ab_w_in[:, o_ka:o_qb], ab_w_in[:, o_vb:]], axis=1).astype(BF16)
    w_out = jnp.concatenate([ab_w_out[HEAD_DIM * h:HEAD_DIM * (h + 1)] for h in PERM_A]
                            + [ab_w_out[A_QW:]]).astype(BF16)
    scale = HEAD_DIM ** -0.5 * LOG2E
    gain = jnp.concatenate([jnp.tile(a_q_norm * scale, A_HEADS), jnp.tile(b_q_norm * scale, B_HEADS),
                            jnp.tile(b_k_norm, B_HEADS), jnp.tile(a_k_norm, A_KV_HEADS),
                            jnp.ones((A_KVW,), F32)])[None, :].astype(F32)
    idx = np.arange(2 * LANES) // HEAD_DIM
    gsum = jnp.asarray(idx[:, None] == idx[None, :], BF16)
    ang = _rope_angles(seq, HEAD_DIM // 2)
    cos = jnp.tile(jnp.cos(ang), (1, LANES // (HEAD_DIM // 2)))
    sin = jnp.tile(jnp.concatenate([-jnp.sin(ang), jnp.sin(ang)], axis=1), (1, LANES // HEAD_DIM))
    sink_col = jnp.repeat(a_sink[np.array(PERM_A)] * LOG2E, BLOCK).reshape(
        A_QW // LANES, 2 * BLOCK, 1).astype(F32)
    i = np.arange(2 * BLOCK)[:, None] % BLOCK
    j = np.arange(3 * BLOCK)[None, :]
    band = (j - i >= BLOCK - WINDOW) & (j - i <= BLOCK + WINDOW)
    band = np.stack([band & (j >= BLOCK), band, band & (j < 2 * BLOCK)])
    band_tab = jnp.asarray(np.where(band, 0.0, NEG_INF), F32)

    c = np.arange(GRID_W)
    cs = np.clip(c - NA_W // 2, 0, GRID_W - NA_W)
    inwin = (c[None, :] >= cs[:, None]) & (c[None, :] < cs[:, None] + NA_W)
    dc = c[None, :] - c[:, None] + NA_W - 1
    pick = jnp.asarray(dc[:, :, None] == np.arange(2 * NA_W - 1), F32)
    tcol = jnp.sum(b_rpb.astype(F32)[:, :, None, None, :] * pick[None, None], axis=-1)
    tab = jnp.stack([tcol[:, NA_MAX_H - 1 - d:2 * NA_MAX_H - 1 - d] for d in range(NA_MAX_H)],
                    axis=1)
    tab = jnp.where(inwin[None, None, None], tab * LOG2E, NEG_INF)
    tab = tab.transpose(1, 0, 3, 2, 4).reshape(NA_MAX_H, B_HEADS // 2, 2 * GRID_W, NA_MAX_H * GRID_W)
    return w_in, w_out, gain, gsum, cos, sin, band_tab, sink_col, tab


def _bounded_tables(a_q_norm, a_k_norm, a_sink, b_q_norm, b_k_norm, b_rpb, band_tab, sink_col, bias_tab):
    scale = HEAD_DIM ** -0.5 * LOG2E
    bound_a = BF16_NORM_MARGIN * HEAD_DIM * jnp.max(jnp.abs(a_q_norm * scale)) * jnp.max(jnp.abs(a_k_norm))
    shift_a = jnp.maximum(bound_a, jnp.max(a_sink) * LOG2E)
    ok_a = bound_a + shift_a <= 2 * SHIFT_MAX
    bound_b = BF16_NORM_MARGIN * HEAD_DIM * jnp.max(jnp.abs(b_q_norm * scale)) * jnp.max(jnp.abs(b_k_norm))
    rpb = b_rpb.astype(F32) * LOG2E
    shift_b = bound_b + jnp.max(rpb)
    ok_b = bound_b + shift_b - jnp.min(rpb) <= 2 * SHIFT_MAX
    sink_term = jnp.broadcast_to(jnp.exp2(sink_col - shift_a), sink_col.shape[:2] + (LANES,))
    return ok_a & ok_b, band_tab - shift_a, sink_term, bias_tab - shift_b


def _prep_mla(c_w_in, c_q_lora_norm, c_kv_lora_norm, c_w_q_up, c_w_kv_up, c_q_norm, c_k_norm,
              c_w_out, seq):
    win = jnp.zeros((D_MODEL, Q_LORA + KV_LORA + LANES), F32)
    win = win.at[:, :Q_LORA + KV_LORA].set(c_w_in[:, :Q_LORA + KV_LORA])
    win = win.at[:, Q_LORA + KV_LORA + C_NOPE:Q_LORA + KV_LORA + C_QK].set(c_w_in[:, Q_LORA + KV_LORA:])
    wq = c_w_q_up.reshape(Q_LORA, C_HEADS, C_QK)
    wq = jnp.pad(wq, ((0, 0), (0, 0), (0, LANES - C_QK)))
    wqt = wq.reshape(Q_LORA, C_HEADS * LANES).T.astype(BF16)
    wkv = c_w_kv_up.reshape(KV_LORA, C_HEADS, C_NOPE + C_VDIM)
    wvt = wkv[:, :, C_NOPE:].reshape(KV_LORA, C_HEADS * C_VDIM).T.astype(BF16)
    wk = jnp.pad(wkv[:, :, :C_NOPE], ((0, 0), (0, 0), (0, LANES - C_NOPE)))
    wk = wk.reshape(KV_LORA, C_HEADS * LANES).astype(BF16)
    qscale = C_QK ** -0.5 * LOG2E
    shift = (BF16_NORM_MARGIN * C_QK * jnp.max(jnp.abs(c_q_norm * qscale)) * jnp.max(jnp.abs(c_k_norm)))
    shift = shift.reshape(1).astype(F32)
    gqn = jnp.pad(c_q_norm * qscale, (0, LANES - C_QK))[:, None].astype(F32)
    gkn = jnp.pad(c_k_norm, (0, LANES - C_QK))[None, :].astype(F32)
    half = C_ROPE // 2
    ang = _rope_angles(seq, half)
    cos, sin = jnp.cos(ang), jnp.sin(ang)
    zl = jnp.zeros((seq, C_NOPE), F32)
    zr = jnp.zeros((seq, LANES - C_QK), F32)
    cosk = jnp.concatenate([zl, cos, cos, zr], axis=1)
    sink = jnp.concatenate([zl, -sin, sin, zr], axis=1)
    return (shift, win.astype(BF16), c_q_lora_norm[None, :].astype(F32),
            c_kv_lora_norm[None, :].astype(F32), wqt, wvt, wk, gqn, gkn, cosk, sink, cos.T, sin.T,
            c_w_out.astype(BF16))


def _trunk(x, p):
    seq = x.shape[1]
    w_in, w_out, gain, gsum, cos, sin, band_tab, sink_col, bias_tab = _prep_ab(
        p["ab_w_in"][0], p["ab_w_out"][0], p["a_q_norm"][0], p["a_k_norm"][0], p["a_sink"][0],
        p["b_q_norm"][0], p["b_k_norm"][0], p["b_rpb"][0], seq)
    qa, ka, va, qb, kb, vb = _proj_ab(x, p["norm_mix"][0][None, :], w_in, gsum, gain, cos, sin)
    ok, band_sh, sink_term, bias_sh = _bounded_tables(
        p["a_q_norm"][0], p["a_k_norm"][0], p["a_sink"][0], p["b_q_norm"][0], p["b_k_norm"][0],
        p["b_rpb"][0], band_tab, sink_col, bias_tab)
    o = lax.cond(
        ok,
        lambda: _attn_ab(qa, ka, va, band_sh, sink_term, qb, kb, vb, bias_sh, bounded=True),
        lambda: _attn_ab(qa, ka, va, band_tab, sink_col, qb, kb, vb, bias_tab, bounded=False))
    x = _out_ffn(x, o, w_out, p["norm_ffn"][0][None, :],
                 p["ffn_w_gate"][0].astype(BF16), p["ffn_w_up"][0].astype(BF16),
                 p["ffn_w_down"][0].astype(BF16))
    (shift, win, gq, gkv, wqt, wvt, wk, gqn, gkn, cosk, sink, cosq, sinq, wo) = _prep_mla(
        p["c_w_in"][0], p["c_q_lora_norm"][0], p["c_kv_lora_norm"][0], p["c_w_q_up"][0],
        p["c_w_kv_up"][0], p["c_q_norm"][0], p["c_k_norm"][0], p["c_w_out"][0], seq)
    qt, k, vt = _proj_mla(shift, x, p["norm_mix"][1][None, :], win, gq, gkv, wqt, wvt, wk, gqn, gkn,
                          cosk, sink, cosq, sinq)
    o = lax.cond(shift[0] <= SHIFT_MAX,
                 functools.partial(_attn_mla, bounded=True),
                 functools.partial(_attn_mla, bounded=False), qt, k, vt)
    x = _out_ffn(x, o, wo, p["norm_ffn"][1][None, :], p["ffn_w_gate"][1].astype(BF16),
                 p["ffn_w_up"][1].astype(BF16), p["ffn_w_down"][1].astype(BF16))
    return x


def kernel(x_prompt, x_sample, norm_mix, norm_ffn, ab_w_in, ab_w_out, a_q_norm, a_k_norm, a_sink,
           b_q_norm, b_k_norm, b_rpb, c_w_in, c_q_lora_norm, c_kv_lora_norm, c_w_q_up, c_w_kv_up,
           c_q_norm, c_k_norm, c_w_out, ffn_w_gate, ffn_w_up, ffn_w_down):
    p = dict(norm_mix=norm_mix, norm_ffn=norm_ffn, ab_w_in=ab_w_in, ab_w_out=ab_w_out,
             a_q_norm=a_q_norm, a_k_norm=a_k_norm, a_sink=a_sink, b_q_norm=b_q_norm,
             b_k_norm=b_k_norm, b_rpb=b_rpb, c_w_in=c_w_in, c_q_lora_norm=c_q_lora_norm,
             c_kv_lora_norm=c_kv_lora_norm, c_w_q_up=c_w_q_up, c_w_kv_up=c_w_kv_up,
             c_q_norm=c_q_norm, c_k_norm=c_k_norm, c_w_out=c_w_out, ffn_w_gate=ffn_w_gate,
             ffn_w_up=ffn_w_up, ffn_w_down=ffn_w_down)
    return _trunk(x_prompt, p), _trunk(x_sample, p)
```

```python
import functools
import math

import numpy as np
import jax
import jax.numpy as jnp
from jax import lax
from jax.experimental import pallas as pl
from jax.experimental.pallas import tpu as pltpu

F32 = jnp.float32
BF16 = jnp.bfloat16

D_MODEL = 1024
GRID_W = 64
HEAD_DIM = 64
ROPE_THETA = 10000.0
EPS = 1e-6
NEG_INF = -1e30
BLOCK = 128
A_HEADS = 8
A_KV_HEADS = 2
WINDOW = 128
B_HEADS = 8
NA_MAX_H = 8
NA_W = 16
C_HEADS = 16
C_NOPE = 64
C_ROPE = 32
C_VDIM = 64
C_QK = C_NOPE + C_ROPE
Q_LORA = 384
KV_LORA = 256
A_QW = A_HEADS * HEAD_DIM
A_KVW = A_KV_HEADS * HEAD_DIM
B_W = B_HEADS * HEAD_DIM

LANES = 128
V7X_VMEM_BYTES = 64 * 1024 * 1024
VMEM_LIMIT = V7X_VMEM_BYTES * 3 // 4

TM_PROJ = 512
TM_FFN = 512
TQ_MLA = 1024
TQ_MLA_BOUNDED = 2048
MLA_QTILES_PER_STEP = 2
MLA_CHUNKS_PER_TRIP = 8
NA_ROWS = 8
A_QBLOCKS = 4
V_ROWS = C_VDIM + 16

PERM_A = (0, 4, 1, 5, 2, 6, 3, 7)

LOG2E = math.log2(math.e)
SHIFT_MAX = 60.0
BF16_NORM_MARGIN = 1.0 + 2.0 ** -6
NT_DIMS = (((1,), (1,)), ((), ()))


def _cparams(*sem):
    return pltpu.CompilerParams(dimension_semantics=sem, vmem_limit_bytes=VMEM_LIMIT)


def _resident(shape):
    nd = len(shape)
    return pl.BlockSpec(shape, lambda *_: (0,) * nd, pipeline_mode=pl.Buffered(1))


def _rms_rows(x, gain):
    ms = jnp.mean(x * x, axis=-1, keepdims=True)
    return x * lax.rsqrt(ms + EPS) * gain


def _proj_ab_kernel(x_ref, g_ref, w_ref, gsum_ref, gain_ref, cos_ref, sin_ref,
                    qa_ref, ka_ref, va_ref, qb_ref, kb_ref, vb_ref):
    h = _rms_rows(x_ref[0], g_ref[...]).astype(BF16)
    y = jnp.dot(h, w_ref[...], preferred_element_type=F32)
    gsum = gsum_ref[...]
    cos = cos_ref[...]
    sin = sin_ref[...]
    lane = lax.broadcasted_iota(jnp.int32, cos.shape, 1)
    first_half = (lane % HEAD_DIM) < (HEAD_DIM // 2)

    def head_norm(c):
        yc = y[:, 2 * LANES * c:2 * LANES * (c + 1)]
        ss = jnp.dot((yc * yc).astype(BF16), gsum, preferred_element_type=F32)
        return yc * lax.rsqrt(ss * (1.0 / HEAD_DIM) + EPS) * gain_ref[:, 2 * LANES * c:2 * LANES * (c + 1)]

    def rope(v):
        swapped = jnp.where(first_half, pltpu.roll(v, LANES - HEAD_DIM // 2, 1),
                            pltpu.roll(v, HEAD_DIM // 2, 1))
        return v * cos + swapped * sin

    for c in range(2):
        yn = head_norm(c)
        for b in range(2):
            qa_ref[0, :, LANES * (2 * c + b):LANES * (2 * c + b + 1)] = rope(
                yn[:, LANES * b:LANES * (b + 1)]).astype(BF16)
    for c in range(2):
        qb_ref[0, :, 2 * LANES * c:2 * LANES * (c + 1)] = head_norm(2 + c).astype(BF16)
    for c in range(2):
        kb_ref[0, :, 2 * LANES * c:2 * LANES * (c + 1)] = head_norm(4 + c).astype(BF16)
    ka_ref[0] = rope(head_norm(6)[:, :LANES]).astype(BF16)
    va_ref[0] = y[:, 13 * LANES:14 * LANES].astype(BF16)
    vb_ref[0] = y[:, 14 * LANES:18 * LANES].astype(BF16)


def _proj_ab(x, g, w, gsum, gain, cos, sin):
    bsz, seq, _ = x.shape
    tm = TM_PROJ
    nt = seq // tm
    tok = lambda width: pl.BlockSpec((1, tm, width), lambda b, i: (b, i, 0))
    tab = pl.BlockSpec((tm, LANES), lambda b, i: (i, 0))
    out = lambda width: jax.ShapeDtypeStruct((bsz, seq, width), BF16)
    return pl.pallas_call(
        _proj_ab_kernel,
        grid=(bsz, nt),
        in_specs=[tok(D_MODEL), _resident(g.shape), _resident(w.shape), _resident(gsum.shape),
                  _resident(gain.shape), tab, tab],
        out_specs=[tok(A_QW), tok(A_KVW), tok(A_KVW), tok(B_W), tok(B_W), tok(B_W)],
        out_shape=[out(A_QW), out(A_KVW), out(A_KVW), out(B_W), out(B_W), out(B_W)],
        compiler_params=_cparams("parallel", "parallel"),
        name="proj_ab",
    )(x, g, w, gsum, gain, cos, sin)


def _attn_a_body(q_ref, kp_ref, kc_ref, kn_ref, vp_ref, vc_ref, vn_ref, bias_ref, sink_ref, o_ref,
                 *, nsteps, lane0, bounded):
    n = pl.program_id(1)
    k = jnp.concatenate([kp_ref[0], kc_ref[0], kn_ref[0]], axis=0)
    v = jnp.concatenate([vp_ref[0], vc_ref[0], vn_ref[0]], axis=0)
    vext = jnp.concatenate([v, jnp.ones(v.shape, BF16)], axis=1)
    lane = lax.broadcasted_iota(jnp.int32, (BLOCK, LANES), 1)
    lo = lane < HEAD_DIM
    zero = jnp.zeros((BLOCK, LANES), BF16)
    for t in range(A_QBLOCKS):
        if t == 0:
            bias = bias_ref[jnp.where(n == 0, 0, 1)]
        elif t == A_QBLOCKS - 1:
            bias = bias_ref[jnp.where(n == nsteps - 1, 2, 1)]
        else:
            bias = bias_ref[1]
        kw = k[BLOCK * t:BLOCK * (t + 3)]
        vw = vext[BLOCK * t:BLOCK * (t + 3)]
        for blk in range(A_QW // LANES):
            qp = q_ref[0, BLOCK * t:BLOCK * (t + 1), LANES * blk:LANES * (blk + 1)]
            qm = jnp.concatenate([jnp.where(lo, qp, zero), jnp.where(lo, zero, qp)], axis=0)
            s = lax.dot_general(qm, kw, NT_DIMS, preferred_element_type=F32) + bias
            if bounded:
                p = jnp.exp2(s)
                sink_term = sink_ref[blk]
            else:
                sink = sink_ref[blk]
                m = jnp.maximum(jnp.max(s, axis=-1, keepdims=True), sink)
                p = jnp.exp2(s - m)
                sink_term = jnp.exp2(sink - m)
            oe = jnp.dot(p.astype(BF16), vw, preferred_element_type=F32)
            o = oe[:, :LANES] / (oe[:, LANES:] + sink_term)
            o_ref[0, BLOCK * t:BLOCK * (t + 1), lane0 + LANES * blk:lane0 + LANES * (blk + 1)] = (
                jnp.where(lo, o[:BLOCK], o[BLOCK:]).astype(BF16))


def _attn_b_body(q_ref, kp_ref, kc_ref, kn_ref, vp_ref, vc_ref, vn_ref, bias_ref, o_ref,
                 kbuf, vbuf, *, nrb, lane0, bounded):
    rb = pl.program_id(1)
    blk_tok = NA_ROWS * GRID_W
    half_tok = blk_tok // 2
    win_tok = NA_MAX_H * GRID_W
    kbuf[0:half_tok] = kp_ref[0, half_tok:blk_tok]
    kbuf[half_tok:half_tok + blk_tok] = kc_ref[0]
    kbuf[half_tok + blk_tok:2 * blk_tok] = kn_ref[0, 0:half_tok]
    vbuf[0:half_tok] = vp_ref[0, half_tok:blk_tok]
    vbuf[half_tok:half_tok + blk_tok] = vc_ref[0]
    vbuf[half_tok + blk_tok:2 * blk_tok] = vn_ref[0, 0:half_tok]
    lane = lax.broadcasted_iota(jnp.int32, (GRID_W, LANES), 1)
    lo = lane < HEAD_DIM
    zero = jnp.zeros((GRID_W, LANES), BF16)
    ones = jnp.ones((win_tok, LANES), BF16)
    mid = NA_MAX_H // 2
    for t in range(NA_ROWS):
        off = jnp.where(rb == 0, max(t, mid), jnp.where(rb == nrb - 1, min(t, mid), t))
        didx = t + mid - off
        kstart = pl.multiple_of(off * GRID_W, GRID_W)
        for blk in range(B_W // LANES):
            qp = q_ref[0, GRID_W * t:GRID_W * (t + 1), LANES * blk:LANES * (blk + 1)]
            qm = jnp.concatenate([jnp.where(lo, qp, zero), jnp.where(lo, zero, qp)], axis=0)
            kw = kbuf[pl.ds(kstart, win_tok), LANES * blk:LANES * (blk + 1)]
            vw = vbuf[pl.ds(kstart, win_tok), LANES * blk:LANES * (blk + 1)]
            s = lax.dot_general(qm, kw, NT_DIMS, preferred_element_type=F32)
            s = s + bias_ref[didx, blk]
            p = jnp.exp2(s) if bounded else jnp.exp2(s - jnp.max(s, axis=-1, keepdims=True))
            oe = jnp.dot(p.astype(BF16), jnp.concatenate([vw, ones], axis=1),
                         preferred_element_type=F32)
            o = oe[:, :LANES] / oe[:, LANES:]
            o_ref[0, GRID_W * t:GRID_W * (t + 1), lane0 + LANES * blk:lane0 + LANES * (blk + 1)] = (
                jnp.where(lo, o[:GRID_W], o[GRID_W:]).astype(BF16))


def _attn_ab_kernel(*refs, nsteps, bounded):
    a_refs, b_refs, o_ref, scratch = refs[:9], refs[9:17], refs[17], refs[18:]
    _attn_a_body(*a_refs, o_ref, nsteps=nsteps, lane0=0, bounded=bounded)
    _attn_b_body(*b_refs, o_ref, *scratch, nrb=nsteps, lane0=A_QW, bounded=bounded)


def _attn_ab(qa, ka, va, band_tab, sink_col, qb, kb, vb, bias_tab, bounded):
    bsz, seq, _ = qa.shape
    tok = A_QBLOCKS * BLOCK
    assert tok == NA_ROWS * GRID_W
    nsteps = seq // tok
    assert nsteps >= 2, "neighbourhood attention needs at least two row blocks"
    nb = seq // BLOCK
    cur = lambda b, n: (b, n, 0)
    a_edge = lambda f: pl.BlockSpec((1, BLOCK, A_KVW), f)
    a_prev = lambda b, n: (b, jnp.maximum(A_QBLOCKS * n - 1, 0), 0)
    a_next = lambda b, n: (b, jnp.minimum(A_QBLOCKS * (n + 1), nb - 1), 0)
    a_mid = pl.BlockSpec((1, tok, A_KVW), cur)
    b_blk = lambda f: pl.BlockSpec((1, tok, B_W), f)
    b_prev = lambda b, n: (b, jnp.maximum(n - 1, 0), 0)
    b_next = lambda b, n: (b, jnp.minimum(n + 1, nsteps - 1), 0)
    return pl.pallas_call(
        functools.partial(_attn_ab_kernel, nsteps=nsteps, bounded=bounded),
        grid=(bsz, nsteps),
        in_specs=[pl.BlockSpec((1, tok, A_QW), cur), a_edge(a_prev), a_mid, a_edge(a_next),
                  a_edge(a_prev), a_mid, a_edge(a_next), _resident(band_tab.shape),
                  _resident(sink_col.shape),
                  b_blk(cur), b_blk(b_prev), b_blk(cur), b_blk(b_next),
                  b_blk(b_prev), b_blk(cur), b_blk(b_next), _resident(bias_tab.shape)],
        out_specs=pl.BlockSpec((1, tok, A_QW + B_W), cur),
        out_shape=jax.ShapeDtypeStruct((bsz, seq, A_QW + B_W), BF16),
        scratch_shapes=[pltpu.VMEM((2 * tok, B_W), BF16), pltpu.VMEM((2 * tok, B_W), BF16)],
        compiler_params=_cparams("parallel", "parallel"),
        name="attn_ab_bounded" if bounded else "attn_ab",
    )(qa, ka, ka, ka, va, va, va, band_tab, sink_col, qb, kb, kb, kb, vb, vb, vb, bias_tab)


def _out_ffn_kernel(x_ref, o_ref, wo_ref, g_ref, wg_ref, wu_ref, wd_ref, y_ref):
    x1 = x_ref[0] + jnp.dot(o_ref[0], wo_ref[...], preferred_element_type=F32)
    h = _rms_rows(x1, g_ref[...]).astype(BF16)
    gate = jnp.dot(h, wg_ref[...], preferred_element_type=F32)
    up = jnp.dot(h, wu_ref[...], preferred_element_type=F32)
    act = (gate / (1.0 + jnp.exp(-gate)) * up).astype(BF16)
    y_ref[0] = x1 + jnp.dot(act, wd_ref[...], preferred_element_type=F32)


def _out_ffn(x, o, wo, g, wg, wu, wd):
    bsz, seq, _ = x.shape
    tm = TM_FFN
    tok = lambda width: pl.BlockSpec((1, tm, width), lambda b, i: (b, i, 0))
    consts = (wo, g, wg, wu, wd)
    return pl.pallas_call(
        _out_ffn_kernel,
        grid=(bsz, seq // tm),
        in_specs=[tok(D_MODEL), tok(o.shape[-1]), *[_resident(c.shape) for c in consts]],
        out_specs=tok(D_MODEL),
        out_shape=jax.ShapeDtypeStruct(x.shape, F32),
        compiler_params=_cparams("parallel", "parallel"),
        name="out_ffn",
    )(x, o, *consts)


def _proj_mla_kernel(shift_ref, x_ref, g_ref, win_ref, gq_ref, gkv_ref, wqt_ref, wvt_ref, wk_ref,
                     gqn_ref, gkn_ref, cosk_ref, sink_ref, cosq_ref, sinq_ref,
                     q_ref, k_ref, v_ref):
    tm = x_ref.shape[1]
    shift = shift_ref[0]
    h = _rms_rows(x_ref[0], g_ref[...]).astype(BF16)
    y = jnp.dot(h, win_ref[...], preferred_element_type=F32)
    cq = _rms_rows(y[:, :Q_LORA], gq_ref[...]).astype(BF16)
    ckv = _rms_rows(y[:, Q_LORA:Q_LORA + KV_LORA], gkv_ref[...]).astype(BF16)
    kpe = y[:, Q_LORA + KV_LORA:]

    qt = lax.dot_general(wqt_ref[...], cq, NT_DIMS, preferred_element_type=F32)
    qt = qt.reshape(C_HEADS, LANES, tm)
    ssq = jnp.sum(qt * qt, axis=1, keepdims=True)
    qn = qt * lax.rsqrt(ssq * (1.0 / C_QK) + EPS) * gqn_ref[...][None]
    half = C_ROPE // 2
    r1 = qn[:, C_NOPE:C_NOPE + half]
    r2 = qn[:, C_NOPE + half:C_QK]
    cos = cosq_ref[...][None]
    sin = sinq_ref[...][None]
    pad_row = lax.broadcasted_iota(jnp.int32, (C_HEADS, LANES - C_QK, tm), 1)
    q_out = jnp.concatenate([qn[:, :C_NOPE], r1 * cos - r2 * sin, r2 * cos + r1 * sin,
                             jnp.where(pad_row == 0, -shift, 0.0)], axis=1)
    q_ref[0] = q_out.astype(BF16)

    vt = lax.dot_general(wvt_ref[...], ckv, NT_DIMS, preferred_element_type=F32)
    vt = vt.reshape(C_HEADS, C_VDIM, tm).astype(BF16)
    row = lax.broadcasted_iota(jnp.int32, (C_HEADS, V_ROWS - C_VDIM, tm), 1)
    v_ref[0, :, 0] = jnp.concatenate([vt, jnp.where(row == 0, 1.0, 0.0).astype(BF16)], axis=1)

    kn = jnp.dot(ckv, wk_ref[...], preferred_element_type=F32)
    gk = gkn_ref[...]
    ss_pe = jnp.sum(kpe * kpe, axis=-1, keepdims=True)
    kg = kpe * gk
    lane = lax.broadcasted_iota(jnp.int32, kg.shape, 1)
    swapped = jnp.where(lane < C_NOPE + half, pltpu.roll(kg, LANES - half, 1),
                        pltpu.roll(kg, half, 1))
    kr = kg * cosk_ref[...] + swapped * sink_ref[...]
    for hd in range(C_HEADS):
        kh = kn[:, LANES * hd:LANES * (hd + 1)]
        ss = jnp.sum(kh * kh, axis=-1, keepdims=True) + ss_pe
        kval = (kh * gk + kr) * lax.rsqrt(ss * (1.0 / C_QK) + EPS)
        k_ref[0, :, LANES * hd:LANES * (hd + 1)] = jnp.where(lane == C_QK, 1.0, kval).astype(BF16)


def _proj_mla(shift, x, g, win, gq, gkv, wqt, wvt, wk, gqn, gkn, cosk, sink, cosq, sinq):
    bsz, seq, _ = x.shape
    tm = TM_PROJ
    nt = seq // tm
    half = C_ROPE // 2
    return pl.pallas_call(
        _proj_mla_kernel,
        grid=(bsz, nt),
        in_specs=[pl.BlockSpec(memory_space=pltpu.SMEM),
                  pl.BlockSpec((1, tm, D_MODEL), lambda b, i: (b, i, 0)),
                  _resident(g.shape), _resident(win.shape), _resident(gq.shape),
                  _resident(gkv.shape), _resident(wqt.shape), _resident(wvt.shape),
                  _resident(wk.shape), _resident(gqn.shape), _resident(gkn.shape),
                  pl.BlockSpec((tm, LANES), lambda b, i: (i, 0)),
                  pl.BlockSpec((tm, LANES), lambda b, i: (i, 0)),
                  pl.BlockSpec((half, tm), lambda b, i: (0, i)),
                  pl.BlockSpec((half, tm), lambda b, i: (0, i))],
        out_specs=[pl.BlockSpec((1, C_HEADS, LANES, tm), lambda b, i: (b, 0, 0, i)),
                   pl.BlockSpec((1, tm, C_HEADS * LANES), lambda b, i: (b, i, 0)),
                   pl.BlockSpec((1, C_HEADS, 1, V_ROWS, tm), lambda b, i: (b, 0, i, 0, 0))],
        out_shape=[jax.ShapeDtypeStruct((bsz, C_HEADS, LANES, seq), BF16),
                   jax.ShapeDtypeStruct((bsz, seq, C_HEADS * LANES), BF16),
                   jax.ShapeDtypeStruct((bsz, C_HEADS, nt, V_ROWS, tm), BF16)],
        compiler_params=_cparams("parallel", "parallel"),
        name="proj_mla",
    )(shift, x, g, win, gq, gkv, wqt, wvt, wk, gqn, gkn, cosk, sink, cosq, sinq)


def _attn_mla_kernel(q_ref, k_ref, v_ref, o_ref, s_buf, cmax_buf, p_buf, alpha_buf, m_buf, acc_buf,
                     *, nchunks, tk):
    heads = range(2)

    def stage_a(c, slot):
        start = pl.multiple_of(c * tk, tk)
        for hh in heads:
            kc = k_ref[0, pl.ds(start, tk), LANES * hh:LANES * (hh + 1)]
            s = jnp.dot(kc, q_ref[0, hh], preferred_element_type=F32)
            s_buf[hh, slot] = s
            cmax_buf[hh, slot] = jnp.max(s, axis=0, keepdims=True)

    def stage_b(slot):
        for hh in heads:
            m_old = m_buf[hh]
            m_new = jnp.maximum(m_old, cmax_buf[hh, slot])
            alpha_buf[hh, slot] = jnp.exp2(m_old - m_new)
            m_buf[hh] = m_new
            p_buf[hh, slot] = jnp.exp2(s_buf[hh, slot] - m_new).astype(BF16)

    def stage_c(c, slot):
        for hh in heads:
            pv = jnp.dot(v_ref[0, hh, c], p_buf[hh, slot], preferred_element_type=F32)
            acc_buf[hh] = alpha_buf[hh, slot] * acc_buf[hh] + pv

    m_buf[...] = jnp.full(m_buf.shape, NEG_INF, F32)
    acc_buf[...] = jnp.zeros(acc_buf.shape, F32)
    p_buf[:, 1] = jnp.zeros(p_buf.shape[:1] + p_buf.shape[2:], BF16)
    alpha_buf[:, 1] = jnp.ones(alpha_buf.shape[:1] + alpha_buf.shape[2:], F32)
    stage_a(0, 0)

    def body(i, carry):
        c0 = 2 * i
        stage_b(0)
        stage_a(c0 + 1, 1)
        stage_c(jnp.maximum(c0 - 1, 0), 1)
        stage_b(1)
        stage_a(jnp.minimum(c0 + 2, nchunks - 1), 0)
        stage_c(c0, 0)
        return carry

    lax.fori_loop(0, nchunks // 2, body, 0)
    stage_c(nchunks - 1, 1)
    outs = [acc_buf[hh, :C_VDIM] / acc_buf[hh, C_VDIM:C_VDIM + 1] for hh in heads]
    o_ref[0] = jnp.concatenate(outs, axis=0).T.astype(BF16)


def _attn_mla_bounded_kernel(q_ref, k_ref, v_ref, o_ref, p_buf, acc_buf, *, nchunks, tk):
    heads = range(2)
    ntiles, tq = acc_buf.shape[0], acc_buf.shape[-1]
    per_trip = math.gcd(nchunks, MLA_CHUNKS_PER_TRIP)
    assert per_trip % 2 == 0 and nchunks % 2 == 0, "work item g lives in buffer slot g % 2"

    def split(g):
        if isinstance(g, int):
            return g // nchunks, g % nchunks
        tile = lax.div(g, nchunks)
        return tile, g - tile * nchunks

    def stage_p(g, slot):
        tile, c = split(g)
        start = c * tk if isinstance(c, int) else pl.multiple_of(c * tk, tk)
        col = tile * tq if isinstance(tile, int) else pl.multiple_of(tile * tq, tq)
        for hh in heads:
            kc = k_ref[0, pl.ds(start, tk), LANES * hh:LANES * (hh + 1)]
            s = jnp.dot(kc, q_ref[0, hh, :, pl.ds(col, tq)], preferred_element_type=F32)
            p_buf[hh, slot] = jnp.exp2(s).astype(BF16)

    def stage_c(g, slot):
        tile, c = split(g)
        for hh in heads:
            acc_buf[tile, hh] += jnp.dot(v_ref[0, hh, c], p_buf[hh, slot],
                                         preferred_element_type=F32)

    acc_buf[...] = jnp.zeros(acc_buf.shape, F32)
    stage_p(0, 0)

    def trip(g0, last):
        for j in range(per_trip):
            if not (last and j == per_trip - 1):
                stage_p(g0 + j + 1, (j + 1) % 2)
            stage_c(g0 + j, j % 2)

    def body(i, carry):
        trip(per_trip * i, last=False)
        return carry

    ntrips = ntiles * nchunks // per_trip
    lax.fori_loop(0, ntrips - 1, body, 0)
    trip(per_trip * (ntrips - 1), last=True)
    for tile in range(ntiles):
        outs = [acc_buf[tile, hh, :C_VDIM] / acc_buf[tile, hh, C_VDIM:C_VDIM + 1] for hh in heads]
        o_ref[0, tq * tile:tq * (tile + 1)] = jnp.concatenate(outs, axis=0).T.astype(BF16)


def _attn_mla(qt, k, vt, bounded):
    bsz, _, _, seq = qt.shape
    nchunks, tk = vt.shape[2], vt.shape[4]
    assert nchunks % 2 == 0, "both pipelined loops handle key chunks in pairs"
    if bounded:
        body = _attn_mla_bounded_kernel
        ntiles = math.gcd(seq // TQ_MLA_BOUNDED, MLA_QTILES_PER_STEP)
        tq = TQ_MLA_BOUNDED * ntiles
        scratch = [pltpu.VMEM((2, 2, tk, TQ_MLA_BOUNDED), BF16),
                   pltpu.VMEM((ntiles, 2, V_ROWS, TQ_MLA_BOUNDED), F32)]
    else:
        tq = TQ_MLA
        body = _attn_mla_kernel
        scratch = [pltpu.VMEM((2, 2, tk, tq), F32),
                   pltpu.VMEM((2, 2, 1, tq), F32),
                   pltpu.VMEM((2, 2, tk, tq), BF16),
                   pltpu.VMEM((2, 2, 1, tq), F32),
                   pltpu.VMEM((2, 1, tq), F32),
                   pltpu.VMEM((2, V_ROWS, tq), F32)]
    return pl.pallas_call(
        functools.partial(body, nchunks=nchunks, tk=tk),
        grid=(bsz, C_HEADS // 2, seq // tq),
        in_specs=[pl.BlockSpec((1, 2, LANES, tq), lambda b, h, i: (b, h, 0, i)),
                  pl.BlockSpec((1, seq, 2 * LANES), lambda b, h, i: (b, 0, h)),
                  pl.BlockSpec((1, 2, nchunks, V_ROWS, tk), lambda b, h, i: (b, h, 0, 0, 0))],
        out_specs=pl.BlockSpec((1, tq, 2 * C_VDIM), lambda b, h, i: (b, i, h)),
        out_shape=jax.ShapeDtypeStruct((bsz, seq, C_HEADS * C_VDIM), BF16),
        scratch_shapes=scratch,
        compiler_params=_cparams("parallel", "parallel", "arbitrary"),
        name="attn_mla_bounded" if bounded else "attn_mla",
    )(qt, k, vt)


def _rope_angles(seq, half):
    inv_freq = ROPE_THETA ** (-jnp.arange(half, dtype=F32) / half)
    return jnp.arange(seq, dtype=F32)[:, None] * inv_freq[None, :]


def _prep_ab(ab_w_in, ab_w_out, a_q_norm, a_k_norm, a_sink, b_q_norm, b_k_norm, b_rpb, seq):
    o_ka, o_va, o_qb = A_QW, A_QW + A_KVW, A_QW + 2 * A_KVW
    o_kb, o_vb = o_qb + B_W, o_qb + 2 * B_W
    w_in = jnp.concatenate(
        [ab_w_in[:, HEAD_DIM * h:HEAD_DIM * (h + 1)] for h in PERM_A]
        + [ab_w_in[:, o_qb:o_vb], ab_w_in[:, o_ka:o_qb], ab_w_in[:, o_vb:]], axis=1).astype(BF16)
    w_out = jnp.concatenate([ab_w_out[HEAD_DIM * h:HEAD_DIM * (h + 1)] for h in PERM_A]
                            + [ab_w_out[A_QW:]]).astype(BF16)
    scale = HEAD_DIM ** -0.5 * LOG2E
    gain = jnp.concatenate([jnp.tile(a_q_norm * scale, A_HEADS), jnp.tile(b_q_norm * scale, B_HEADS),
                            jnp.tile(b_k_norm, B_HEADS), jnp.tile(a_k_norm, A_KV_HEADS),
                            jnp.ones((A_KVW,), F32)])[None, :].astype(F32)
    idx = np.arange(2 * LANES) // HEAD_DIM
    gsum = jnp.asarray(idx[:, None] == idx[None, :], BF16)
    ang = _rope_angles(seq, HEAD_DIM // 2)
    cos = jnp.tile(jnp.cos(ang), (1, LANES // (HEAD_DIM // 2)))
    sin = jnp.tile(jnp.concatenate([-jnp.sin(ang), jnp.sin(ang)], axis=1), (1, LANES // HEAD_DIM))
    sink_col = jnp.repeat(a_sink[np.array(PERM_A)] * LOG2E, BLOCK).reshape(
        A_QW // LANES, 2 * BLOCK, 1).astype(F32)
    i = np.arange(2 * BLOCK)[:, None] % BLOCK
    j = np.arange(3 * BLOCK)[None, :]
    band = (j - i >= BLOCK - WINDOW) & (j - i <= BLOCK + WINDOW)
    band = np.stack([band & (j >= BLOCK), band, band & (j < 2 * BLOCK)])
    band_tab = jnp.asarray(np.where(band, 0.0, NEG_INF), F32)

    c = np.arange(GRID_W)
    cs = np.clip(c - NA_W // 2, 0, GRID_W - NA_W)
    inwin = (c[None, :] >= cs[:, None]) & (c[None, :] < cs[:, None] + NA_W)
    dc = c[None, :] - c[:, None] + NA_W - 1
    pick = jnp.asarray(dc[:, :, None] == np.arange(2 * NA_W - 1), F32)
    tcol = jnp.sum(b_rpb.astype(F32)[:, :, None, None, :] * pick[None, None], axis=-1)
    tab = jnp.stack([tcol[:, NA_MAX_H - 1 - d:2 * NA_MAX_H - 1 - d] for d in range(NA_MAX_H)],
                    axis=1)
    tab = jnp.where(inwin[None, None, None], tab * LOG2E, NEG_INF)
    tab = tab.transpose(1, 0, 3, 2, 4).reshape(NA_MAX_H, B_HEADS // 2, 2 * GRID_W, NA_MAX_H * GRID_W)
    return w_in, w_out, gain, gsum, cos, sin, band_tab, sink_col, tab


def _bounded_tables(a_q_norm, a_k_norm, a_sink, b_q_norm, b_k_norm, b_rpb, band_tab, sink_col, bias_tab):
    scale = HEAD_DIM ** -0.5 * LOG2E
    bound_a = BF16_NORM_MARGIN * HEAD_DIM * jnp.max(jnp.abs(a_q_norm * scale)) * jnp.max(jnp.abs(a_k_norm))
    shift_a = jnp.maximum(bound_a, jnp.max(a_sink) * LOG2E)
    ok_a = bound_a + shift_a <= 2 * SHIFT_MAX
    bound_b = BF16_NORM_MARGIN * HEAD_DIM * jnp.max(jnp.abs(b_q_norm * scale)) * jnp.max(jnp.abs(b_k_norm))
    rpb = b_rpb.astype(F32) * LOG2E
    shift_b = bound_b + jnp.max(rpb)
    ok_b = bound_b + shift_b - jnp.min(rpb) <= 2 * SHIFT_MAX
    sink_term = jnp.broadcast_to(jnp.exp2(sink_col - shift_a), sink_col.shape[:2] + (LANES,))
    return ok_a & ok_b, band_tab - shift_a, sink_term, bias_tab - shift_b


def _prep_mla(c_w_in, c_q_lora_norm, c_kv_lora_norm, c_w_q_up, c_w_kv_up, c_q_norm, c_k_norm,
              c_w_out, seq):
    win = jnp.zeros((D_MODEL, Q_LORA + KV_LORA + LANES), F32)
    win = win.at[:, :Q_LORA + KV_LORA].set(c_w_in[:, :Q_LORA + KV_LORA])
    win = win.at[:, Q_LORA + KV_LORA + C_NOPE:Q_LORA + KV_LORA + C_QK].set(c_w_in[:, Q_LORA + KV_LORA:])
    wq = c_w_q_up.reshape(Q_LORA, C_HEADS, C_QK)
    wq = jnp.pad(wq, ((0, 0), (0, 0), (0, LANES - C_QK)))
    wqt = wq.reshape(Q_LORA, C_HEADS * LANES).T.astype(BF16)
    wkv = c_w_kv_up.reshape(KV_LORA, C_HEADS, C_NOPE + C_VDIM)
    wvt = wkv[:, :, C_NOPE:].reshape(KV_LORA, C_HEADS * C_VDIM).T.astype(BF16)
    wk = jnp.pad(wkv[:, :, :C_NOPE], ((0, 0), (0, 0), (0, LANES - C_NOPE)))
    wk = wk.reshape(KV_LORA, C_HEADS * LANES).astype(BF16)
    qscale = C_QK ** -0.5 * LOG2E
    shift = (BF16_NORM_MARGIN * C_QK * jnp.max(jnp.abs(c_q_norm * qscale)) * jnp.max(jnp.abs(c_k_norm)))
    shift = shift.reshape(1).astype(F32)
    gqn = jnp.pad(c_q_norm * qscale, (0, LANES - C_QK))[:, None].astype(F32)
    gkn = jnp.pad(c_k_norm, (0, LANES - C_QK))[None, :].astype(F32)
    half = C_ROPE // 2
    ang = _rope_angles(seq, half)
    cos, sin = jnp.cos(ang), jnp.sin(ang)
    zl = jnp.zeros((seq, C_NOPE), F32)
    zr = jnp.zeros((seq, LANES - C_QK), F32)
    cosk = jnp.concatenate([zl, cos, cos, zr], axis=1)
    sink = jnp.concatenate([zl, -sin, sin, zr], axis=1)
    return (shift, win.astype(BF16), c_q_lora_norm[None, :].astype(F32),
            c_kv_lora_norm[None, :].astype(F32), wqt, wvt, wk, gqn, gkn, cosk, sink, cos.T, sin.T,
            c_w_out.astype(BF16))


def _trunk(x, p):
    seq = x.shape[1]
    w_in, w_out, gain, gsum, cos, sin, band_tab, sink_col, bias_tab = _prep_ab(
        p["ab_w_in"][0], p["ab_w_out"][0], p["a_q_norm"][0], p["a_k_norm"][0], p["a_sink"][0],
        p["b_q_norm"][0], p["b_k_norm"][0], p["b_rpb"][0], seq)
    qa, ka, va, qb, kb, vb = _proj_ab(x, p["norm_mix"][0][None, :], w_in, gsum, gain, cos, sin)
    ok, band_sh, sink_term, bias_sh = _bounded_tables(
        p["a_q_norm"][0], p["a_k_norm"][0], p["a_sink"][0], p["b_q_norm"][0], p["b_k_norm"][0],
        p["b_rpb"][0], band_tab, sink_col, bias_tab)
    o = lax.cond(
        ok,
        lambda: _attn_ab(qa, ka, va, band_sh, sink_term, qb, kb, vb, bias_sh, bounded=True),
        lambda: _attn_ab(qa, ka, va, band_tab, sink_col, qb, kb, vb, bias_tab, bounded=False))
    x = _out_ffn(x, o, w_out, p["norm_ffn"][0][None, :],
                 p["ffn_w_gate"][0].astype(BF16), p["ffn_w_up"][0].astype(BF16),
                 p["ffn_w_down"][0].astype(BF16))
    (shift, win, gq, gkv, wqt, wvt, wk, gqn, gkn, cosk, sink, cosq, sinq, wo) = _prep_mla(
        p["c_w_in"][0], p["c_q_lora_norm"][0], p["c_kv_lora_norm"][0], p["c_w_q_up"][0],
        p["c_w_kv_up"][0], p["c_q_norm"][0], p["c_k_norm"][0], p["c_w_out"][0], seq)
    qt, k, vt = _proj_mla(shift, x, p["norm_mix"][1][None, :], win, gq, gkv, wqt, wvt, wk, gqn, gkn,
                          cosk, sink, cosq, sinq)
    o = lax.cond(shift[0] <= SHIFT_MAX,
                 functools.partial(_attn_mla, bounded=True),
                 functools.partial(_attn_mla, bounded=False), qt, k, vt)
    x = _out_ffn(x, o, wo, p["norm_ffn"][1][None, :], p["ffn_w_gate"][1].astype(BF16),
                 p["ffn_w_up"][1].astype(BF16), p["ffn_w_down"][1].astype(BF16))
    return x


def kernel(x_prompt, x_sample, norm_mix, norm_ffn, ab_w_in, ab_w_out, a_q_norm, a_k_norm, a_sink,
           b_q_norm, b_k_norm, b_rpb, c_w_in, c_q_lora_norm, c_kv_lora_norm, c_w_q_up, c_w_kv_up,
           c_q_norm, c_k_norm, c_w_out, ffn_w_gate, ffn_w_up, ffn_w_down):
    p = dict(norm_mix=norm_mix, norm_ffn=norm_ffn, ab_w_in=ab_w_in, ab_w_out=ab_w_out,
             a_q_norm=a_q_norm, a_k_norm=a_k_norm, a_sink=a_sink, b_q_norm=b_q_norm,
             b_k_norm=b_k_norm, b_rpb=b_rpb, c_w_in=c_w_in, c_q_lora_norm=c_q_lora_norm,
             c_kv_lora_norm=c_kv_lora_norm, c_w_q_up=c_w_q_up, c_w_kv_up=c_w_kv_up,
             c_q_norm=c_q_norm, c_k_norm=c_k_norm, c_w_out=c_w_out, ffn_w_gate=ffn_w_gate,
             ffn_w_up=ffn_w_up, ffn_w_down=ffn_w_down)
    return _trunk(x_prompt, p), _trunk(x_sample, p)
```

```python
import functools
import math

import numpy as np
import jax
import jax.numpy as jnp
from jax import lax
from jax.experimental import pallas as pl
from jax.experimental.pallas import tpu as pltpu

F32 = jnp.float32
BF16 = jnp.bfloat16

D_MODEL = 1024
GRID_W = 64
HEAD_DIM = 64
ROPE_THETA = 10000.0
EPS = 1e-6
NEG_INF = -1e30
BLOCK = 128
A_HEADS = 8
A_KV_HEADS = 2
WINDOW = 128
B_HEADS = 8
NA_MAX_H = 8
NA_W = 16
C_HEADS = 16
C_NOPE = 64
C_ROPE = 32
C_VDIM = 64
C_QK = C_NOPE + C_ROPE
Q_LORA = 384
KV_LORA = 256
A_QW = A_HEADS * HEAD_DIM
A_KVW = A_KV_HEADS * HEAD_DIM
B_W = B_HEADS * HEAD_DIM

LANES = 128
V7X_VMEM_BYTES = 64 * 1024 * 1024
VMEM_LIMIT = V7X_VMEM_BYTES * 3 // 4

TM_PROJ = 512
TM_FFN = 512
TQ_MLA = 1024
TQ_MLA_BOUNDED = 2048
MLA_CHUNKS_PER_TRIP = 8
NA_ROWS = 8
A_QBLOCKS = 4
V_ROWS = C_VDIM + 16

PERM_A = (0, 4, 1, 5, 2, 6, 3, 7)

LOG2E = math.log2(math.e)
SHIFT_MAX = 60.0
BF16_NORM_MARGIN = 1.0 + 2.0 ** -6
NT_DIMS = (((1,), (1,)), ((), ()))


def _cparams(*sem):
    return pltpu.CompilerParams(dimension_semantics=sem, vmem_limit_bytes=VMEM_LIMIT)


def _resident(shape):
    nd = len(shape)
    return pl.BlockSpec(shape, lambda *_: (0,) * nd, pipeline_mode=pl.Buffered(1))


def _rms_rows(x, gain):
    ms = jnp.mean(x * x, axis=-1, keepdims=True)
    return x * lax.rsqrt(ms + EPS) * gain


def _proj_ab_kernel(x_ref, g_ref, w_ref, gsum_ref, gain_ref, cos_ref, sin_ref,
                    qa_ref, ka_ref, va_ref, qb_ref, kb_ref, vb_ref):
    h = _rms_rows(x_ref[0], g_ref[...]).astype(BF16)
    y = jnp.dot(h, w_ref[...], preferred_element_type=F32)
    gsum = gsum_ref[...]
    cos = cos_ref[...]
    sin = sin_ref[...]
    lane = lax.broadcasted_iota(jnp.int32, cos.shape, 1)
    first_half = (lane % HEAD_DIM) < (HEAD_DIM // 2)

    def head_norm(c):
        yc = y[:, 2 * LANES * c:2 * LANES * (c + 1)]
        ss = jnp.dot((yc * yc).astype(BF16), gsum, preferred_element_type=F32)
        return yc * lax.rsqrt(ss * (1.0 / HEAD_DIM) + EPS) * gain_ref[:, 2 * LANES * c:2 * LANES * (c + 1)]

    def rope(v):
        swapped = jnp.where(first_half, pltpu.roll(v, LANES - HEAD_DIM // 2, 1),
                            pltpu.roll(v, HEAD_DIM // 2, 1))
        return v * cos + swapped * sin

    for c in range(2):
        yn = head_norm(c)
        for b in range(2):
            qa_ref[0, :, LANES * (2 * c + b):LANES * (2 * c + b + 1)] = rope(
                yn[:, LANES * b:LANES * (b + 1)]).astype(BF16)
    for c in range(2):
        qb_ref[0, :, 2 * LANES * c:2 * LANES * (c + 1)] = head_norm(2 + c).astype(BF16)
    for c in range(2):
        kb_ref[0, :, 2 * LANES * c:2 * LANES * (c + 1)] = head_norm(4 + c).astype(BF16)
    ka_ref[0] = rope(head_norm(6)[:, :LANES]).astype(BF16)
    va_ref[0] = y[:, 13 * LANES:14 * LANES].astype(BF16)
    vb_ref[0] = y[:, 14 * LANES:18 * LANES].astype(BF16)


def _proj_ab(x, g, w, gsum, gain, cos, sin):
    bsz, seq, _ = x.shape
    tm = TM_PROJ
    nt = seq // tm
    tok = lambda width: pl.BlockSpec((1, tm, width), lambda b, i: (b, i, 0))
    tab = pl.BlockSpec((tm, LANES), lambda b, i: (i, 0))
    out = lambda width: jax.ShapeDtypeStruct((bsz, seq, width), BF16)
    return pl.pallas_call(
        _proj_ab_kernel,
        grid=(bsz, nt),
        in_specs=[tok(D_MODEL), _resident(g.shape), _resident(w.shape), _resident(gsum.shape),
                  _resident(gain.shape), tab, tab],
        out_specs=[tok(A_QW), tok(A_KVW), tok(A_KVW), tok(B_W), tok(B_W), tok(B_W)],
        out_shape=[out(A_QW), out(A_KVW), out(A_KVW), out(B_W), out(B_W), out(B_W)],
        compiler_params=_cparams("parallel", "parallel"),
        name="proj_ab",
    )(x, g, w, gsum, gain, cos, sin)


def _attn_a_body(q_ref, kp_ref, kc_ref, kn_ref, vp_ref, vc_ref, vn_ref, bias_ref, sink_ref, o_ref,
                 *, nsteps, lane0, bounded):
    n = pl.program_id(1)
    k = jnp.concatenate([kp_ref[0], kc_ref[0], kn_ref[0]], axis=0)
    v = jnp.concatenate([vp_ref[0], vc_ref[0], vn_ref[0]], axis=0)
    vext = jnp.concatenate([v, jnp.ones(v.shape, BF16)], axis=1)
    lane = lax.broadcasted_iota(jnp.int32, (BLOCK, LANES), 1)
    lo = lane < HEAD_DIM
    zero = jnp.zeros((BLOCK, LANES), BF16)
    for t in range(A_QBLOCKS):
        if t == 0:
            bias = bias_ref[jnp.where(n == 0, 0, 1)]
        elif t == A_QBLOCKS - 1:
            bias = bias_ref[jnp.where(n == nsteps - 1, 2, 1)]
        else:
            bias = bias_ref[1]
        kw = k[BLOCK * t:BLOCK * (t + 3)]
        vw = vext[BLOCK * t:BLOCK * (t + 3)]
        for blk in range(A_QW // LANES):
            qp = q_ref[0, BLOCK * t:BLOCK * (t + 1), LANES * blk:LANES * (blk + 1)]
            qm = jnp.concatenate([jnp.where(lo, qp, zero), jnp.where(lo, zero, qp)], axis=0)
            s = lax.dot_general(qm, kw, NT_DIMS, preferred_element_type=F32) + bias
            if bounded:
                p = jnp.exp2(s)
                sink_term = sink_ref[blk]
            else:
                sink = sink_ref[blk]
                m = jnp.maximum(jnp.max(s, axis=-1, keepdims=True), sink)
                p = jnp.exp2(s - m)
                sink_term = jnp.exp2(sink - m)
            oe = jnp.dot(p.astype(BF16), vw, preferred_element_type=F32)
            o = oe[:, :LANES] / (oe[:, LANES:] + sink_term)
            o_ref[0, BLOCK * t:BLOCK * (t + 1), lane0 + LANES * blk:lane0 + LANES * (blk + 1)] = (
                jnp.where(lo, o[:BLOCK], o[BLOCK:]).astype(BF16))


def _attn_b_body(q_ref, kp_ref, kc_ref, kn_ref, vp_ref, vc_ref, vn_ref, bias_ref, o_ref,
                 kbuf, vbuf, *, nrb, lane0, bounded):
    rb = pl.program_id(1)
    blk_tok = NA_ROWS * GRID_W
    half_tok = blk_tok // 2
    win_tok = NA_MAX_H * GRID_W
    kbuf[0:half_tok] = kp_ref[0, half_tok:blk_tok]
    kbuf[half_tok:half_tok + blk_tok] = kc_ref[0]
    kbuf[half_tok + blk_tok:2 * blk_tok] = kn_ref[0, 0:half_tok]
    vbuf[0:half_tok] = vp_ref[0, half_tok:blk_tok]
    vbuf[half_tok:half_tok + blk_tok] = vc_ref[0]
    vbuf[half_tok + blk_tok:2 * blk_tok] = vn_ref[0, 0:half_tok]
    lane = lax.broadcasted_iota(jnp.int32, (GRID_W, LANES), 1)
    lo = lane < HEAD_DIM
    zero = jnp.zeros((GRID_W, LANES), BF16)
    ones = jnp.ones((win_tok, LANES), BF16)
    mid = NA_MAX_H // 2
    for t in range(NA_ROWS):
        off = jnp.where(rb == 0, max(t, mid), jnp.where(rb == nrb - 1, min(t, mid), t))
        didx = t + mid - off
        kstart = pl.multiple_of(off * GRID_W, GRID_W)
        for blk in range(B_W // LANES):
            qp = q_ref[0, GRID_W * t:GRID_W * (t + 1), LANES * blk:LANES * (blk + 1)]
            qm = jnp.concatenate([jnp.where(lo, qp, zero), jnp.where(lo, zero, qp)], axis=0)
            kw = kbuf[pl.ds(kstart, win_tok), LANES * blk:LANES * (blk + 1)]
            vw = vbuf[pl.ds(kstart, win_tok), LANES * blk:LANES * (blk + 1)]
            s = lax.dot_general(qm, kw, NT_DIMS, preferred_element_type=F32)
            s = s + bias_ref[didx, blk]
            p = jnp.exp2(s) if bounded else jnp.exp2(s - jnp.max(s, axis=-1, keepdims=True))
            oe = jnp.dot(p.astype(BF16), jnp.concatenate([vw, ones], axis=1),
                         preferred_element_type=F32)
            o = oe[:, :LANES] / oe[:, LANES:]
            o_ref[0, GRID_W * t:GRID_W * (t + 1), lane0 + LANES * blk:lane0 + LANES * (blk + 1)] = (
                jnp.where(lo, o[:GRID_W], o[GRID_W:]).astype(BF16))


def _attn_ab_kernel(*refs, nsteps, bounded):
    a_refs, b_refs, o_ref, scratch = refs[:9], refs[9:17], refs[17], refs[18:]
    _attn_a_body(*a_refs, o_ref, nsteps=nsteps, lane0=0, bounded=bounded)
    _attn_b_body(*b_refs, o_ref, *scratch, nrb=nsteps, lane0=A_QW, bounded=bounded)


def _attn_ab(qa, ka, va, band_tab, sink_col, qb, kb, vb, bias_tab, bounded):
    bsz, seq, _ = qa.shape
    tok = A_QBLOCKS * BLOCK
    assert tok == NA_ROWS * GRID_W
    nsteps = seq // tok
    assert nsteps >= 2, "neighbourhood attention needs at least two row blocks"
    nb = seq // BLOCK
    cur = lambda b, n: (b, n, 0)
    a_edge = lambda f: pl.BlockSpec((1, BLOCK, A_KVW), f)
    a_prev = lambda b, n: (b, jnp.maximum(A_QBLOCKS * n - 1, 0), 0)
    a_next = lambda b, n: (b, jnp.minimum(A_QBLOCKS * (n + 1), nb - 1), 0)
    a_mid = pl.BlockSpec((1, tok, A_KVW), cur)
    b_blk = lambda f: pl.BlockSpec((1, tok, B_W), f)
    b_prev = lambda b, n: (b, jnp.maximum(n - 1, 0), 0)
    b_next = lambda b, n: (b, jnp.minimum(n + 1, nsteps - 1), 0)
    return pl.pallas_call(
        functools.partial(_attn_ab_kernel, nsteps=nsteps, bounded=bounded),
        grid=(bsz, nsteps),
        in_specs=[pl.BlockSpec((1, tok, A_QW), cur), a_edge(a_prev), a_mid, a_edge(a_next),
                  a_edge(a_prev), a_mid, a_edge(a_next), _resident(band_tab.shape),
                  _resident(sink_col.shape),
                  b_blk(cur), b_blk(b_prev), b_blk(cur), b_blk(b_next),
                  b_blk(b_prev), b_blk(cur), b_blk(b_next), _resident(bias_tab.shape)],
        out_specs=pl.BlockSpec((1, tok, A_QW + B_W), cur),
        out_shape=jax.ShapeDtypeStruct((bsz, seq, A_QW + B_W), BF16),
        scratch_shapes=[pltpu.VMEM((2 * tok, B_W), BF16), pltpu.VMEM((2 * tok, B_W), BF16)],
        compiler_params=_cparams("parallel", "parallel"),
        name="attn_ab_bounded" if bounded else "attn_ab",
    )(qa, ka, ka, ka, va, va, va, band_tab, sink_col, qb, kb, kb, kb, vb, vb, vb, bias_tab)


def _out_ffn_kernel(x_ref, o_ref, wo_ref, g_ref, wg_ref, wu_ref, wd_ref, y_ref):
    x1 = x_ref[0] + jnp.dot(o_ref[0], wo_ref[...], preferred_element_type=F32)
    h = _rms_rows(x1, g_ref[...]).astype(BF16)
    gate = jnp.dot(h, wg_ref[...], preferred_element_type=F32)
    up = jnp.dot(h, wu_ref[...], preferred_element_type=F32)
    act = (gate / (1.0 + jnp.exp(-gate)) * up).astype(BF16)
    y_ref[0] = x1 + jnp.dot(act, wd_ref[...], preferred_element_type=F32)


def _out_ffn(x, o, wo, g, wg, wu, wd):
    bsz, seq, _ = x.shape
    tm = TM_FFN
    tok = lambda width: pl.BlockSpec((1, tm, width), lambda b, i: (b, i, 0))
    consts = (wo, g, wg, wu, wd)
    return pl.pallas_call(
        _out_ffn_kernel,
        grid=(bsz, seq // tm),
        in_specs=[tok(D_MODEL), tok(o.shape[-1]), *[_resident(c.shape) for c in consts]],
        out_specs=tok(D_MODEL),
        out_shape=jax.ShapeDtypeStruct(x.shape, F32),
        compiler_params=_cparams("parallel", "parallel"),
        name="out_ffn",
    )(x, o, *consts)


def _proj_mla_kernel(shift_ref, x_ref, g_ref, win_ref, gq_ref, gkv_ref, wqt_ref, wvt_ref, wk_ref,
                     gqn_ref, gkn_ref, cosk_ref, sink_ref, cosq_ref, sinq_ref,
                     q_ref, k_ref, v_ref):
    tm = x_ref.shape[1]
    shift = shift_ref[0]
    h = _rms_rows(x_ref[0], g_ref[...]).astype(BF16)
    y = jnp.dot(h, win_ref[...], preferred_element_type=F32)
    cq = _rms_rows(y[:, :Q_LORA], gq_ref[...]).astype(BF16)
    ckv = _rms_rows(y[:, Q_LORA:Q_LORA + KV_LORA], gkv_ref[...]).astype(BF16)
    kpe = y[:, Q_LORA + KV_LORA:]

    qt = lax.dot_general(wqt_ref[...], cq, NT_DIMS, preferred_element_type=F32)
    qt = qt.reshape(C_HEADS, LANES, tm)[:, :C_QK]
    ssq = jnp.sum(qt * qt, axis=1, keepdims=True)
    qn = qt * lax.rsqrt(ssq * (1.0 / C_QK) + EPS) * gqn_ref[:C_QK][None]
    half = C_ROPE // 2
    r1 = qn[:, C_NOPE:C_NOPE + half]
    r2 = qn[:, C_NOPE + half:C_QK]
    cos = cosq_ref[...][None]
    sin = sinq_ref[...][None]
    pad_row = lax.broadcasted_iota(jnp.int32, (C_HEADS, LANES - C_QK, tm), 1)
    q_out = jnp.concatenate([qn[:, :C_NOPE], r1 * cos - r2 * sin, r2 * cos + r1 * sin,
                             jnp.where(pad_row == 0, -shift, 0.0)], axis=1)
    q_ref[0] = q_out.astype(BF16)

    vt = lax.dot_general(wvt_ref[...], ckv, NT_DIMS, preferred_element_type=F32)
    vt = vt.reshape(C_HEADS, C_VDIM, tm).astype(BF16)
    row = lax.broadcasted_iota(jnp.int32, (C_HEADS, V_ROWS - C_VDIM, tm), 1)
    v_ref[0, :, 0] = jnp.concatenate([vt, jnp.where(row == 0, 1.0, 0.0).astype(BF16)], axis=1)

    kn = jnp.dot(ckv, wk_ref[...], preferred_element_type=F32)
    gk = gkn_ref[...]
    ss_pe = jnp.sum(kpe * kpe, axis=-1, keepdims=True)
    kg = kpe * gk
    lane = lax.broadcasted_iota(jnp.int32, kg.shape, 1)
    swapped = jnp.where(lane < C_NOPE + half, pltpu.roll(kg, LANES - half, 1),
                        pltpu.roll(kg, half, 1))
    kr = kg * cosk_ref[...] + swapped * sink_ref[...]
    for hd in range(C_HEADS):
        kh = kn[:, LANES * hd:LANES * (hd + 1)]
        ss = jnp.sum(kh * kh, axis=-1, keepdims=True) + ss_pe
        kval = (kh * gk + kr) * lax.rsqrt(ss * (1.0 / C_QK) + EPS)
        k_ref[0, :, LANES * hd:LANES * (hd + 1)] = jnp.where(lane == C_QK, 1.0, kval).astype(BF16)


def _proj_mla(shift, x, g, win, gq, gkv, wqt, wvt, wk, gqn, gkn, cosk, sink, cosq, sinq):
    bsz, seq, _ = x.shape
    tm = TM_PROJ
    nt = seq // tm
    half = C_ROPE // 2
    return pl.pallas_call(
        _proj_mla_kernel,
        grid=(bsz, nt),
        in_specs=[pl.BlockSpec(memory_space=pltpu.SMEM),
                  pl.BlockSpec((1, tm, D_MODEL), lambda b, i: (b, i, 0)),
                  _resident(g.shape), _resident(win.shape), _resident(gq.shape),
                  _resident(gkv.shape), _resident(wqt.shape), _resident(wvt.shape),
                  _resident(wk.shape), _resident(gqn.shape), _resident(gkn.shape),
                  pl.BlockSpec((tm, LANES), lambda b, i: (i, 0)),
                  pl.BlockSpec((tm, LANES), lambda b, i: (i, 0)),
                  pl.BlockSpec((half, tm), lambda b, i: (0, i)),
                  pl.BlockSpec((half, tm), lambda b, i: (0, i))],
        out_specs=[pl.BlockSpec((1, C_HEADS, LANES, tm), lambda b, i: (b, 0, 0, i)),
                   pl.BlockSpec((1, tm, C_HEADS * LANES), lambda b, i: (b, i, 0)),
                   pl.BlockSpec((1, C_HEADS, 1, V_ROWS, tm), lambda b, i: (b, 0, i, 0, 0))],
        out_shape=[jax.ShapeDtypeStruct((bsz, C_HEADS, LANES, seq), BF16),
                   jax.ShapeDtypeStruct((bsz, seq, C_HEADS * LANES), BF16),
                   jax.ShapeDtypeStruct((bsz, C_HEADS, nt, V_ROWS, tm), BF16)],
        compiler_params=_cparams("parallel", "parallel"),
        name="proj_mla",
    )(shift, x, g, win, gq, gkv, wqt, wvt, wk, gqn, gkn, cosk, sink, cosq, sinq)


def _attn_mla_kernel(q_ref, k_ref, v_ref, o_ref, s_buf, cmax_buf, p_buf, alpha_buf, m_buf, acc_buf,
                     *, nchunks, tk):
    heads = range(2)

    def stage_a(c, slot):
        start = pl.multiple_of(c * tk, tk)
        for hh in heads:
            kc = k_ref[0, pl.ds(start, tk), LANES * hh:LANES * (hh + 1)]
            s = jnp.dot(kc, q_ref[0, hh], preferred_element_type=F32)
            s_buf[hh, slot] = s
            cmax_buf[hh, slot] = jnp.max(s, axis=0, keepdims=True)

    def stage_b(slot):
        for hh in heads:
            m_old = m_buf[hh]
            m_new = jnp.maximum(m_old, cmax_buf[hh, slot])
            alpha_buf[hh, slot] = jnp.exp2(m_old - m_new)
            m_buf[hh] = m_new
            p_buf[hh, slot] = jnp.exp2(s_buf[hh, slot] - m_new).astype(BF16)

    def stage_c(c, slot):
        for hh in heads:
            pv = jnp.dot(v_ref[0, hh, c], p_buf[hh, slot], preferred_element_type=F32)
            acc_buf[hh] = alpha_buf[hh, slot] * acc_buf[hh] + pv

    m_buf[...] = jnp.full(m_buf.shape, NEG_INF, F32)
    acc_buf[...] = jnp.zeros(acc_buf.shape, F32)
    p_buf[:, 1] = jnp.zeros(p_buf.shape[:1] + p_buf.shape[2:], BF16)
    alpha_buf[:, 1] = jnp.ones(alpha_buf.shape[:1] + alpha_buf.shape[2:], F32)
    stage_a(0, 0)

    def body(i, carry):
        c0 = 2 * i
        stage_b(0)
        stage_a(c0 + 1, 1)
        stage_c(jnp.maximum(c0 - 1, 0), 1)
        stage_b(1)
        stage_a(jnp.minimum(c0 + 2, nchunks - 1), 0)
        stage_c(c0, 0)
        return carry

    lax.fori_loop(0, nchunks // 2, body, 0)
    stage_c(nchunks - 1, 1)
    outs = [acc_buf[hh, :C_VDIM] / acc_buf[hh, C_VDIM:C_VDIM + 1] for hh in heads]
    o_ref[0] = jnp.concatenate(outs, axis=0).T.astype(BF16)


def _attn_mla_bounded_kernel(q_ref, k_ref, v_ref, o_ref, p_buf, acc_buf, *, nchunks, tk):
    heads = range(2)
    per_trip = math.gcd(nchunks, MLA_CHUNKS_PER_TRIP)
    assert per_trip % 2 == 0, "chunk c lives in buffer slot c % 2 of every trip"

    def stage_p(c, slot):
        start = pl.multiple_of(c * tk, tk)
        for hh in heads:
            kc = k_ref[0, pl.ds(start, tk), LANES * hh:LANES * (hh + 1)]
            s = jnp.dot(kc, q_ref[0, hh], preferred_element_type=F32)
            p_buf[hh, slot] = jnp.exp2(s).astype(BF16)

    def stage_c(c, slot):
        for hh in heads:
            acc_buf[hh] += jnp.dot(v_ref[0, hh, c], p_buf[hh, slot], preferred_element_type=F32)

    acc_buf[...] = jnp.zeros(acc_buf.shape, F32)
    stage_p(0, 0)

    def trip(c0, last):
        for j in range(per_trip):
            if not (last and j == per_trip - 1):
                stage_p(c0 + j + 1, (j + 1) % 2)
            stage_c(c0 + j, j % 2)

    def body(i, carry):
        trip(per_trip * i, last=False)
        return carry

    ntrips = nchunks // per_trip
    lax.fori_loop(0, ntrips - 1, body, 0)
    trip(per_trip * (ntrips - 1), last=True)
    outs = [acc_buf[hh, :C_VDIM] / acc_buf[hh, C_VDIM:C_VDIM + 1] for hh in heads]
    o_ref[0] = jnp.concatenate(outs, axis=0).T.astype(BF16)


def _attn_mla(qt, k, vt, bounded):
    bsz, _, _, seq = qt.shape
    nchunks, tk = vt.shape[2], vt.shape[4]
    assert nchunks % 2 == 0, "both pipelined loops handle key chunks in pairs"
    tq = TQ_MLA_BOUNDED if bounded else TQ_MLA
    if bounded:
        body = _attn_mla_bounded_kernel
        scratch = [pltpu.VMEM((2, 2, tk, tq), BF16),
                   pltpu.VMEM((2, V_ROWS, tq), F32)]
    else:
        body = _attn_mla_kernel
        scratch = [pltpu.VMEM((2, 2, tk, tq), F32),
                   pltpu.VMEM((2, 2, 1, tq), F32),
                   pltpu.VMEM((2, 2, tk, tq), BF16),
                   pltpu.VMEM((2, 2, 1, tq), F32),
                   pltpu.VMEM((2, 1, tq), F32),
                   pltpu.VMEM((2, V_ROWS, tq), F32)]
    return pl.pallas_call(
        functools.partial(body, nchunks=nchunks, tk=tk),
        grid=(bsz, C_HEADS // 2, seq // tq),
        in_specs=[pl.BlockSpec((1, 2, LANES, tq), lambda b, h, i: (b, h, 0, i)),
                  pl.BlockSpec((1, seq, 2 * LANES), lambda b, h, i: (b, 0, h)),
                  pl.BlockSpec((1, 2, nchunks, V_ROWS, tk), lambda b, h, i: (b, h, 0, 0, 0))],
        out_specs=pl.BlockSpec((1, tq, 2 * C_VDIM), lambda b, h, i: (b, i, h)),
        out_shape=jax.ShapeDtypeStruct((bsz, seq, C_HEADS * C_VDIM), BF16),
        scratch_shapes=scratch,
        compiler_params=_cparams("parallel", "parallel", "arbitrary"),
        name="attn_mla_bounded" if bounded else "attn_mla",
    )(qt, k, vt)


def _rope_angles(seq, half):
    inv_freq = ROPE_THETA ** (-jnp.arange(half, dtype=F32) / half)
    return jnp.arange(seq, dtype=F32)[:, None] * inv_freq[None, :]


def _prep_ab(ab_w_in, ab_w_out, a_q_norm, a_k_norm, a_sink, b_q_norm, b_k_norm, b_rpb, seq):
    o_ka, o_va, o_qb = A_QW, A_QW + A_KVW, A_QW + 2 * A_KVW
    o_kb, o_vb = o_qb + B_W, o_qb + 2 * B_W
    w_in = jnp.concatenate(
        [ab_w_in[:, HEAD_DIM * h:HEAD_DIM * (h + 1)] for h in PERM_A]
        + [ab_w_in[:, o_qb:o_vb], ab_w_in[:, o_ka:o_qb], ab_w_in[:, o_vb:]], axis=1).astype(BF16)
    w_out = jnp.concatenate([ab_w_out[HEAD_DIM * h:HEAD_DIM * (h + 1)] for h in PERM_A]
                            + [ab_w_out[A_QW:]]).astype(BF16)
    scale = HEAD_DIM ** -0.5 * LOG2E
    gain = jnp.concatenate([jnp.tile(a_q_norm * scale, A_HEADS), jnp.tile(b_q_norm * scale, B_HEADS),
                            jnp.tile(b_k_norm, B_HEADS), jnp.tile(a_k_norm, A_KV_HEADS),
                            jnp.ones((A_KVW,), F32)])[None, :].astype(F32)
    idx = np.arange(2 * LANES) // HEAD_DIM
    gsum = jnp.asarray(idx[:, None] == idx[None, :], BF16)
    ang = _rope_angles(seq, HEAD_DIM // 2)
    cos = jnp.tile(jnp.cos(ang), (1, LANES // (HEAD_DIM // 2)))
    sin = jnp.tile(jnp.concatenate([-jnp.sin(ang), jnp.sin(ang)], axis=1), (1, LANES // HEAD_DIM))
    sink_col = jnp.repeat(a_sink[np.array(PERM_A)] * LOG2E, BLOCK).reshape(
        A_QW // LANES, 2 * BLOCK, 1).astype(F32)
    i = np.arange(2 * BLOCK)[:, None] % BLOCK
    j = np.arange(3 * BLOCK)[None, :]
    band = (j - i >= BLOCK - WINDOW) & (j - i <= BLOCK + WINDOW)
    band = np.stack([band & (j >= BLOCK), band, band & (j < 2 * BLOCK)])
    band_tab = jnp.asarray(np.where(band, 0.0, NEG_INF), F32)

    c = np.arange(GRID_W)
    cs = np.clip(c - NA_W // 2, 0, GRID_W - NA_W)
    inwin = (c[None, :] >= cs[:, None]) & (c[None, :] < cs[:, None] + NA_W)
    dc = c[None, :] - c[:, None] + NA_W - 1
    pick = jnp.asarray(dc[:, :, None] == np.arange(2 * NA_W - 1), F32)
    tcol = jnp.sum(b_rpb.astype(F32)[:, :, None, None, :] * pick[None, None], axis=-1)
    tab = jnp.stack([jnp.stack([tcol[:, NA_MAX_H - 1 - d + w] for w in range(NA_MAX_H)], axis=2)
                     for d in range(NA_MAX_H)], axis=0)
    tab = jnp.where(inwin[None, None, :, None, :], tab * LOG2E, NEG_INF)
    tab = tab.reshape(NA_MAX_H, B_HEADS // 2, 2 * GRID_W, NA_MAX_H * GRID_W)
    return w_in, w_out, gain, gsum, cos, sin, band_tab, sink_col, tab


def _bounded_tables(a_q_norm, a_k_norm, a_sink, b_q_norm, b_k_norm, b_rpb, band_tab, sink_col, bias_tab):
    scale = HEAD_DIM ** -0.5 * LOG2E
    bound_a = BF16_NORM_MARGIN * HEAD_DIM * jnp.max(jnp.abs(a_q_norm * scale)) * jnp.max(jnp.abs(a_k_norm))
    shift_a = jnp.maximum(bound_a, jnp.max(a_sink) * LOG2E)
    ok_a = bound_a + shift_a <= 2 * SHIFT_MAX
    bound_b = BF16_NORM_MARGIN * HEAD_DIM * jnp.max(jnp.abs(b_q_norm * scale)) * jnp.max(jnp.abs(b_k_norm))
    rpb = b_rpb.astype(F32) * LOG2E
    shift_b = bound_b + jnp.max(rpb)
    ok_b = bound_b + shift_b - jnp.min(rpb) <= 2 * SHIFT_MAX
    sink_term = jnp.broadcast_to(jnp.exp2(sink_col - shift_a), sink_col.shape[:2] + (LANES,))
    return ok_a & ok_b, band_tab - shift_a, sink_term, bias_tab - shift_b


def _prep_mla(c_w_in, c_q_lora_norm, c_kv_lora_norm, c_w_q_up, c_w_kv_up, c_q_norm, c_k_norm,
              c_w_out, seq):
    win = jnp.zeros((D_MODEL, Q_LORA + KV_LORA + LANES), F32)
    win = win.at[:, :Q_LORA + KV_LORA].set(c_w_in[:, :Q_LORA + KV_LORA])
    win = win.at[:, Q_LORA + KV_LORA + C_NOPE:Q_LORA + KV_LORA + C_QK].set(c_w_in[:, Q_LORA + KV_LORA:])
    wq = c_w_q_up.reshape(Q_LORA, C_HEADS, C_QK)
    wq = jnp.pad(wq, ((0, 0), (0, 0), (0, LANES - C_QK)))
    wqt = wq.reshape(Q_LORA, C_HEADS * LANES).T.astype(BF16)
    wkv = c_w_kv_up.reshape(KV_LORA, C_HEADS, C_NOPE + C_VDIM)
    wvt = wkv[:, :, C_NOPE:].reshape(KV_LORA, C_HEADS * C_VDIM).T.astype(BF16)
    wk = jnp.pad(wkv[:, :, :C_NOPE], ((0, 0), (0, 0), (0, LANES - C_NOPE)))
    wk = wk.reshape(KV_LORA, C_HEADS * LANES).astype(BF16)
    qscale = C_QK ** -0.5 * LOG2E
    shift = (BF16_NORM_MARGIN * C_QK * jnp.max(jnp.abs(c_q_norm * qscale)) * jnp.max(jnp.abs(c_k_norm)))
    shift = shift.reshape(1).astype(F32)
    gqn = jnp.pad(c_q_norm * qscale, (0, LANES - C_QK))[:, None].astype(F32)
    gkn = jnp.pad(c_k_norm, (0, LANES - C_QK))[None, :].astype(F32)
    half = C_ROPE // 2
    ang = _rope_angles(seq, half)
    cos, sin = jnp.cos(ang), jnp.sin(ang)
    zl = jnp.zeros((seq, C_NOPE), F32)
    zr = jnp.zeros((seq, LANES - C_QK), F32)
    cosk = jnp.concatenate([zl, cos, cos, zr], axis=1)
    sink = jnp.concatenate([zl, -sin, sin, zr], axis=1)
    return (shift, win.astype(BF16), c_q_lora_norm[None, :].astype(F32),
            c_kv_lora_norm[None, :].astype(F32), wqt, wvt, wk, gqn, gkn, cosk, sink, cos.T, sin.T,
            c_w_out.astype(BF16))


def _trunk(x, p):
    seq = x.shape[1]
    w_in, w_out, gain, gsum, cos, sin, band_tab, sink_col, bias_tab = _prep_ab(
        p["ab_w_in"][0], p["ab_w_out"][0], p["a_q_norm"][0], p["a_k_norm"][0], p["a_sink"][0],
        p["b_q_norm"][0], p["b_k_norm"][0], p["b_rpb"][0], seq)
    qa, ka, va, qb, kb, vb = _proj_ab(x, p["norm_mix"][0][None, :], w_in, gsum, gain, cos, sin)
    ok, band_sh, sink_term, bias_sh = _bounded_tables(
        p["a_q_norm"][0], p["a_k_norm"][0], p["a_sink"][0], p["b_q_norm"][0], p["b_k_norm"][0],
        p["b_rpb"][0], band_tab, sink_col, bias_tab)
    o = lax.cond(
        ok,
        lambda: _attn_ab(qa, ka, va, band_sh, sink_term, qb, kb, vb, bias_sh, bounded=True),
        lambda: _attn_ab(qa, ka, va, band_tab, sink_col, qb, kb, vb, bias_tab, bounded=False))
    x = _out_ffn(x, o, w_out, p["norm_ffn"][0][None, :],
                 p["ffn_w_gate"][0].astype(BF16), p["ffn_w_up"][0].astype(BF16),
                 p["ffn_w_down"][0].astype(BF16))
    (shift, win, gq, gkv, wqt, wvt, wk, gqn, gkn, cosk, sink, cosq, sinq, wo) = _prep_mla(
        p["c_w_in"][0], p["c_q_lora_norm"][0], p["c_kv_lora_norm"][0], p["c_w_q_up"][0],
        p["c_w_kv_up"][0], p["c_q_norm"][0], p["c_k_norm"][0], p["c_w_out"][0], seq)
    qt, k, vt = _proj_mla(shift, x, p["norm_mix"][1][None, :], win, gq, gkv, wqt, wvt, wk, gqn, gkn,
                          cosk, sink, cosq, sinq)
    o = lax.cond(shift[0] <= SHIFT_MAX,
                 functools.partial(_attn_mla, bounded=True),
                 functools.partial(_attn_mla, bounded=False), qt, k, vt)
    x = _out_ffn(x, o, wo, p["norm_ffn"][1][None, :], p["ffn_w_gate"][1].astype(BF16),
                 p["ffn_w_up"][1].astype(BF16), p["ffn_w_down"][1].astype(BF16))
    return x


def kernel(x_prompt, x_sample, norm_mix, norm_ffn, ab_w_in, ab_w_out, a_q_norm, a_k_norm, a_sink,
           b_q_norm, b_k_norm, b_rpb, c_w_in, c_q_lora_norm, c_kv_lora_norm, c_w_q_up, c_w_kv_up,
           c_q_norm, c_k_norm, c_w_out, ffn_w_gate, ffn_w_up, ffn_w_down):
    p = dict(norm_mix=norm_mix, norm_ffn=norm_ffn, ab_w_in=ab_w_in, ab_w_out=ab_w_out,
             a_q_norm=a_q_norm, a_k_norm=a_k_norm, a_sink=a_sink, b_q_norm=b_q_norm,
             b_k_norm=b_k_norm, b_rpb=b_rpb, c_w_in=c_w_in, c_q_lora_norm=c_q_lora_norm,
             c_kv_lora_norm=c_kv_lora_norm, c_w_q_up=c_w_q_up, c_w_kv_up=c_w_kv_up,
             c_q_norm=c_q_norm, c_k_norm=c_k_norm, c_w_out=c_w_out, ffn_w_gate=ffn_w_gate,
             ffn_w_up=ffn_w_up, ffn_w_down=ffn_w_down)
    return _trunk(x_prompt, p), _trunk(x_sample, p)
```

```python
import functools
import math

import numpy as np
import jax
import jax.numpy as jnp
from jax import lax
from jax.experimental import pallas as pl
from jax.experimental.pallas import tpu as pltpu

F32 = jnp.float32
BF16 = jnp.bfloat16

D_MODEL = 1024
GRID_W = 64
HEAD_DIM = 64
ROPE_THETA = 10000.0
EPS = 1e-6
NEG_INF = -1e30
BLOCK = 128
A_HEADS = 8
A_KV_HEADS = 2
WINDOW = 128
B_HEADS = 8
NA_MAX_H = 8
NA_W = 16
C_HEADS = 16
C_NOPE = 64
C_ROPE = 32
C_VDIM = 64
C_QK = C_NOPE + C_ROPE
Q_LORA = 384
KV_LORA = 256
A_QW = A_HEADS * HEAD_DIM
A_KVW = A_KV_HEADS * HEAD_DIM
B_W = B_HEADS * HEAD_DIM

LANES = 128
V7X_VMEM_BYTES = 64 * 1024 * 1024
VMEM_LIMIT = V7X_VMEM_BYTES * 3 // 4

TM_PROJ = 512
TM_FFN = 512
TQ_MLA = 1024
TQ_MLA_BOUNDED = 2048
MLA_CHUNKS_PER_TRIP = 8
NA_ROWS = 8
A_QBLOCKS = 4
V_ROWS = 2 * C_VDIM

PERM_A = (0, 4, 1, 5, 2, 6, 3, 7)

LOG2E = math.log2(math.e)
SHIFT_MAX = 60.0
BF16_NORM_MARGIN = 1.0 + 2.0 ** -6
NT_DIMS = (((1,), (1,)), ((), ()))


def _cparams(*sem):
    return pltpu.CompilerParams(dimension_semantics=sem, vmem_limit_bytes=VMEM_LIMIT)


def _resident(shape):
    nd = len(shape)
    return pl.BlockSpec(shape, lambda *_: (0,) * nd, pipeline_mode=pl.Buffered(1))


def _rms_rows(x, gain):
    ms = jnp.mean(x * x, axis=-1, keepdims=True)
    return x * lax.rsqrt(ms + EPS) * gain


def _proj_ab_kernel(x_ref, g_ref, w_ref, gsum_ref, gain_ref, cos_ref, sin_ref,
                    qa_ref, ka_ref, va_ref, qb_ref, kb_ref, vb_ref):
    h = _rms_rows(x_ref[0], g_ref[...]).astype(BF16)
    y = jnp.dot(h, w_ref[...], preferred_element_type=F32)
    gsum = gsum_ref[...]
    cos = cos_ref[...]
    sin = sin_ref[...]
    lane = lax.broadcasted_iota(jnp.int32, cos.shape, 1)
    first_half = (lane % HEAD_DIM) < (HEAD_DIM // 2)

    def head_norm(c):
        yc = y[:, 2 * LANES * c:2 * LANES * (c + 1)]
        ss = jnp.dot((yc * yc).astype(BF16), gsum, preferred_element_type=F32)
        return yc * lax.rsqrt(ss * (1.0 / HEAD_DIM) + EPS) * gain_ref[:, 2 * LANES * c:2 * LANES * (c + 1)]

    def rope(v):
        swapped = jnp.where(first_half, pltpu.roll(v, LANES - HEAD_DIM // 2, 1),
                            pltpu.roll(v, HEAD_DIM // 2, 1))
        return v * cos + swapped * sin

    for c in range(2):
        yn = head_norm(c)
        for b in range(2):
            qa_ref[0, :, LANES * (2 * c + b):LANES * (2 * c + b + 1)] = rope(
                yn[:, LANES * b:LANES * (b + 1)]).astype(BF16)
    for c in range(2):
        qb_ref[0, :, 2 * LANES * c:2 * LANES * (c + 1)] = head_norm(2 + c).astype(BF16)
    for c in range(2):
        kb_ref[0, :, 2 * LANES * c:2 * LANES * (c + 1)] = head_norm(4 + c).astype(BF16)
    ka_ref[0] = rope(head_norm(6)[:, :LANES]).astype(BF16)
    va_ref[0] = y[:, 13 * LANES:14 * LANES].astype(BF16)
    vb_ref[0] = y[:, 14 * LANES:18 * LANES].astype(BF16)


def _proj_ab(x, g, w, gsum, gain, cos, sin):
    bsz, seq, _ = x.shape
    tm = TM_PROJ
    nt = seq // tm
    tok = lambda width: pl.BlockSpec((1, tm, width), lambda b, i: (b, i, 0))
    tab = pl.BlockSpec((tm, LANES), lambda b, i: (i, 0))
    out = lambda width: jax.ShapeDtypeStruct((bsz, seq, width), BF16)
    return pl.pallas_call(
        _proj_ab_kernel,
        grid=(bsz, nt),
        in_specs=[tok(D_MODEL), _resident(g.shape), _resident(w.shape), _resident(gsum.shape),
                  _resident(gain.shape), tab, tab],
        out_specs=[tok(A_QW), tok(A_KVW), tok(A_KVW), tok(B_W), tok(B_W), tok(B_W)],
        out_shape=[out(A_QW), out(A_KVW), out(A_KVW), out(B_W), out(B_W), out(B_W)],
        compiler_params=_cparams("parallel", "parallel"),
        name="proj_ab",
    )(x, g, w, gsum, gain, cos, sin)


def _attn_a_body(q_ref, kp_ref, kc_ref, kn_ref, vp_ref, vc_ref, vn_ref, bias_ref, sink_ref, o_ref,
                 *, nsteps, lane0, bounded):
    n = pl.program_id(1)
    k = jnp.concatenate([kp_ref[0], kc_ref[0], kn_ref[0]], axis=0)
    v = jnp.concatenate([vp_ref[0], vc_ref[0], vn_ref[0]], axis=0)
    vext = jnp.concatenate([v, jnp.ones(v.shape, BF16)], axis=1)
    lane = lax.broadcasted_iota(jnp.int32, (BLOCK, LANES), 1)
    lo = lane < HEAD_DIM
    zero = jnp.zeros((BLOCK, LANES), BF16)
    for t in range(A_QBLOCKS):
        if t == 0:
            bias = bias_ref[jnp.where(n == 0, 0, 1)]
        elif t == A_QBLOCKS - 1:
            bias = bias_ref[jnp.where(n == nsteps - 1, 2, 1)]
        else:
            bias = bias_ref[1]
        kw = k[BLOCK * t:BLOCK * (t + 3)]
        vw = vext[BLOCK * t:BLOCK * (t + 3)]
        for blk in range(A_QW // LANES):
            qp = q_ref[0, BLOCK * t:BLOCK * (t + 1), LANES * blk:LANES * (blk + 1)]
            qm = jnp.concatenate([jnp.where(lo, qp, zero), jnp.where(lo, zero, qp)], axis=0)
            s = lax.dot_general(qm, kw, NT_DIMS, preferred_element_type=F32) + bias
            if bounded:
                p = jnp.exp2(s)
                sink_term = sink_ref[blk]
            else:
                sink = sink_ref[blk]
                m = jnp.maximum(jnp.max(s, axis=-1, keepdims=True), sink)
                p = jnp.exp2(s - m)
                sink_term = jnp.exp2(sink - m)
            oe = jnp.dot(p.astype(BF16), vw, preferred_element_type=F32)
            o = oe[:, :LANES] / (oe[:, LANES:] + sink_term)
            o_ref[0, BLOCK * t:BLOCK * (t + 1), lane0 + LANES * blk:lane0 + LANES * (blk + 1)] = (
                jnp.where(lo, o[:BLOCK], o[BLOCK:]).astype(BF16))


def _attn_b_body(q_ref, kp_ref, kc_ref, kn_ref, vp_ref, vc_ref, vn_ref, bias_ref, o_ref,
                 kbuf, vbuf, *, nrb, lane0, bounded):
    rb = pl.program_id(1)
    blk_tok = NA_ROWS * GRID_W
    half_tok = blk_tok // 2
    win_tok = NA_MAX_H * GRID_W
    kbuf[0:half_tok] = kp_ref[0, half_tok:blk_tok]
    kbuf[half_tok:half_tok + blk_tok] = kc_ref[0]
    kbuf[half_tok + blk_tok:2 * blk_tok] = kn_ref[0, 0:half_tok]
    vbuf[0:half_tok] = vp_ref[0, half_tok:blk_tok]
    vbuf[half_tok:half_tok + blk_tok] = vc_ref[0]
    vbuf[half_tok + blk_tok:2 * blk_tok] = vn_ref[0, 0:half_tok]
    lane = lax.broadcasted_iota(jnp.int32, (GRID_W, LANES), 1)
    lo = lane < HEAD_DIM
    zero = jnp.zeros((GRID_W, LANES), BF16)
    ones = jnp.ones((win_tok, LANES), BF16)
    mid = NA_MAX_H // 2
    for t in range(NA_ROWS):
        off = jnp.where(rb == 0, max(t, mid), jnp.where(rb == nrb - 1, min(t, mid), t))
        didx = t + mid - off
        kstart = pl.multiple_of(off * GRID_W, GRID_W)
        for blk in range(B_W // LANES):
            qp = q_ref[0, GRID_W * t:GRID_W * (t + 1), LANES * blk:LANES * (blk + 1)]
            qm = jnp.concatenate([jnp.where(lo, qp, zero), jnp.where(lo, zero, qp)], axis=0)
            kw = kbuf[pl.ds(kstart, win_tok), LANES * blk:LANES * (blk + 1)]
            vw = vbuf[pl.ds(kstart, win_tok), LANES * blk:LANES * (blk + 1)]
            s = lax.dot_general(qm, kw, NT_DIMS, preferred_element_type=F32)
            s = s + bias_ref[didx, blk]
            p = jnp.exp2(s) if bounded else jnp.exp2(s - jnp.max(s, axis=-1, keepdims=True))
            oe = jnp.dot(p.astype(BF16), jnp.concatenate([vw, ones], axis=1),
                         preferred_element_type=F32)
            o = oe[:, :LANES] / oe[:, LANES:]
            o_ref[0, GRID_W * t:GRID_W * (t + 1), lane0 + LANES * blk:lane0 + LANES * (blk + 1)] = (
                jnp.where(lo, o[:GRID_W], o[GRID_W:]).astype(BF16))


def _attn_ab_kernel(*refs, nsteps, bounded):
    a_refs, b_refs, o_ref, scratch = refs[:9], refs[9:17], refs[17], refs[18:]
    _attn_a_body(*a_refs, o_ref, nsteps=nsteps, lane0=0, bounded=bounded)
    _attn_b_body(*b_refs, o_ref, *scratch, nrb=nsteps, lane0=A_QW, bounded=bounded)


def _attn_ab(qa, ka, va, band_tab, sink_col, qb, kb, vb, bias_tab, bounded):
    bsz, seq, _ = qa.shape
    tok = A_QBLOCKS * BLOCK
    assert tok == NA_ROWS * GRID_W
    nsteps = seq // tok
    assert nsteps >= 2, "neighbourhood attention needs at least two row blocks"
    nb = seq // BLOCK
    cur = lambda b, n: (b, n, 0)
    a_edge = lambda f: pl.BlockSpec((1, BLOCK, A_KVW), f)
    a_prev = lambda b, n: (b, jnp.maximum(A_QBLOCKS * n - 1, 0), 0)
    a_next = lambda b, n: (b, jnp.minimum(A_QBLOCKS * (n + 1), nb - 1), 0)
    a_mid = pl.BlockSpec((1, tok, A_KVW), cur)
    b_blk = lambda f: pl.BlockSpec((1, tok, B_W), f)
    b_prev = lambda b, n: (b, jnp.maximum(n - 1, 0), 0)
    b_next = lambda b, n: (b, jnp.minimum(n + 1, nsteps - 1), 0)
    return pl.pallas_call(
        functools.partial(_attn_ab_kernel, nsteps=nsteps, bounded=bounded),
        grid=(bsz, nsteps),
        in_specs=[pl.BlockSpec((1, tok, A_QW), cur), a_edge(a_prev), a_mid, a_edge(a_next),
                  a_edge(a_prev), a_mid, a_edge(a_next), _resident(band_tab.shape),
                  _resident(sink_col.shape),
                  b_blk(cur), b_blk(b_prev), b_blk(cur), b_blk(b_next),
                  b_blk(b_prev), b_blk(cur), b_blk(b_next), _resident(bias_tab.shape)],
        out_specs=pl.BlockSpec((1, tok, A_QW + B_W), cur),
        out_shape=jax.ShapeDtypeStruct((bsz, seq, A_QW + B_W), BF16),
        scratch_shapes=[pltpu.VMEM((2 * tok, B_W), BF16), pltpu.VMEM((2 * tok, B_W), BF16)],
        compiler_params=_cparams("parallel", "parallel"),
        name="attn_ab_bounded" if bounded else "attn_ab",
    )(qa, ka, ka, ka, va, va, va, band_tab, sink_col, qb, kb, kb, kb, vb, vb, vb, bias_tab)


def _out_ffn_kernel(x_ref, o_ref, wo_ref, g_ref, wg_ref, wu_ref, wd_ref, y_ref):
    x1 = x_ref[0] + jnp.dot(o_ref[0], wo_ref[...], preferred_element_type=F32)
    h = _rms_rows(x1, g_ref[...]).astype(BF16)
    gate = jnp.dot(h, wg_ref[...], preferred_element_type=F32)
    up = jnp.dot(h, wu_ref[...], preferred_element_type=F32)
    act = (gate / (1.0 + jnp.exp(-gate)) * up).astype(BF16)
    y_ref[0] = x1 + jnp.dot(act, wd_ref[...], preferred_element_type=F32)


def _out_ffn(x, o, wo, g, wg, wu, wd):
    bsz, seq, _ = x.shape
    tm = TM_FFN
    tok = lambda width: pl.BlockSpec((1, tm, width), lambda b, i: (b, i, 0))
    consts = (wo, g, wg, wu, wd)
    return pl.pallas_call(
        _out_ffn_kernel,
        grid=(bsz, seq // tm),
        in_specs=[tok(D_MODEL), tok(o.shape[-1]), *[_resident(c.shape) for c in consts]],
        out_specs=tok(D_MODEL),
        out_shape=jax.ShapeDtypeStruct(x.shape, F32),
        compiler_params=_cparams("parallel", "parallel"),
        name="out_ffn",
    )(x, o, *consts)


def _proj_mla_kernel(shift_ref, x_ref, g_ref, win_ref, gq_ref, gkv_ref, wqt_ref, wvt_ref, wk_ref,
                     gqn_ref, gkn_ref, cosk_ref, sink_ref, cosq_ref, sinq_ref,
                     q_ref, k_ref, v_ref):
    tm = x_ref.shape[1]
    shift = shift_ref[0]
    h = _rms_rows(x_ref[0], g_ref[...]).astype(BF16)
    y = jnp.dot(h, win_ref[...], preferred_element_type=F32)
    cq = _rms_rows(y[:, :Q_LORA], gq_ref[...]).astype(BF16)
    ckv = _rms_rows(y[:, Q_LORA:Q_LORA + KV_LORA], gkv_ref[...]).astype(BF16)
    kpe = y[:, Q_LORA + KV_LORA:]

    qt = lax.dot_general(wqt_ref[...], cq, NT_DIMS, preferred_element_type=F32)
    qt = qt.reshape(C_HEADS, LANES, tm)[:, :C_QK]
    ssq = jnp.sum(qt * qt, axis=1, keepdims=True)
    qn = qt * lax.rsqrt(ssq * (1.0 / C_QK) + EPS) * gqn_ref[:C_QK][None]
    half = C_ROPE // 2
    r1 = qn[:, C_NOPE:C_NOPE + half]
    r2 = qn[:, C_NOPE + half:C_QK]
    cos = cosq_ref[...][None]
    sin = sinq_ref[...][None]
    pad_row = lax.broadcasted_iota(jnp.int32, (C_HEADS, LANES - C_QK, tm), 1)
    q_out = jnp.concatenate([qn[:, :C_NOPE], r1 * cos - r2 * sin, r2 * cos + r1 * sin,
                             jnp.where(pad_row == 0, -shift, 0.0)], axis=1)
    q_ref[0] = q_out.astype(BF16)

    vt = lax.dot_general(wvt_ref[...], ckv, NT_DIMS, preferred_element_type=F32)
    vt = vt.reshape(C_HEADS, C_VDIM, tm).astype(BF16)
    row = lax.broadcasted_iota(jnp.int32, (C_HEADS, V_ROWS - C_VDIM, tm), 1)
    v_ref[0, :, 0] = jnp.concatenate([vt, jnp.where(row == 0, 1.0, 0.0).astype(BF16)], axis=1)

    kn = jnp.dot(ckv, wk_ref[...], preferred_element_type=F32)
    gk = gkn_ref[...]
    ss_pe = jnp.sum(kpe * kpe, axis=-1, keepdims=True)
    kg = kpe * gk
    lane = lax.broadcasted_iota(jnp.int32, kg.shape, 1)
    swapped = jnp.where(lane < C_NOPE + half, pltpu.roll(kg, LANES - half, 1),
                        pltpu.roll(kg, half, 1))
    kr = kg * cosk_ref[...] + swapped * sink_ref[...]
    for hd in range(C_HEADS):
        kh = kn[:, LANES * hd:LANES * (hd + 1)]
        ss = jnp.sum(kh * kh, axis=-1, keepdims=True) + ss_pe
        kval = (kh * gk + kr) * lax.rsqrt(ss * (1.0 / C_QK) + EPS)
        k_ref[0, :, LANES * hd:LANES * (hd + 1)] = jnp.where(lane == C_QK, 1.0, kval).astype(BF16)


def _proj_mla(shift, x, g, win, gq, gkv, wqt, wvt, wk, gqn, gkn, cosk, sink, cosq, sinq):
    bsz, seq, _ = x.shape
    tm = TM_PROJ
    nt = seq // tm
    half = C_ROPE // 2
    return pl.pallas_call(
        _proj_mla_kernel,
        grid=(bsz, nt),
        in_specs=[pl.BlockSpec(memory_space=pltpu.SMEM),
                  pl.BlockSpec((1, tm, D_MODEL), lambda b, i: (b, i, 0)),
                  _resident(g.shape), _resident(win.shape), _resident(gq.shape),
                  _resident(gkv.shape), _resident(wqt.shape), _resident(wvt.shape),
                  _resident(wk.shape), _resident(gqn.shape), _resident(gkn.shape),
                  pl.BlockSpec((tm, LANES), lambda b, i: (i, 0)),
                  pl.BlockSpec((tm, LANES), lambda b, i: (i, 0)),
                  pl.BlockSpec((half, tm), lambda b, i: (0, i)),
                  pl.BlockSpec((half, tm), lambda b, i: (0, i))],
        out_specs=[pl.BlockSpec((1, C_HEADS, LANES, tm), lambda b, i: (b, 0, 0, i)),
                   pl.BlockSpec((1, tm, C_HEADS * LANES), lambda b, i: (b, i, 0)),
                   pl.BlockSpec((1, C_HEADS, 1, V_ROWS, tm), lambda b, i: (b, 0, i, 0, 0))],
        out_shape=[jax.ShapeDtypeStruct((bsz, C_HEADS, LANES, seq), BF16),
                   jax.ShapeDtypeStruct((bsz, seq, C_HEADS * LANES), BF16),
                   jax.ShapeDtypeStruct((bsz, C_HEADS, nt, V_ROWS, tm), BF16)],
        compiler_params=_cparams("parallel", "parallel"),
        name="proj_mla",
    )(shift, x, g, win, gq, gkv, wqt, wvt, wk, gqn, gkn, cosk, sink, cosq, sinq)


def _attn_mla_kernel(q_ref, k_ref, v_ref, o_ref, s_buf, cmax_buf, p_buf, alpha_buf, m_buf, acc_buf,
                     *, nchunks, tk):
    heads = range(2)

    def stage_a(c, slot):
        start = pl.multiple_of(c * tk, tk)
        for hh in heads:
            kc = k_ref[0, pl.ds(start, tk), LANES * hh:LANES * (hh + 1)]
            s = jnp.dot(kc, q_ref[0, hh], preferred_element_type=F32)
            s_buf[hh, slot] = s
            cmax_buf[hh, slot] = jnp.max(s, axis=0, keepdims=True)

    def stage_b(slot):
        for hh in heads:
            m_old = m_buf[hh]
            m_new = jnp.maximum(m_old, cmax_buf[hh, slot])
            alpha_buf[hh, slot] = jnp.exp2(m_old - m_new)
            m_buf[hh] = m_new
            p_buf[hh, slot] = jnp.exp2(s_buf[hh, slot] - m_new).astype(BF16)

    def stage_c(c, slot):
        for hh in heads:
            pv = jnp.dot(v_ref[0, hh, c], p_buf[hh, slot], preferred_element_type=F32)
            acc_buf[hh] = alpha_buf[hh, slot] * acc_buf[hh] + pv

    m_buf[...] = jnp.full(m_buf.shape, NEG_INF, F32)
    acc_buf[...] = jnp.zeros(acc_buf.shape, F32)
    p_buf[:, 1] = jnp.zeros(p_buf.shape[:1] + p_buf.shape[2:], BF16)
    alpha_buf[:, 1] = jnp.ones(alpha_buf.shape[:1] + alpha_buf.shape[2:], F32)
    stage_a(0, 0)

    def body(i, carry):
        c0 = 2 * i
        stage_b(0)
        stage_a(c0 + 1, 1)
        stage_c(jnp.maximum(c0 - 1, 0), 1)
        stage_b(1)
        stage_a(jnp.minimum(c0 + 2, nchunks - 1), 0)
        stage_c(c0, 0)
        return carry

    lax.fori_loop(0, nchunks // 2, body, 0)
    stage_c(nchunks - 1, 1)
    outs = [acc_buf[hh, :C_VDIM] / acc_buf[hh, C_VDIM:C_VDIM + 1] for hh in heads]
    o_ref[0] = jnp.concatenate(outs, axis=0).T.astype(BF16)


def _attn_mla_bounded_kernel(q_ref, k_ref, v_ref, o_ref, p_buf, acc_buf, *, nchunks, tk):
    heads = range(2)
    per_trip = math.gcd(nchunks, MLA_CHUNKS_PER_TRIP)
    assert per_trip % 2 == 0, "chunk c lives in buffer slot c % 2 of every trip"

    def stage_p(c, slot):
        start = pl.multiple_of(c * tk, tk)
        for hh in heads:
            kc = k_ref[0, pl.ds(start, tk), LANES * hh:LANES * (hh + 1)]
            s = jnp.dot(kc, q_ref[0, hh], preferred_element_type=F32)
            p_buf[hh, slot] = jnp.exp2(s).astype(BF16)

    def stage_c(c, slot):
        for hh in heads:
            acc_buf[hh] += jnp.dot(v_ref[0, hh, c], p_buf[hh, slot], preferred_element_type=F32)

    acc_buf[...] = jnp.zeros(acc_buf.shape, F32)
    stage_p(0, 0)

    def trip(c0, last):
        for j in range(per_trip):
            if not (last and j == per_trip - 1):
                stage_p(c0 + j + 1, (j + 1) % 2)
            stage_c(c0 + j, j % 2)

    def body(i, carry):
        trip(per_trip * i, last=False)
        return carry

    ntrips = nchunks // per_trip
    lax.fori_loop(0, ntrips - 1, body, 0)
    trip(per_trip * (ntrips - 1), last=True)
    outs = [acc_buf[hh, :C_VDIM] / acc_buf[hh, C_VDIM:C_VDIM + 1] for hh in heads]
    o_ref[0] = jnp.concatenate(outs, axis=0).T.astype(BF16)


def _attn_mla(qt, k, vt, bounded):
    bsz, _, _, seq = qt.shape
    nchunks, tk = vt.shape[2], vt.shape[4]
    assert nchunks % 2 == 0, "both pipelined loops handle key chunks in pairs"
    tq = TQ_MLA_BOUNDED if bounded else TQ_MLA
    if bounded:
        body = _attn_mla_bounded_kernel
        scratch = [pltpu.VMEM((2, 2, tk, tq), BF16),
                   pltpu.VMEM((2, V_ROWS, tq), F32)]
    else:
        body = _attn_mla_kernel
        scratch = [pltpu.VMEM((2, 2, tk, tq), F32),
                   pltpu.VMEM((2, 2, 1, tq), F32),
                   pltpu.VMEM((2, 2, tk, tq), BF16),
                   pltpu.VMEM((2, 2, 1, tq), F32),
                   pltpu.VMEM((2, 1, tq), F32),
                   pltpu.VMEM((2, V_ROWS, tq), F32)]
    return pl.pallas_call(
        functools.partial(body, nchunks=nchunks, tk=tk),
        grid=(bsz, C_HEADS // 2, seq // tq),
        in_specs=[pl.BlockSpec((1, 2, LANES, tq), lambda b, h, i: (b, h, 0, i)),
                  pl.BlockSpec((1, seq, 2 * LANES), lambda b, h, i: (b, 0, h)),
                  pl.BlockSpec((1, 2, nchunks, V_ROWS, tk), lambda b, h, i: (b, h, 0, 0, 0))],
        out_specs=pl.BlockSpec((1, tq, 2 * C_VDIM), lambda b, h, i: (b, i, h)),
        out_shape=jax.ShapeDtypeStruct((bsz, seq, C_HEADS * C_VDIM), BF16),
        scratch_shapes=scratch,
        compiler_params=_cparams("parallel", "parallel", "arbitrary"),
        name="attn_mla_bounded" if bounded else "attn_mla",
    )(qt, k, vt)


def _rope_angles(seq, half):
    inv_freq = ROPE_THETA ** (-jnp.arange(half, dtype=F32) / half)
    return jnp.arange(seq, dtype=F32)[:, None] * inv_freq[None, :]


def _prep_ab(ab_w_in, ab_w_out, a_q_norm, a_k_norm, a_sink, b_q_norm, b_k_norm, b_rpb, seq):
    o_ka, o_va, o_qb = A_QW, A_QW + A_KVW, A_QW + 2 * A_KVW
    o_kb, o_vb = o_qb + B_W, o_qb + 2 * B_W
    w_in = jnp.concatenate(
        [ab_w_in[:, HEAD_DIM * h:HEAD_DIM * (h + 1)] for h in PERM_A]
        + [ab_w_in[:, o_qb:o_vb], ab_w_in[:, o_ka:o_qb], ab_w_in[:, o_vb:]], axis=1).astype(BF16)
    w_out = jnp.concatenate([ab_w_out[HEAD_DIM * h:HEAD_DIM * (h + 1)] for h in PERM_A]
                            + [ab_w_out[A_QW:]]).astype(BF16)
    scale = HEAD_DIM ** -0.5 * LOG2E
    gain = jnp.concatenate([jnp.tile(a_q_norm * scale, A_HEADS), jnp.tile(b_q_norm * scale, B_HEADS),
                            jnp.tile(b_k_norm, B_HEADS), jnp.tile(a_k_norm, A_KV_HEADS),
                            jnp.ones((A_KVW,), F32)])[None, :].astype(F32)
    idx = np.arange(2 * LANES) // HEAD_DIM
    gsum = jnp.asarray(idx[:, None] == idx[None, :], BF16)
    ang = _rope_angles(seq, HEAD_DIM // 2)
    cos = jnp.tile(jnp.cos(ang), (1, LANES // (HEAD_DIM // 2)))
    sin = jnp.tile(jnp.concatenate([-jnp.sin(ang), jnp.sin(ang)], axis=1), (1, LANES // HEAD_DIM))
    sink_col = jnp.repeat(a_sink[np.array(PERM_A)] * LOG2E, BLOCK).reshape(
        A_QW // LANES, 2 * BLOCK, 1).astype(F32)
    i = np.arange(2 * BLOCK)[:, None] % BLOCK
    j = np.arange(3 * BLOCK)[None, :]
    band = (j - i >= BLOCK - WINDOW) & (j - i <= BLOCK + WINDOW)
    band = np.stack([band & (j >= BLOCK), band, band & (j < 2 * BLOCK)])
    band_tab = jnp.asarray(np.where(band, 0.0, NEG_INF), F32)

    c = np.arange(GRID_W)
    cs = np.clip(c - NA_W // 2, 0, GRID_W - NA_W)
    inwin = (c[None, :] >= cs[:, None]) & (c[None, :] < cs[:, None] + NA_W)
    dc = c[None, :] - c[:, None] + NA_W - 1
    pick = jnp.asarray(dc[:, :, None] == np.arange(2 * NA_W - 1), F32)
    tcol = jnp.sum(b_rpb.astype(F32)[:, :, None, None, :] * pick[None, None], axis=-1)
    tab = jnp.stack([jnp.stack([tcol[:, NA_MAX_H - 1 - d + w] for w in range(NA_MAX_H)], axis=2)
                     for d in range(NA_MAX_H)], axis=0)
    tab = jnp.where(inwin[None, None, :, None, :], tab * LOG2E, NEG_INF)
    tab = tab.reshape(NA_MAX_H, B_HEADS // 2, 2 * GRID_W, NA_MAX_H * GRID_W)
    return w_in, w_out, gain, gsum, cos, sin, band_tab, sink_col, tab


def _bounded_tables(a_q_norm, a_k_norm, a_sink, b_q_norm, b_k_norm, b_rpb, band_tab, sink_col, bias_tab):
    scale = HEAD_DIM ** -0.5 * LOG2E
    bound_a = BF16_NORM_MARGIN * HEAD_DIM * jnp.max(jnp.abs(a_q_norm * scale)) * jnp.max(jnp.abs(a_k_norm))
    shift_a = jnp.maximum(bound_a, jnp.max(a_sink) * LOG2E)
    ok_a = bound_a + shift_a <= 2 * SHIFT_MAX
    bound_b = BF16_NORM_MARGIN * HEAD_DIM * jnp.max(jnp.abs(b_q_norm * scale)) * jnp.max(jnp.abs(b_k_norm))
    rpb = b_rpb.astype(F32) * LOG2E
    shift_b = bound_b + jnp.max(rpb)
    ok_b = bound_b + shift_b - jnp.min(rpb) <= 2 * SHIFT_MAX
    sink_term = jnp.broadcast_to(jnp.exp2(sink_col - shift_a), sink_col.shape[:2] + (LANES,))
    return ok_a & ok_b, band_tab - shift_a, sink_term, bias_tab - shift_b


def _prep_mla(c_w_in, c_q_lora_norm, c_kv_lora_norm, c_w_q_up, c_w_kv_up, c_q_norm, c_k_norm,
              c_w_out, seq):
    win = jnp.zeros((D_MODEL, Q_LORA + KV_LORA + LANES), F32)
    win = win.at[:, :Q_LORA + KV_LORA].set(c_w_in[:, :Q_LORA + KV_LORA])
    win = win.at[:, Q_LORA + KV_LORA + C_NOPE:Q_LORA + KV_LORA + C_QK].set(c_w_in[:, Q_LORA + KV_LORA:])
    wq = c_w_q_up.reshape(Q_LORA, C_HEADS, C_QK)
    wq = jnp.pad(wq, ((0, 0), (0, 0), (0, LANES - C_QK)))
    wqt = wq.reshape(Q_LORA, C_HEADS * LANES).T.astype(BF16)
    wkv = c_w_kv_up.reshape(KV_LORA, C_HEADS, C_NOPE + C_VDIM)
    wvt = wkv[:, :, C_NOPE:].reshape(KV_LORA, C_HEADS * C_VDIM).T.astype(BF16)
    wk = jnp.pad(wkv[:, :, :C_NOPE], ((0, 0), (0, 0), (0, LANES - C_NOPE)))
    wk = wk.reshape(KV_LORA, C_HEADS * LANES).astype(BF16)
    qscale = C_QK ** -0.5 * LOG2E
    shift = (BF16_NORM_MARGIN * C_QK * jnp.max(jnp.abs(c_q_norm * qscale)) * jnp.max(jnp.abs(c_k_norm)))
    shift = shift.reshape(1).astype(F32)
    gqn = jnp.pad(c_q_norm * qscale, (0, LANES - C_QK))[:, None].astype(F32)
    gkn = jnp.pad(c_k_norm, (0, LANES - C_QK))[None, :].astype(F32)
    half = C_ROPE // 2
    ang = _rope_angles(seq, half)
    cos, sin = jnp.cos(ang), jnp.sin(ang)
    zl = jnp.zeros((seq, C_NOPE), F32)
    zr = jnp.zeros((seq, LANES - C_QK), F32)
    cosk = jnp.concatenate([zl, cos, cos, zr], axis=1)
    sink = jnp.concatenate([zl, -sin, sin, zr], axis=1)
    return (shift, win.astype(BF16), c_q_lora_norm[None, :].astype(F32),
            c_kv_lora_norm[None, :].astype(F32), wqt, wvt, wk, gqn, gkn, cosk, sink, cos.T, sin.T,
            c_w_out.astype(BF16))


def _trunk(x, p):
    seq = x.shape[1]
    w_in, w_out, gain, gsum, cos, sin, band_tab, sink_col, bias_tab = _prep_ab(
        p["ab_w_in"][0], p["ab_w_out"][0], p["a_q_norm"][0], p["a_k_norm"][0], p["a_sink"][0],
        p["b_q_norm"][0], p["b_k_norm"][0], p["b_rpb"][0], seq)
    qa, ka, va, qb, kb, vb = _proj_ab(x, p["norm_mix"][0][None, :], w_in, gsum, gain, cos, sin)
    ok, band_sh, sink_term, bias_sh = _bounded_tables(
        p["a_q_norm"][0], p["a_k_norm"][0], p["a_sink"][0], p["b_q_norm"][0], p["b_k_norm"][0],
        p["b_rpb"][0], band_tab, sink_col, bias_tab)
    o = lax.cond(
        ok,
        lambda: _attn_ab(qa, ka, va, band_sh, sink_term, qb, kb, vb, bias_sh, bounded=True),
        lambda: _attn_ab(qa, ka, va, band_tab, sink_col, qb, kb, vb, bias_tab, bounded=False))
    x = _out_ffn(x, o, w_out, p["norm_ffn"][0][None, :],
                 p["ffn_w_gate"][0].astype(BF16), p["ffn_w_up"][0].astype(BF16),
                 p["ffn_w_down"][0].astype(BF16))
    (shift, win, gq, gkv, wqt, wvt, wk, gqn, gkn, cosk, sink, cosq, sinq, wo) = _prep_mla(
        p["c_w_in"][0], p["c_q_lora_norm"][0], p["c_kv_lora_norm"][0], p["c_w_q_up"][0],
        p["c_w_kv_up"][0], p["c_q_norm"][0], p["c_k_norm"][0], p["c_w_out"][0], seq)
    qt, k, vt = _proj_mla(shift, x, p["norm_mix"][1][None, :], win, gq, gkv, wqt, wvt, wk, gqn, gkn,
                          cosk, sink, cosq, sinq)
    o = lax.cond(shift[0] <= SHIFT_MAX,
                 functools.partial(_attn_mla, bounded=True),
                 functools.partial(_attn_mla, bounded=False), qt, k, vt)
    x = _out_ffn(x, o, wo, p["norm_ffn"][1][None, :], p["ffn_w_gate"][1].astype(BF16),
                 p["ffn_w_up"][1].astype(BF16), p["ffn_w_down"][1].astype(BF16))
    return x


def kernel(x_prompt, x_sample, norm_mix, norm_ffn, ab_w_in, ab_w_out, a_q_norm, a_k_norm, a_sink,
           b_q_norm, b_k_norm, b_rpb, c_w_in, c_q_lora_norm, c_kv_lora_norm, c_w_q_up, c_w_kv_up,
           c_q_norm, c_k_norm, c_w_out, ffn_w_gate, ffn_w_up, ffn_w_down):
    p = dict(norm_mix=norm_mix, norm_ffn=norm_ffn, ab_w_in=ab_w_in, ab_w_out=ab_w_out,
             a_q_norm=a_q_norm, a_k_norm=a_k_norm, a_sink=a_sink, b_q_norm=b_q_norm,
             b_k_norm=b_k_norm, b_rpb=b_rpb, c_w_in=c_w_in, c_q_lora_norm=c_q_lora_norm,
             c_kv_lora_norm=c_kv_lora_norm, c_w_q_up=c_w_q_up, c_w_kv_up=c_w_kv_up,
             c_q_norm=c_q_norm, c_k_norm=c_k_norm, c_w_out=c_w_out, ffn_w_gate=ffn_w_gate,
             ffn_w_up=ffn_w_up, ffn_w_down=ffn_w_down)
    return _trunk(x_prompt, p), _trunk(x_sample, p)
```

```python
import functools
import math

import numpy as np
import jax
import jax.numpy as jnp
from jax import lax
from jax.experimental import pallas as pl
from jax.experimental.pallas import tpu as pltpu

F32 = jnp.float32
BF16 = jnp.bfloat16

D_MODEL = 1024
GRID_W = 64
HEAD_DIM = 64
ROPE_THETA = 10000.0
EPS = 1e-6
NEG_INF = -1e30
BLOCK = 128
A_HEADS = 8
A_KV_HEADS = 2
WINDOW = 128
B_HEADS = 8
NA_MAX_H = 8
NA_W = 16
C_HEADS = 16
C_NOPE = 64
C_ROPE = 32
C_VDIM = 64
C_QK = C_NOPE + C_ROPE
Q_LORA = 384
KV_LORA = 256
A_QW = A_HEADS * HEAD_DIM
A_KVW = A_KV_HEADS * HEAD_DIM
B_W = B_HEADS * HEAD_DIM

LANES = 128
V7X_VMEM_BYTES = 64 * 1024 * 1024
VMEM_LIMIT = V7X_VMEM_BYTES * 3 // 4

TM_PROJ = 512
TM_FFN = 512
TQ_MLA = 1024
TQ_MLA_BOUNDED = 2048
MLA_CHUNKS_PER_TRIP = 8
NA_ROWS = 8
A_QBLOCKS = 4
V_ROWS = 2 * C_VDIM

PERM_A = (0, 4, 1, 5, 2, 6, 3, 7)

LOG2E = math.log2(math.e)
SHIFT_MAX = 60.0
BF16_NORM_MARGIN = 1.0 + 2.0 ** -6
NT_DIMS = (((1,), (1,)), ((), ()))


def _cparams(*sem):
    return pltpu.CompilerParams(dimension_semantics=sem, vmem_limit_bytes=VMEM_LIMIT)


def _resident(shape):
    nd = len(shape)
    return pl.BlockSpec(shape, lambda *_: (0,) * nd, pipeline_mode=pl.Buffered(1))


def _rms_rows(x, gain):
    ms = jnp.mean(x * x, axis=-1, keepdims=True)
    return x * lax.rsqrt(ms + EPS) * gain


def _proj_ab_kernel(x_ref, g_ref, w_ref, gsum_ref, gain_ref, cos_ref, sin_ref,
                    qa_ref, ka_ref, va_ref, qb_ref, kb_ref, vb_ref):
    h = _rms_rows(x_ref[0], g_ref[...]).astype(BF16)
    y = jnp.dot(h, w_ref[...], preferred_element_type=F32)
    gsum = gsum_ref[...]
    cos = cos_ref[...]
    sin = sin_ref[...]
    lane = lax.broadcasted_iota(jnp.int32, cos.shape, 1)
    first_half = (lane % HEAD_DIM) < (HEAD_DIM // 2)

    def head_norm(c):
        yc = y[:, 2 * LANES * c:2 * LANES * (c + 1)]
        ss = jnp.dot((yc * yc).astype(BF16), gsum, preferred_element_type=F32)
        return yc * lax.rsqrt(ss * (1.0 / HEAD_DIM) + EPS) * gain_ref[:, 2 * LANES * c:2 * LANES * (c + 1)]

    def rope(v):
        swapped = jnp.where(first_half, pltpu.roll(v, LANES - HEAD_DIM // 2, 1),
                            pltpu.roll(v, HEAD_DIM // 2, 1))
        return v * cos + swapped * sin

    for c in range(2):
        yn = head_norm(c)
        for b in range(2):
            qa_ref[0, :, LANES * (2 * c + b):LANES * (2 * c + b + 1)] = rope(
                yn[:, LANES * b:LANES * (b + 1)]).astype(BF16)
    for c in range(2):
        qb_ref[0, :, 2 * LANES * c:2 * LANES * (c + 1)] = head_norm(2 + c).astype(BF16)
    for c in range(2):
        kb_ref[0, :, 2 * LANES * c:2 * LANES * (c + 1)] = head_norm(4 + c).astype(BF16)
    ka_ref[0] = rope(head_norm(6)[:, :LANES]).astype(BF16)
    va_ref[0] = y[:, 13 * LANES:14 * LANES].astype(BF16)
    vb_ref[0] = y[:, 14 * LANES:18 * LANES].astype(BF16)


def _proj_ab(x, g, w, gsum, gain, cos, sin):
    bsz, seq, _ = x.shape
    tm = TM_PROJ
    nt = seq // tm
    tok = lambda width: pl.BlockSpec((1, tm, width), lambda b, i: (b, i, 0))
    tab = pl.BlockSpec((tm, LANES), lambda b, i: (i, 0))
    out = lambda width: jax.ShapeDtypeStruct((bsz, seq, width), BF16)
    return pl.pallas_call(
        _proj_ab_kernel,
        grid=(bsz, nt),
        in_specs=[tok(D_MODEL), _resident(g.shape), _resident(w.shape), _resident(gsum.shape),
                  _resident(gain.shape), tab, tab],
        out_specs=[tok(A_QW), tok(A_KVW), tok(A_KVW), tok(B_W), tok(B_W), tok(B_W)],
        out_shape=[out(A_QW), out(A_KVW), out(A_KVW), out(B_W), out(B_W), out(B_W)],
        compiler_params=_cparams("parallel", "parallel"),
        name="proj_ab",
    )(x, g, w, gsum, gain, cos, sin)


def _attn_a_body(q_ref, kp_ref, kc_ref, kn_ref, vp_ref, vc_ref, vn_ref, bias_ref, sink_ref, o_ref,
                 *, nsteps, lane0, bounded):
    n = pl.program_id(1)
    k = jnp.concatenate([kp_ref[0], kc_ref[0], kn_ref[0]], axis=0)
    v = jnp.concatenate([vp_ref[0], vc_ref[0], vn_ref[0]], axis=0)
    vext = jnp.concatenate([v, jnp.ones(v.shape, BF16)], axis=1)
    lane = lax.broadcasted_iota(jnp.int32, (BLOCK, LANES), 1)
    lo = lane < HEAD_DIM
    zero = jnp.zeros((BLOCK, LANES), BF16)
    for t in range(A_QBLOCKS):
        if t == 0:
            bias = bias_ref[jnp.where(n == 0, 0, 1)]
        elif t == A_QBLOCKS - 1:
            bias = bias_ref[jnp.where(n == nsteps - 1, 2, 1)]
        else:
            bias = bias_ref[1]
        kw = k[BLOCK * t:BLOCK * (t + 3)]
        vw = vext[BLOCK * t:BLOCK * (t + 3)]
        for blk in range(A_QW // LANES):
            qp = q_ref[0, BLOCK * t:BLOCK * (t + 1), LANES * blk:LANES * (blk + 1)]
            qm = jnp.concatenate([jnp.where(lo, qp, zero), jnp.where(lo, zero, qp)], axis=0)
            s = lax.dot_general(qm, kw, NT_DIMS, preferred_element_type=F32) + bias
            if bounded:
                p = jnp.exp2(s)
                sink_term = sink_ref[blk]
            else:
                sink = sink_ref[blk]
                m = jnp.maximum(jnp.max(s, axis=-1, keepdims=True), sink)
                p = jnp.exp2(s - m)
                sink_term = jnp.exp2(sink - m)
            oe = jnp.dot(p.astype(BF16), vw, preferred_element_type=F32)
            o = oe[:, :LANES] / (oe[:, LANES:] + sink_term)
            o_ref[0, BLOCK * t:BLOCK * (t + 1), lane0 + LANES * blk:lane0 + LANES * (blk + 1)] = (
                jnp.where(lo, o[:BLOCK], o[BLOCK:]).astype(BF16))


def _attn_b_body(q_ref, kp_ref, kc_ref, kn_ref, vp_ref, vc_ref, vn_ref, bias_ref, o_ref,
                 kbuf, vbuf, *, nrb, lane0, bounded):
    rb = pl.program_id(1)
    blk_tok = NA_ROWS * GRID_W
    half_tok = blk_tok // 2
    win_tok = NA_MAX_H * GRID_W
    kbuf[0:half_tok] = kp_ref[0, half_tok:blk_tok]
    kbuf[half_tok:half_tok + blk_tok] = kc_ref[0]
    kbuf[half_tok + blk_tok:2 * blk_tok] = kn_ref[0, 0:half_tok]
    vbuf[0:half_tok] = vp_ref[0, half_tok:blk_tok]
    vbuf[half_tok:half_tok + blk_tok] = vc_ref[0]
    vbuf[half_tok + blk_tok:2 * blk_tok] = vn_ref[0, 0:half_tok]
    lane = lax.broadcasted_iota(jnp.int32, (GRID_W, LANES), 1)
    lo = lane < HEAD_DIM
    zero = jnp.zeros((GRID_W, LANES), BF16)
    ones = jnp.ones((win_tok, LANES), BF16)
    mid = NA_MAX_H // 2
    for t in range(NA_ROWS):
        off = jnp.where(rb == 0, max(t, mid), jnp.where(rb == nrb - 1, min(t, mid), t))
        didx = t + mid - off
        kstart = pl.multiple_of(off * GRID_W, GRID_W)
        for blk in range(B_W // LANES):
            qp = q_ref[0, GRID_W * t:GRID_W * (t + 1), LANES * blk:LANES * (blk + 1)]
            qm = jnp.concatenate([jnp.where(lo, qp, zero), jnp.where(lo, zero, qp)], axis=0)
            kw = kbuf[pl.ds(kstart, win_tok), LANES * blk:LANES * (blk + 1)]
            vw = vbuf[pl.ds(kstart, win_tok), LANES * blk:LANES * (blk + 1)]
            s = lax.dot_general(qm, kw, NT_DIMS, preferred_element_type=F32)
            s = s + bias_ref[didx, blk]
            p = jnp.exp2(s) if bounded else jnp.exp2(s - jnp.max(s, axis=-1, keepdims=True))
            oe = jnp.dot(p.astype(BF16), jnp.concatenate([vw, ones], axis=1),
                         preferred_element_type=F32)
            o = oe[:, :LANES] / oe[:, LANES:]
            o_ref[0, GRID_W * t:GRID_W * (t + 1), lane0 + LANES * blk:lane0 + LANES * (blk + 1)] = (
                jnp.where(lo, o[:GRID_W], o[GRID_W:]).astype(BF16))


def _attn_ab_kernel(*refs, nsteps, bounded):
    a_refs, b_refs, o_ref, scratch = refs[:9], refs[9:17], refs[17], refs[18:]
    _attn_a_body(*a_refs, o_ref, nsteps=nsteps, lane0=0, bounded=bounded)
    _attn_b_body(*b_refs, o_ref, *scratch, nrb=nsteps, lane0=A_QW, bounded=bounded)


def _attn_ab(qa, ka, va, band_tab, sink_col, qb, kb, vb, bias_tab, bounded):
    bsz, seq, _ = qa.shape
    tok = A_QBLOCKS * BLOCK
    assert tok == NA_ROWS * GRID_W
    nsteps = seq // tok
    assert nsteps >= 2, "neighbourhood attention needs at least two row blocks"
    nb = seq // BLOCK
    cur = lambda b, n: (b, n, 0)
    a_edge = lambda f: pl.BlockSpec((1, BLOCK, A_KVW), f)
    a_prev = lambda b, n: (b, jnp.maximum(A_QBLOCKS * n - 1, 0), 0)
    a_next = lambda b, n: (b, jnp.minimum(A_QBLOCKS * (n + 1), nb - 1), 0)
    a_mid = pl.BlockSpec((1, tok, A_KVW), cur)
    b_blk = lambda f: pl.BlockSpec((1, tok, B_W), f)
    b_prev = lambda b, n: (b, jnp.maximum(n - 1, 0), 0)
    b_next = lambda b, n: (b, jnp.minimum(n + 1, nsteps - 1), 0)
    return pl.pallas_call(
        functools.partial(_attn_ab_kernel, nsteps=nsteps, bounded=bounded),
        grid=(bsz, nsteps),
        in_specs=[pl.BlockSpec((1, tok, A_QW), cur), a_edge(a_prev), a_mid, a_edge(a_next),
                  a_edge(a_prev), a_mid, a_edge(a_next), _resident(band_tab.shape),
                  _resident(sink_col.shape),
                  b_blk(cur), b_blk(b_prev), b_blk(cur), b_blk(b_next),
                  b_blk(b_prev), b_blk(cur), b_blk(b_next), _resident(bias_tab.shape)],
        out_specs=pl.BlockSpec((1, tok, A_QW + B_W), cur),
        out_shape=jax.ShapeDtypeStruct((bsz, seq, A_QW + B_W), BF16),
        scratch_shapes=[pltpu.VMEM((2 * tok, B_W), BF16), pltpu.VMEM((2 * tok, B_W), BF16)],
        compiler_params=_cparams("parallel", "parallel"),
        name="attn_ab_bounded" if bounded else "attn_ab",
    )(qa, ka, ka, ka, va, va, va, band_tab, sink_col, qb, kb, kb, kb, vb, vb, vb, bias_tab)


def _out_ffn_kernel(x_ref, o_ref, wo_ref, g_ref, wg_ref, wu_ref, wd_ref, y_ref):
    x1 = x_ref[0] + jnp.dot(o_ref[0], wo_ref[...], preferred_element_type=F32)
    h = (x1 * g_ref[...]).astype(BF16)
    inv = lax.rsqrt(jnp.mean(x1 * x1, axis=-1, keepdims=True) + EPS)
    gate = jnp.dot(h, wg_ref[...], preferred_element_type=F32) * inv
    up = jnp.dot(h, wu_ref[...], preferred_element_type=F32) * inv
    act =(gate / (1.0 + jnp.exp(-gate)) * up).astype(BF16)
    y_ref[0] = x1 + jnp.dot(act, wd_ref[...], preferred_element_type=F32)


def _out_ffn(x, o, wo, g, wg, wu, wd):
    bsz, seq, _ = x.shape
    tm = TM_FFN
    tok = lambda width: pl.BlockSpec((1, tm, width), lambda b, i: (b, i, 0))
    consts = (wo, g, wg, wu, wd)
    return pl.pallas_call(
        _out_ffn_kernel,
        grid=(bsz, seq // tm),
        in_specs=[tok(D_MODEL), tok(o.shape[-1]), *[_resident(c.shape) for c in consts]],
        out_specs=tok(D_MODEL),
        out_shape=jax.ShapeDtypeStruct(x.shape, F32),
        compiler_params=_cparams("parallel", "parallel"),
        name="out_ffn",
    )(x, o, *consts)


def _proj_mla_kernel(shift_ref, x_ref, g_ref, win_ref, gq_ref, gkv_ref, wqt_ref, wvt_ref, wk_ref,
                     gqn_ref, gkn_ref, cosk_ref, sink_ref, cosq_ref, sinq_ref,
                     q_ref, k_ref, v_ref):
    tm = x_ref.shape[1]
    shift = shift_ref[0]
    h = _rms_rows(x_ref[0], g_ref[...]).astype(BF16)
    y = jnp.dot(h, win_ref[...], preferred_element_type=F32)
    cq = _rms_rows(y[:, :Q_LORA], gq_ref[...]).astype(BF16)
    ckv = _rms_rows(y[:, Q_LORA:Q_LORA + KV_LORA], gkv_ref[...]).astype(BF16)
    kpe = y[:, Q_LORA + KV_LORA:]

    qt = lax.dot_general(wqt_ref[...], cq, NT_DIMS, preferred_element_type=F32)
    qt = qt.reshape(C_HEADS, LANES, tm)[:, :C_QK]
    ssq = jnp.sum(qt * qt, axis=1, keepdims=True)
    qn = qt * lax.rsqrt(ssq * (1.0 / C_QK) + EPS) * gqn_ref[:C_QK][None]
    half = C_ROPE // 2
    r1 = qn[:, C_NOPE:C_NOPE + half]
    r2 = qn[:, C_NOPE + half:C_QK]
    cos = cosq_ref[...][None]
    sin = sinq_ref[...][None]
    pad_row = lax.broadcasted_iota(jnp.int32, (C_HEADS, LANES - C_QK, tm), 1)
    q_out = jnp.concatenate([qn[:, :C_NOPE], r1 * cos - r2 * sin, r2 * cos + r1 * sin,
                             jnp.where(pad_row == 0, -shift, 0.0)], axis=1)
    q_ref[0] = q_out.astype(BF16)

    vt = lax.dot_general(wvt_ref[...], ckv, NT_DIMS, preferred_element_type=F32)
    vt = vt.reshape(C_HEADS, C_VDIM, tm).astype(BF16)
    row = lax.broadcasted_iota(jnp.int32, (C_HEADS, V_ROWS - C_VDIM, tm), 1)
    v_ref[0, :, 0] = jnp.concatenate([vt, jnp.where(row == 0, 1.0, 0.0).astype(BF16)], axis=1)

    kn = jnp.dot(ckv, wk_ref[...], preferred_element_type=F32)
    gk = gkn_ref[...]
    ss_pe = jnp.sum(kpe * kpe, axis=-1, keepdims=True)
    kg = kpe * gk
    lane = lax.broadcasted_iota(jnp.int32, kg.shape, 1)
    swapped = jnp.where(lane < C_NOPE + half, pltpu.roll(kg, LANES - half, 1),
                        pltpu.roll(kg, half, 1))
    kr = kg * cosk_ref[...] + swapped * sink_ref[...]
    for hd in range(C_HEADS):
        kh = kn[:, LANES * hd:LANES * (hd + 1)]
        ss = jnp.sum(kh * kh, axis=-1, keepdims=True) + ss_pe
        kval = (kh * gk + kr) * lax.rsqrt(ss * (1.0 / C_QK) + EPS)
        k_ref[0, :, LANES * hd:LANES * (hd + 1)] = jnp.where(lane == C_QK, 1.0, kval).astype(BF16)


def _proj_mla(shift, x, g, win, gq, gkv, wqt, wvt, wk, gqn, gkn, cosk, sink, cosq, sinq):
    bsz, seq, _ = x.shape
    tm = TM_PROJ
    nt = seq // tm
    half = C_ROPE // 2
    return pl.pallas_call(
        _proj_mla_kernel,
        grid=(bsz, nt),
        in_specs=[pl.BlockSpec(memory_space=pltpu.SMEM),
                  pl.BlockSpec((1, tm, D_MODEL), lambda b, i: (b, i, 0)),
                  _resident(g.shape), _resident(win.shape), _resident(gq.shape),
                  _resident(gkv.shape), _resident(wqt.shape), _resident(wvt.shape),
                  _resident(wk.shape), _resident(gqn.shape), _resident(gkn.shape),
                  pl.BlockSpec((tm, LANES), lambda b, i: (i, 0)),
                  pl.BlockSpec((tm, LANES), lambda b, i: (i, 0)),
                  pl.BlockSpec((half, tm), lambda b, i: (0, i)),
                  pl.BlockSpec((half, tm), lambda b, i: (0, i))],
        out_specs=[pl.BlockSpec((1, C_HEADS, LANES, tm), lambda b, i: (b, 0, 0, i)),
                   pl.BlockSpec((1, tm, C_HEADS * LANES), lambda b, i: (b, i, 0)),
                   pl.BlockSpec((1, C_HEADS, 1, V_ROWS, tm), lambda b, i: (b, 0, i, 0, 0))],
        out_shape=[jax.ShapeDtypeStruct((bsz, C_HEADS, LANES, seq), BF16),
                   jax.ShapeDtypeStruct((bsz, seq, C_HEADS * LANES), BF16),
                   jax.ShapeDtypeStruct((bsz, C_HEADS, nt, V_ROWS, tm), BF16)],
        compiler_params=_cparams("parallel", "parallel"),
        name="proj_mla",
    )(shift, x, g, win, gq, gkv, wqt, wvt, wk, gqn, gkn, cosk, sink, cosq, sinq)


def _attn_mla_kernel(q_ref, k_ref, v_ref, o_ref, s_buf, cmax_buf, p_buf, alpha_buf, m_buf, acc_buf,
                     *, nchunks, tk):
    heads = range(2)

    def stage_a(c, slot):
        start = pl.multiple_of(c * tk, tk)
        for hh in heads:
            kc = k_ref[0, pl.ds(start, tk), LANES * hh:LANES * (hh + 1)]
            s = jnp.dot(kc, q_ref[0, hh], preferred_element_type=F32)
            s_buf[hh, slot] = s
            cmax_buf[hh, slot] = jnp.max(s, axis=0, keepdims=True)

    def stage_b(slot):
        for hh in heads:
            m_old = m_buf[hh]
            m_new = jnp.maximum(m_old, cmax_buf[hh, slot])
            alpha_buf[hh, slot] = jnp.exp2(m_old - m_new)
            m_buf[hh] = m_new
            p_buf[hh, slot] = jnp.exp2(s_buf[hh, slot] - m_new).astype(BF16)

    def stage_c(c, slot):
        for hh in heads:
            pv = jnp.dot(v_ref[0, hh, c], p_buf[hh, slot], preferred_element_type=F32)
            acc_buf[hh] = alpha_buf[hh, slot] * acc_buf[hh] + pv

    m_buf[...] = jnp.full(m_buf.shape, NEG_INF, F32)
    acc_buf[...] = jnp.zeros(acc_buf.shape, F32)
    p_buf[:, 1] = jnp.zeros(p_buf.shape[:1] + p_buf.shape[2:], BF16)
    alpha_buf[:, 1] = jnp.ones(alpha_buf.shape[:1] + alpha_buf.shape[2:], F32)
    stage_a(0, 0)

    def body(i, carry):
        c0 = 2 * i
        stage_b(0)
        stage_a(c0 + 1, 1)
        stage_c(jnp.maximum(c0 - 1, 0), 1)
        stage_b(1)
        stage_a(jnp.minimum(c0 + 2, nchunks - 1), 0)
        stage_c(c0, 0)
        return carry

    lax.fori_loop(0, nchunks // 2, body, 0)
    stage_c(nchunks - 1, 1)
    outs = [acc_buf[hh, :C_VDIM] / acc_buf[hh, C_VDIM:C_VDIM + 1] for hh in heads]
    o_ref[0] = jnp.concatenate(outs, axis=0).T.astype(BF16)


def _attn_mla_bounded_kernel(q_ref, k_ref, v_ref, o_ref, p_buf, acc_buf, *, nchunks, tk):
    heads = range(2)
    per_trip = math.gcd(nchunks, MLA_CHUNKS_PER_TRIP)
    assert per_trip % 2 == 0, "chunk c lives in buffer slot c % 2 of every trip"

    def stage_p(c, slot):
        start = pl.multiple_of(c * tk, tk)
        for hh in heads:
            kc = k_ref[0, pl.ds(start, tk), LANES * hh:LANES * (hh + 1)]
            s = jnp.dot(kc, q_ref[0, hh], preferred_element_type=F32)
            p_buf[hh, slot] = jnp.exp2(s).astype(BF16)

    def stage_c(c, slot):
        for hh in heads:
            acc_buf[hh] += jnp.dot(v_ref[0, hh, c], p_buf[hh, slot], preferred_element_type=F32)

    acc_buf[...] = jnp.zeros(acc_buf.shape, F32)
    stage_p(0, 0)

    def trip(c0, last):
        for j in range(per_trip):
            if not (last and j == per_trip - 1):
                stage_p(c0 + j + 1, (j + 1) % 2)
            stage_c(c0 + j, j % 2)

    def body(i, carry):
        trip(per_trip * i, last=False)
        return carry

    ntrips = nchunks // per_trip
    lax.fori_loop(0, ntrips - 1, body, 0)
    trip(per_trip * (ntrips - 1), last=True)
    outs = [acc_buf[hh, :C_VDIM] / acc_buf[hh, C_VDIM:C_VDIM + 1] for hh in heads]
    o_ref[0] = jnp.concatenate(outs, axis=0).T.astype(BF16)


def _attn_mla(qt, k, vt, bounded):
    bsz, _, _, seq = qt.shape
    nchunks, tk = vt.shape[2], vt.shape[4]
    assert nchunks % 2 == 0, "both pipelined loops handle key chunks in pairs"
    tq = TQ_MLA_BOUNDED if bounded else TQ_MLA
    if bounded:
        body = _attn_mla_bounded_kernel
        scratch = [pltpu.VMEM((2, 2, tk, tq), BF16),
                   pltpu.VMEM((2, V_ROWS, tq), F32)]
    else:
        body = _attn_mla_kernel
        scratch = [pltpu.VMEM((2, 2, tk, tq), F32),
                   pltpu.VMEM((2, 2, 1, tq), F32),
                   pltpu.VMEM((2, 2, tk, tq), BF16),
                   pltpu.VMEM((2, 2, 1, tq), F32),
                   pltpu.VMEM((2, 1, tq), F32),
                   pltpu.VMEM((2, V_ROWS, tq), F32)]
    return pl.pallas_call(
        functools.partial(body, nchunks=nchunks, tk=tk),
        grid=(bsz, C_HEADS // 2, seq // tq),
        in_specs=[pl.BlockSpec((1, 2, LANES, tq), lambda b, h, i: (b, h, 0, i)),
                  pl.BlockSpec((1, seq, 2 * LANES), lambda b, h, i: (b, 0, h)),
                  pl.BlockSpec((1, 2, nchunks, V_ROWS, tk), lambda b, h, i: (b, h, 0, 0, 0))],
        out_specs=pl.BlockSpec((1, tq, 2 * C_VDIM), lambda b, h, i: (b, i, h)),
        out_shape=jax.ShapeDtypeStruct((bsz, seq, C_HEADS * C_VDIM), BF16),
        scratch_shapes=scratch,
        compiler_params=_cparams("parallel", "parallel", "arbitrary"),
        name="attn_mla_bounded" if bounded else "attn_mla",
    )(qt, k, vt)


def _rope_angles(seq, half):
    inv_freq = ROPE_THETA ** (-jnp.arange(half, dtype=F32) / half)
    return jnp.arange(seq, dtype=F32)[:, None] * inv_freq[None, :]


def _prep_ab(ab_w_in, ab_w_out, a_q_norm, a_k_norm, a_sink, b_q_norm, b_k_norm, b_rpb, seq):
    o_ka, o_va, o_qb = A_QW, A_QW + A_KVW, A_QW + 2 * A_KVW
    o_kb, o_vb = o_qb + B_W, o_qb + 2 * B_W
    w_in = jnp.concatenate(
        [ab_w_in[:, HEAD_DIM * h:HEAD_DIM * (h + 1)] for h in PERM_A]
        + [ab_w_in[:, o_qb:o_vb], ab_w_in[:, o_ka:o_qb], ab_w_in[:, o_vb:]], axis=1).astype(BF16)
    w_out = jnp.concatenate([ab_w_out[HEAD_DIM * h:HEAD_DIM * (h + 1)] for h in PERM_A]
                            + [ab_w_out[A_QW:]]).astype(BF16)
    scale = HEAD_DIM ** -0.5 * LOG2E
    gain = jnp.concatenate([jnp.tile(a_q_norm * scale, A_HEADS), jnp.tile(b_q_norm * scale, B_HEADS),
                            jnp.tile(b_k_norm, B_HEADS), jnp.tile(a_k_norm, A_KV_HEADS),
                            jnp.ones((A_KVW,), F32)])[None, :].astype(F32)
    idx = np.arange(2 * LANES) // HEAD_DIM
    gsum = jnp.asarray(idx[:, None] == idx[None, :], BF16)
    ang = _rope_angles(seq, HEAD_DIM // 2)
    cos = jnp.tile(jnp.cos(ang), (1, LANES // (HEAD_DIM // 2)))
    sin = jnp.tile(jnp.concatenate([-jnp.sin(ang), jnp.sin(ang)], axis=1), (1, LANES // HEAD_DIM))
    sink_col = jnp.repeat(a_sink[np.array(PERM_A)] * LOG2E, BLOCK).reshape(
        A_QW // LANES, 2 * BLOCK, 1).astype(F32)
    i = np.arange(2 * BLOCK)[:, None] % BLOCK
    j = np.arange(3 * BLOCK)[None, :]
    band = (j - i >= BLOCK - WINDOW) & (j - i <= BLOCK + WINDOW)
    band = np.stack([band & (j >= BLOCK), band, band & (j < 2 * BLOCK)])
    band_tab = jnp.asarray(np.where(band, 0.0, NEG_INF), F32)

    c = np.arange(GRID_W)
    cs = np.clip(c - NA_W // 2, 0, GRID_W - NA_W)
    inwin = (c[None, :] >= cs[:, None]) & (c[None, :] < cs[:, None] + NA_W)
    dc = c[None, :] - c[:, None] + NA_W - 1
    pick = jnp.asarray(dc[:, :, None] == np.arange(2 * NA_W - 1), F32)
    tcol = jnp.sum(b_rpb.astype(F32)[:, :, None, None, :] * pick[None, None], axis=-1)
    tab = jnp.stack([jnp.stack([tcol[:, NA_MAX_H - 1 - d + w] for w in range(NA_MAX_H)], axis=2)
                     for d in range(NA_MAX_H)], axis=0)
    tab = jnp.where(inwin[None, None, :, None, :], tab * LOG2E, NEG_INF)
    tab = tab.reshape(NA_MAX_H, B_HEADS // 2, 2 * GRID_W, NA_MAX_H * GRID_W)
    return w_in, w_out, gain, gsum, cos, sin, band_tab, sink_col, tab


def _bounded_tables(a_q_norm, a_k_norm, a_sink, b_q_norm, b_k_norm, b_rpb, band_tab, sink_col, bias_tab):
    scale = HEAD_DIM ** -0.5 * LOG2E
    bound_a = BF16_NORM_MARGIN * HEAD_DIM * jnp.max(jnp.abs(a_q_norm * scale)) * jnp.max(jnp.abs(a_k_norm))
    shift_a = jnp.maximum(bound_a, jnp.max(a_sink) * LOG2E)
    ok_a = bound_a + shift_a <= 2 * SHIFT_MAX
    bound_b = BF16_NORM_MARGIN * HEAD_DIM * jnp.max(jnp.abs(b_q_norm * scale)) * jnp.max(jnp.abs(b_k_norm))
    rpb = b_rpb.astype(F32) * LOG2E
    shift_b = bound_b + jnp.max(rpb)
    ok_b = bound_b + shift_b - jnp.min(rpb) <= 2 * SHIFT_MAX
    sink_term = jnp.broadcast_to(jnp.exp2(sink_col - shift_a), sink_col.shape[:2] + (LANES,))
    return ok_a & ok_b, band_tab - shift_a, sink_term, bias_tab - shift_b


def _prep_mla(c_w_in, c_q_lora_norm, c_kv_lora_norm, c_w_q_up, c_w_kv_up, c_q_norm, c_k_norm,
              c_w_out, seq):
    win = jnp.zeros((D_MODEL, Q_LORA + KV_LORA + LANES), F32)
    win = win.at[:, :Q_LORA + KV_LORA].set(c_w_in[:, :Q_LORA + KV_LORA])
    win = win.at[:, Q_LORA + KV_LORA + C_NOPE:Q_LORA + KV_LORA + C_QK].set(c_w_in[:, Q_LORA + KV_LORA:])
    wq = c_w_q_up.reshape(Q_LORA, C_HEADS, C_QK)
    wq = jnp.pad(wq, ((0, 0), (0, 0), (0, LANES - C_QK)))
    wqt = wq.reshape(Q_LORA, C_HEADS * LANES).T.astype(BF16)
    wkv = c_w_kv_up.reshape(KV_LORA, C_HEADS, C_NOPE + C_VDIM)
    wvt = wkv[:, :, C_NOPE:].reshape(KV_LORA, C_HEADS * C_VDIM).T.astype(BF16)
    wk = jnp.pad(wkv[:, :, :C_NOPE], ((0, 0), (0, 0), (0, LANES - C_NOPE)))
    wk = wk.reshape(KV_LORA, C_HEADS * LANES).astype(BF16)
    qscale = C_QK ** -0.5 * LOG2E
    shift = (BF16_NORM_MARGIN * C_QK * jnp.max(jnp.abs(c_q_norm * qscale)) * jnp.max(jnp.abs(c_k_norm)))
    shift = shift.reshape(1).astype(F32)
    gqn = jnp.pad(c_q_norm * qscale, (0, LANES - C_QK))[:, None].astype(F32)
    gkn = jnp.pad(c_k_norm, (0, LANES - C_QK))[None, :].astype(F32)
    half = C_ROPE // 2
    ang = _rope_angles(seq, half)
    cos, sin = jnp.cos(ang), jnp.sin(ang)
    zl = jnp.zeros((seq, C_NOPE), F32)
    zr = jnp.zeros((seq, LANES - C_QK), F32)
    cosk = jnp.concatenate([zl, cos, cos, zr], axis=1)
    sink = jnp.concatenate([zl, -sin, sin, zr], axis=1)
    return (shift, win.astype(BF16), c_q_lora_norm[None, :].astype(F32),
            c_kv_lora_norm[None, :].astype(F32), wqt, wvt, wk, gqn, gkn, cosk, sink, cos.T, sin.T,
            c_w_out.astype(BF16))


def _trunk(x, p):
    seq = x.shape[1]
    w_in, w_out, gain, gsum, cos, sin, band_tab, sink_col, bias_tab = _prep_ab(
        p["ab_w_in"][0], p["ab_w_out"][0], p["a_q_norm"][0], p["a_k_norm"][0], p["a_sink"][0],
        p["b_q_norm"][0], p["b_k_norm"][0], p["b_rpb"][0], seq)
    qa, ka, va, qb, kb, vb = _proj_ab(x, p["norm_mix"][0][None, :], w_in, gsum, gain, cos, sin)
    ok, band_sh, sink_term, bias_sh = _bounded_tables(
        p["a_q_norm"][0], p["a_k_norm"][0], p["a_sink"][0], p["b_q_norm"][0], p["b_k_norm"][0],
        p["b_rpb"][0], band_tab, sink_col, bias_tab)
    o = lax.cond(
        ok,
        lambda: _attn_ab(qa, ka, va, band_sh, sink_term, qb, kb, vb, bias_sh, bounded=True),
        lambda: _attn_ab(qa, ka, va, band_tab, sink_col, qb, kb, vb, bias_tab, bounded=False))
    x = _out_ffn(x, o, w_out, p["norm_ffn"][0][None, :],
                 p["ffn_w_gate"][0].astype(BF16), p["ffn_w_up"][0].astype(BF16),
                 p["ffn_w_down"][0].astype(BF16))
    (shift, win, gq, gkv, wqt, wvt, wk, gqn, gkn, cosk, sink, cosq, sinq, wo) = _prep_mla(
        p["c_w_in"][0], p["c_q_lora_norm"][0], p["c_kv_lora_norm"][0], p["c_w_q_up"][0],
        p["c_w_kv_up"][0], p["c_q_norm"][0], p["c_k_norm"][0], p["c_w_out"][0], seq)
    qt, k, vt = _proj_mla(shift, x, p["norm_mix"][1][None, :], win, gq, gkv, wqt, wvt, wk, gqn, gkn,
                          cosk, sink, cosq, sinq)
    o = lax.cond(shift[0] <= SHIFT_MAX,
                 functools.partial(_attn_mla, bounded=True),
                 functools.partial(_attn_mla, bounded=False), qt, k, vt)
    x = _out_ffn(x, o, wo, p["norm_ffn"][1][None, :], p["ffn_w_gate"][1].astype(BF16),
                 p["ffn_w_up"][1].astype(BF16), p["ffn_w_down"][1].astype(BF16))
    return x


def kernel(x_prompt, x_sample, norm_mix, norm_ffn, ab_w_in, ab_w_out, a_q_norm, a_k_norm, a_sink,
           b_q_norm, b_k_norm, b_rpb, c_w_in, c_q_lora_norm, c_kv_lora_norm, c_w_q_up, c_w_kv_up,
           c_q_norm, c_k_norm, c_w_out, ffn_w_gate, ffn_w_up, ffn_w_down):
    p = dict(norm_mix=norm_mix, norm_ffn=norm_ffn, ab_w_in=ab_w_in, ab_w_out=ab_w_out,
             a_q_norm=a_q_norm, a_k_norm=a_k_norm, a_sink=a_sink, b_q_norm=b_q_norm,
             b_k_norm=b_k_norm, b_rpb=b_rpb, c_w_in=c_w_in, c_q_lora_norm=c_q_lora_norm,
             c_kv_lora_norm=c_kv_lora_norm, c_w_q_up=c_w_q_up, c_w_kv_up=c_w_kv_up,
             c_q_norm=c_q_norm, c_k_norm=c_k_norm, c_w_out=c_w_out, ffn_w_gate=ffn_w_gate,
             ffn_w_up=ffn_w_up, ffn_w_down=ffn_w_down)
    return _trunk(x_prompt, p), _trunk(x_sample, p)
```
